```python
import math
import jax
import jax.numpy as jnp
from jax import lax
import numpy as np

D_MODEL = 1024
BATCH = 8
SEQ = 2048
DEPTH = 1

CHUNK = 64
Q_BLOCK = 128
GDN_HEADS = D_MODEL // 256
GDN_DK = 128
GDN_DV = 128
GDN_WIDTH = GDN_HEADS * GDN_DV
FOX_HEADS = D_MODEL // 128
FOX_DH = 64
FOX_WIDTH = FOX_HEADS * FOX_DH
CONV_W = 4
D_FF = 4 * D_MODEL
D_PLE = 256
LN_EPS = 1e-5
NORM_EPS = 1e-6
ALPHA = (2.0 * DEPTH) ** 0.25
BETA_INIT = (8.0 * DEPTH) ** -0.25

GDN_QK = GDN_HEADS * GDN_DK
GDN_QKV = 2 * GDN_QK + GDN_WIDTH
OFF_Z = GDN_QKV
OFF_BETA = OFF_Z + GDN_WIDTH
OFF_A = OFF_BETA + GDN_HEADS
OFF_FOX = OFF_A + GDN_HEADS
OFF_F = OFF_FOX + 3 * FOX_WIDTH
D_IN = OFF_F + FOX_HEADS

kernel_name = 'hybrid_gdn_fox_deepnorm_block'


def _layer_norm(x, g, b):
    xf = x.astype(jnp.float32)
    mu = jnp.mean(xf, -1, keepdims=True)
    var = jnp.mean(jnp.square(xf - mu), -1, keepdims=True)
    return ((xf - mu) * lax.rsqrt(var + LN_EPS) * g.astype(jnp.float32) + b.astype(jnp.float32)).astype(x.dtype)


def _rms_norm(x, g):
    xf = x.astype(jnp.float32)
    return (xf * lax.rsqrt(jnp.mean(xf * xf, -1, keepdims=True) + NORM_EPS) * g.astype(jnp.float32)).astype(x.dtype)


def _l2norm(x):
    xf = x.astype(jnp.float32)
    return xf * lax.rsqrt(jnp.sum(xf * xf, -1, keepdims=True) + NORM_EPS)


def _causal_conv(x, w):
    c = x.shape[-1]
    return lax.conv_general_dilated(x, w[:, None, :], window_strides=(1,), padding=[(CONV_W - 1, 0)],
                                    dimension_numbers=('NWC', 'WIO', 'NWC'), feature_group_count=c)


def _gated_delta_rule(q, k, v, beta, log_g):
    B, T, H, dk = q.shape
    dv = v.shape[-1]
    n = T // CHUNK
    f32 = jnp.float32

    def to_chunks(a):
        a = a.reshape((B, n, CHUNK) + a.shape[2:])
        return jnp.moveaxis(a, 3, 1)

    q = to_chunks(q.astype(f32)) * (dk ** -0.5)
    k = to_chunks(k.astype(f32))
    v = to_chunks(v.astype(f32))
    beta = to_chunks(beta)
    gam = jnp.cumsum(to_chunks(log_g), axis=-1)
    idx = jnp.arange(CHUNK)
    causal = idx[:, None] >= idx[None, :]
    strict = idx[:, None] > idx[None, :]
    decay = jnp.exp(jnp.where(causal, gam[..., :, None] - gam[..., None, :], -jnp.inf))

    kk = jnp.einsum('bhnid,bhnjd->bhnij', k, k)
    a_mat = jnp.where(strict, kk * beta[..., :, None] * decay, 0.0) + jnp.eye(CHUNK, dtype=f32)
    rhs = jnp.concatenate([v * beta[..., None], k * (beta * jnp.exp(gam))[..., None]], axis=-1)
    sol = lax.linalg.triangular_solve(a_mat, rhs, left_side=True, lower=True, unit_diagonal=True)
    u, w = sol[..., :dv], sol[..., dv:]

    qk_intra = jnp.where(causal, jnp.einsum('bhnid,bhnjd->bhnij', q, k) * decay, 0.0)
    q_dec = q * jnp.exp(gam)[..., None]
    k_dec = k * jnp.exp(gam[..., -1:] - gam)[..., None]
    g_last = jnp.exp(gam[..., -1])

    xs = tuple(jnp.moveaxis(a, 2, 0) for a in (q_dec, k_dec, u, w, qk_intra, g_last))

    def step(S, inp):
        qd, kd, u_c, w_c, a_c, gl = inp
        v_new = u_c - jnp.einsum('bhck,bhkv->bhcv', w_c, S)
        o = jnp.einsum('bhck,bhkv->bhcv', qd, S) + jnp.einsum('bhij,bhjv->bhiv', a_c, v_new)
        S = S * gl[..., None, None] + jnp.einsum('bhck,bhcv->bhkv', kd, v_new)
        return S, o

    s0 = jnp.zeros((B, H, dk, dv), f32)
    _, o = lax.scan(step, s0, xs)
    o = jnp.moveaxis(o, 0, 2)
    return jnp.moveaxis(o, 1, 3).reshape(B, T, H, dv)


def _forgetting_attention(q, k, v, log_f):
    B, T, H, d = q.shape
    nb = T // Q_BLOCK
    scale = d ** -0.5
    c_all = jnp.transpose(jnp.cumsum(log_f, axis=1), (0, 2, 1))
    qb = jnp.moveaxis(q.reshape(B, nb, Q_BLOCK, H, d), 1, 0)
    cb = jnp.moveaxis(c_all.reshape(B, H, nb, Q_BLOCK), 2, 0)
    k_pos = jnp.arange(T)

    def block(args):
        i, q_i, c_i = args
        s = jnp.einsum('bqhd,bkhd->bhqk', q_i, k).astype(jnp.float32) * scale
        s = s + c_i[..., :, None] - c_all[..., None, :]
        q_pos = i * Q_BLOCK + jnp.arange(Q_BLOCK)
        s = jnp.where(k_pos[None, :] <= q_pos[:, None], s, -jnp.inf)
        attn = jax.nn.softmax(s, axis=-1)
        return jnp.einsum('bhqk,bkhd->bqhd', attn.astype(v.dtype), v)

    out = lax.map(block, (jnp.arange(nb), qb, cb))
    return jnp.moveaxis(out, 0, 1).reshape(B, T, H, d)


def setup_inputs(seed: int = 0) -> dict:
    key = jax.random.key(seed)
    ks = jax.random.split(key, 24)
    f32 = jnp.float32

    def nrm(k, shape, s):
        return jax.random.normal(k, shape, f32) * s

    x = nrm(ks[0], (BATCH, SEQ, D_MODEL), 1.0)
    p = nrm(ks[1], (DEPTH, BATCH, SEQ, D_PLE), 1.0)
    ln_in_g = 1.0 + nrm(ks[2], (D_MODEL,), 0.02)
    ln_in_b = nrm(ks[3], (D_MODEL,), 0.02)
    w_in = nrm(ks[4], (DEPTH, D_MODEL, D_IN), D_MODEL ** -0.5)
    conv_w = nrm(ks[5], (DEPTH, CONV_W, GDN_QKV), CONV_W ** -0.5)
    a_log = jnp.log(jax.random.uniform(ks[6], (DEPTH, GDN_HEADS), f32, 1.0, 16.0))
    dt = jnp.exp(jax.random.uniform(ks[7], (DEPTH, GDN_HEADS), f32, math.log(1e-3), math.log(1e-1)))
    dt_bias = dt + jnp.log(-jnp.expm1(-dt))
    gdn_norm_g = 1.0 + nrm(ks[8], (DEPTH, GDN_DV), 0.02)
    b_f = jnp.linspace(1.0, 5.0, FOX_HEADS, dtype=f32)[None, :] + nrm(ks[9], (DEPTH, FOX_HEADS), 0.1)
    fox_norm_g = 1.0 + nrm(ks[10], (DEPTH, FOX_DH), 0.02)
    w_out = nrm(ks[11], (DEPTH, D_MODEL, D_MODEL), BETA_INIT * D_MODEL ** -0.5)
    ln1_g = 1.0 + nrm(ks[12], (DEPTH, D_MODEL), 0.02)
    ln1_b = nrm(ks[13], (DEPTH, D_MODEL), 0.02)
    w_up = nrm(ks[14], (DEPTH, D_MODEL, D_FF), D_MODEL ** -0.5)
    w_down = nrm(ks[15], (DEPTH, D_FF, D_MODEL), BETA_INIT * D_FF ** -0.5)
    w_ple = nrm(ks[16], (DEPTH, D_PLE, D_MODEL), BETA_INIT * D_PLE ** -0.5)
    w_ple_gate = nrm(ks[17], (DEPTH, D_MODEL, D_MODEL), D_MODEL ** -0.5)
    b_ple_gate = nrm(ks[18], (DEPTH, D_MODEL), 0.02)
    ln2_g = 1.0 + nrm(ks[19], (DEPTH, D_MODEL), 0.02)
    ln2_b = nrm(ks[20], (DEPTH, D_MODEL), 0.02)
    return {'x': x, 'p': p, 'ln_in_g': ln_in_g, 'ln_in_b': ln_in_b, 'w_in': w_in, 'conv_w': conv_w,
            'a_log': a_log, 'dt_bias': dt_bias, 'gdn_norm_g': gdn_norm_g, 'b_f': b_f,
            'fox_norm_g': fox_norm_g, 'w_out': w_out, 'ln1_g': ln1_g, 'ln1_b': ln1_b, 'w_up': w_up,
            'w_down': w_down, 'w_ple': w_ple, 'w_ple_gate': w_ple_gate, 'b_ple_gate': b_ple_gate,
            'ln2_g': ln2_g, 'ln2_b': ln2_b}


def reference(x, p, ln_in_g, ln_in_b, w_in, conv_w, a_log, dt_bias, gdn_norm_g, b_f, fox_norm_g,
              w_out, ln1_g, ln1_b, w_up, w_down, w_ple, w_ple_gate, b_ple_gate, ln2_g, ln2_b):
    B, T, _ = x.shape
    f32 = jnp.float32
    h = _layer_norm(x, ln_in_g, ln_in_b)
    for i in range(DEPTH):
        proj = h @ w_in[i]

        qkv = jax.nn.silu(_causal_conv(proj[..., :GDN_QKV], conv_w[i]))
        gq = _l2norm(qkv[..., :GDN_QK].reshape(B, T, GDN_HEADS, GDN_DK))
        gk = _l2norm(qkv[..., GDN_QK:2 * GDN_QK].reshape(B, T, GDN_HEADS, GDN_DK))
        gv = qkv[..., 2 * GDN_QK:].reshape(B, T, GDN_HEADS, GDN_DV)
        z = proj[..., OFF_Z:OFF_BETA].reshape(B, T, GDN_HEADS, GDN_DV)
        beta = jax.nn.sigmoid(proj[..., OFF_BETA:OFF_A].astype(f32))
        log_g = -jnp.exp(a_log[i].astype(f32)) * jax.nn.softplus(proj[..., OFF_A:OFF_FOX].astype(f32) + dt_bias[i].astype(f32))
        o_gdn = _gated_delta_rule(gq, gk, gv, beta, log_g).astype(x.dtype)
        o_gdn = (_rms_norm(o_gdn, gdn_norm_g[i]) * jax.nn.silu(z)).reshape(B, T, GDN_WIDTH)

        fqkv = proj[..., OFF_FOX:OFF_F].reshape(B, T, 3, FOX_HEADS, FOX_DH)
        log_f = jax.nn.log_sigmoid(proj[..., OFF_F:].astype(f32) + b_f[i].astype(f32))
        o_fox = _forgetting_attention(fqkv[:, :, 0], fqkv[:, :, 1], fqkv[:, :, 2], log_f)
        o_fox = _rms_norm(o_fox, fox_norm_g[i]).reshape(B, T, FOX_WIDTH)

        mix = jnp.concatenate([o_gdn, o_fox], axis=-1) @ w_out[i]
        h = _layer_norm(ALPHA * h + mix, ln1_g[i], ln1_b[i])

        ff = jnp.square(jax.nn.relu(h @ w_up[i])) @ w_down[i]
        ple = (p[i] @ w_ple[i]) * jax.nn.sigmoid(h @ w_ple_gate[i] + b_ple_gate[i])
        h = _layer_norm(ALPHA * h + ff + ple, ln2_g[i], ln2_b[i])
    return h
```

```python
import functools

import jax
import jax.numpy as jnp
from jax import lax
from jax.experimental import pallas as pl
from jax.experimental.pallas import tpu as pltpu

F32 = jnp.float32
BF16 = jnp.bfloat16
HIGHEST = lax.Precision.HIGHEST

D_MODEL = 1024
CHUNK = 64
GDN_HEADS = 4
GDN_DK = 128
GDN_DV = 128
GDN_QK = GDN_HEADS * GDN_DK
GDN_QKV = 3 * GDN_QK
GDN_WIDTH = GDN_HEADS * GDN_DV
FOX_HEADS = 8
FOX_DH = 64
FOX_WIDTH = FOX_HEADS * FOX_DH
CONV_W = 4
D_FF = 4 * D_MODEL
D_PLE = 256
LN_EPS = 1e-5
NORM_EPS = 1e-6
ALPHA = 2.0 ** 0.25

OFF_Z = GDN_QKV
OFF_BETA = OFF_Z + GDN_WIDTH
OFF_FOX = OFF_BETA + 2 * GDN_HEADS
OFF_F = OFF_FOX + 3 * FOX_WIDTH
N_BIG = GDN_QKV + GDN_WIDTH + 3 * FOX_WIDTH
N_SMALL = 128
LANE_BETA = 0
LANE_GAM = GDN_HEADS
LANE_C = 2 * GDN_HEADS

VMEM_LIMIT = 56 * 1024 * 1024

PROJ_TM = 512
PROJ_TN = 512
GATE_TB = 256
GDN_TB = 256
FOX_TQ = 256
TAIL_TM = 512
TAIL_TF = 1024


def _layer_norm(x, g, b):
    mu = jnp.mean(x, -1, keepdims=True)
    xc = x - mu
    var = jnp.mean(xc * xc, -1, keepdims=True)
    return xc * lax.rsqrt(var + LN_EPS) * g + b


def _softplus(x):
    return jnp.maximum(x, 0.0) + jnp.log(1.0 + jnp.exp(-jnp.abs(x)))


def _sigmoid(x):
    return 1.0 / (1.0 + jnp.exp(-x))


def _dot(a, b):
    return jnp.dot(a, b, preferred_element_type=F32)


def _dot_hi(a, b):
    return jnp.dot(a, b, precision=HIGHEST, preferred_element_type=F32)


def _dot_nt(a, b):
    return lax.dot_general(a, b, (((1,), (1,)), ((), ())), preferred_element_type=F32)


def _dot_tn(a, b):
    return lax.dot_general(a, b, (((0,), (0,)), ((), ())), preferred_element_type=F32)


def _resident(shape):
    return pl.BlockSpec(shape, lambda *_: (0,) * len(shape), pipeline_mode=pl.Buffered(1))


def _proj_kernel(x_ref, g_ref, b_ref, wb_ref, ws_ref, ob_ref, os_ref):
    h = _layer_norm(x_ref[...], g_ref[...], b_ref[...]).astype(BF16)
    for j in range(N_BIG // PROJ_TN):
        cols = slice(j * PROJ_TN, (j + 1) * PROJ_TN)
        ob_ref[:, cols] = _dot(h, wb_ref[:, cols]).astype(BF16)
    os_ref[...] = _dot(h, ws_ref[...])


def _proj(x2, ln_g, ln_b, w_big, w_small):
    n = x2.shape[0]
    return pl.pallas_call(
        _proj_kernel,
        grid=(n // PROJ_TM,),
        in_specs=[
            pl.BlockSpec((PROJ_TM, D_MODEL), lambda i: (i, 0)),
            _resident((1, D_MODEL)),
            _resident((1, D_MODEL)),
            _resident((D_MODEL, N_BIG)),
            _resident((D_MODEL, N_SMALL)),
        ],
        out_specs=[
            pl.BlockSpec((PROJ_TM, N_BIG), lambda i: (i, 0)),
            pl.BlockSpec((PROJ_TM, N_SMALL), lambda i: (i, 0)),
        ],
        out_shape=[
            jax.ShapeDtypeStruct((n, N_BIG), BF16),
            jax.ShapeDtypeStruct((n, N_SMALL), F32),
        ],
        compiler_params=pltpu.CompilerParams(
            dimension_semantics=("arbitrary",), vmem_limit_bytes=VMEM_LIMIT),
        name="proj",
    )(x2, ln_g, ln_b, w_big, w_small)


def _gates_kernel(s_ref, pv_ref, o_ref, *, seq):
    lane = lax.broadcasted_iota(jnp.int32, (1, N_SMALL), 1)
    bias = pv_ref[0:1, :]
    neg_a = -jnp.exp(pv_ref[1:2, :])
    r = lax.broadcasted_iota(jnp.int32, (GATE_TB, GATE_TB), 0)
    c = lax.broadcasted_iota(jnp.int32, (GATE_TB, GATE_TB), 1)
    tri = r >= c
    l_full = jnp.where(tri, 1.0, 0.0).astype(F32)
    same_chunk = jnp.right_shift(r, 6) == jnp.right_shift(c, 6)
    l_chunk = jnp.where(same_chunk, l_full, 0.0)
    carry = jnp.zeros((1, N_SMALL), F32)
    for t in range(seq // GATE_TB):
        rows = slice(t * GATE_TB, (t + 1) * GATE_TB)
        x = s_ref[rows, :] + bias
        beta = _sigmoid(x)
        log_g = neg_a * _softplus(x)
        log_f = -_softplus(-x)
        gam = _dot_hi(l_chunk, log_g)
        cum = _dot_hi(l_full, log_f) + carry
        carry = cum[GATE_TB - 1:GATE_TB, :]
        o_ref[rows, :] = jnp.where(lane < LANE_GAM, beta, jnp.where(lane < LANE_C, gam, cum))


def _gates(small, pvec, batch, seq):
    return pl.pallas_call(
        functools.partial(_gates_kernel, seq=seq),
        grid=(batch,),
        in_specs=[
            pl.BlockSpec((seq, N_SMALL), lambda b: (b, 0)),
            _resident((8, N_SMALL)),
        ],
        out_specs=pl.BlockSpec((seq, N_SMALL), lambda b: (b, 0)),
        out_shape=jax.ShapeDtypeStruct((batch * seq, N_SMALL), F32),
        compiler_params=pltpu.CompilerParams(
            dimension_semantics=("arbitrary",), vmem_limit_bytes=VMEM_LIMIT),
        name="gates",
    )(small, pvec)


def _unit_lower_inverse(a):
    n = a.shape[0]
    r = lax.broadcasted_iota(jnp.int32, (n, n), 0)
    c = lax.broadcasted_iota(jnp.int32, (n, n), 1)
    p = jnp.where(r == c, 1.0, 0.0).astype(F32) - a
    x = a
    for _ in range(5):
        x = _dot_hi(x, x)
        p = p + _dot_hi(p, x)
    return p


def _gdn_kernel(x_ref, z_ref, gc_ref, gr_ref, cw_ref, ng_ref, o_ref, xs_ref, s_ref):
    tb = x_ref.shape[0]
    halo = 8

    @pl.when(pl.program_id(1) == 0)
    def _():
        xs_ref[0:halo, :] = jnp.zeros((halo, GDN_QKV), F32)
        s_ref[...] = jnp.zeros_like(s_ref)

    xs_ref[halo:halo + tb, :] = x_ref[...].astype(F32)

    def conv_silu(c0):
        cols = slice(c0, c0 + GDN_DK)
        acc = xs_ref[halo:halo + tb, cols] * cw_ref[CONV_W - 1:CONV_W, cols]
        for j in range(1, CONV_W):
            acc = acc + xs_ref[halo - j:halo - j + tb, cols] * cw_ref[CONV_W - 1 - j:CONV_W - j, cols]
        return acc * _sigmoid(acc)

    def l2norm(v):
        return v * lax.rsqrt(jnp.sum(v * v, -1, keepdims=True) + NORM_EPS)

    ri = lax.broadcasted_iota(jnp.int32, (CHUNK, CHUNK), 0)
    ci = lax.broadcasted_iota(jnp.int32, (CHUNK, CHUNK), 1)
    causal = ri >= ci
    strict = ri > ci
    ng = ng_ref[...]

    for h in range(GDN_HEADS):
        q = l2norm(conv_silu(h * GDN_DK)) * (GDN_DK ** -0.5)
        k = l2norm(conv_silu(GDN_QK + h * GDN_DK))
        v = conv_silu(2 * GDN_QK + h * GDN_DV)
        beta = gc_ref[:, LANE_BETA + h:LANE_BETA + h + 1]
        gam = gc_ref[:, LANE_GAM + h:LANE_GAM + h + 1]
        egam = jnp.exp(gam)
        state = s_ref[h]
        for c in range(tb // CHUNK):
            rows = slice(c * CHUNK, (c + 1) * CHUNK)
            qc, kc, vc, bc, gcol, eg = q[rows], k[rows], v[rows], beta[rows], gam[rows], egam[rows]
            grow = gr_ref[c, LANE_GAM + h:LANE_GAM + h + 1, :]
            glast = gcol[CHUNK - 1:CHUNK, :]
            decay = jnp.exp(jnp.where(causal, gcol - grow, -jnp.inf))
            kcb = kc.astype(BF16)
            kk = _dot_nt(kcb, kcb)
            a = jnp.where(strict, kk * bc * decay, 0.0)
            tinv = _unit_lower_inverse(a)
            rhs = jnp.concatenate([vc * bc, kc * (bc * eg)], axis=1)
            sol = _dot_hi(tinv, rhs)
            u, w = sol[:, :GDN_DV], sol[:, GDN_DV:]
            qk = _dot_nt(qc.astype(BF16), kcb) * decay
            qd = qc * eg
            kd = kc * jnp.exp(glast - gcol)
            sb = state.astype(BF16)
            v_new = u - _dot(w.astype(BF16), sb)
            vb = v_new.astype(BF16)
            o = _dot(qd.astype(BF16), sb) + _dot(qk.astype(BF16), vb)
            state = state * jnp.exp(glast) + _dot_tn(kd.astype(BF16), vb)
            on = o * lax.rsqrt(jnp.mean(o * o, -1, keepdims=True) + NORM_EPS) * ng
            zc = z_ref[rows, h * GDN_DV:(h + 1) * GDN_DV].astype(F32)
            o_ref[rows, h * GDN_DV:(h + 1) * GDN_DV] = (on * (zc * _sigmoid(zc))).astype(BF16)
        s_ref[h] = state

    xs_ref[0:halo, :] = xs_ref[tb:tb + halo, :]


def _gdn(proj_big, gcol, grow, conv_w, norm_g, batch, seq):
    nt = seq // GDN_TB
    nch = GDN_TB // CHUNK
    return pl.pallas_call(
        _gdn_kernel,
        grid=(batch, nt),
        in_specs=[
            pl.BlockSpec((GDN_TB, GDN_QKV), lambda b, t: (b * nt + t, 0)),
            pl.BlockSpec((GDN_TB, GDN_WIDTH), lambda b, t: (b * nt + t, OFF_Z // GDN_WIDTH)),
            pl.BlockSpec((GDN_TB, N_SMALL), lambda b, t: (b * nt + t, 0)),
            pl.BlockSpec((nch, 16, CHUNK), lambda b, t: (b * nt + t, 0, 0)),
            _resident((CONV_W, GDN_QKV)),
            _resident((1, GDN_DV)),
        ],
        out_specs=pl.BlockSpec((GDN_TB, GDN_WIDTH), lambda b, t: (b * nt + t, 0)),
        out_shape=jax.ShapeDtypeStruct((batch * seq, GDN_WIDTH), BF16),
        scratch_shapes=[
            pltpu.VMEM((8 + GDN_TB, GDN_QKV), F32),
            pltpu.VMEM((GDN_HEADS, GDN_DK, GDN_DV), F32),
        ],
        compiler_params=pltpu.CompilerParams(
            dimension_semantics=("arbitrary", "arbitrary"), vmem_limit_bytes=VMEM_LIMIT),
        name="gdn",
    )(proj_big, proj_big, gcol, grow, conv_w, norm_g)


def _fox_kernel(q_ref, k_ref, v_ref, cc_ref, cr_ref, ng_ref, o_ref):
    tq = q_ref.shape[0]
    i = pl.program_id(1)
    lane = lax.broadcasted_iota(jnp.int32, (1, 2 * FOX_DH), 1)
    ri = lax.broadcasted_iota(jnp.int32, (tq, tq), 0)
    ci = lax.broadcasted_iota(jnp.int32, (tq, tq), 1)
    causal = ri >= ci
    ng = ng_ref[...]

    for pair in range(FOX_HEADS // 2):
        cols = slice(pair * 2 * FOX_DH, (pair + 1) * 2 * FOX_DH)
        q_pair = q_ref[:, cols]
        outs = []
        for half in range(2):
            head = 2 * pair + half
            mine = (lane >= half * FOX_DH) & (lane < (half + 1) * FOX_DH)
            qm = jnp.where(mine, q_pair * (FOX_DH ** -0.5), 0.0).astype(BF16)
            ct = cc_ref[:, LANE_C + head:LANE_C + head + 1]

            def scores(j):
                start = pl.multiple_of(j * tq, tq)
                kj = k_ref[pl.ds(start, tq), cols]
                cs = cr_ref[j, head:head + 1, :]
                return _dot_nt(qm, kj) + (ct - cs), start

            def update(carry, s, start):
                m, l, acc = carry
                m_new = jnp.maximum(m, jnp.max(s, -1, keepdims=True))
                scale = jnp.exp(m - m_new)
                p = jnp.exp(s - m_new)
                l = scale * l + jnp.sum(p, -1, keepdims=True)
                vj = v_ref[pl.ds(start, tq), cols]
                acc = scale * acc + _dot(p.astype(BF16), vj)
                return m_new, l, acc

            def body(j, carry):
                s, start = scores(j)
                return update(carry, s, start)

            init = (jnp.full((tq, 1), -jnp.inf, F32), jnp.zeros((tq, 1), F32),
                    jnp.zeros((tq, 2 * FOX_DH), F32))
            carry = lax.fori_loop(0, i, body, init)
            s, start = scores(i)
            _, l, acc = update(carry, jnp.where(causal, s, -jnp.inf), start)
            o = acc / l
            ms = jnp.sum(jnp.where(mine, o * o, 0.0), -1, keepdims=True) * (1.0 / FOX_DH)
            outs.append((mine, o * lax.rsqrt(ms + NORM_EPS) * ng))
        o_ref[:, cols] = jnp.where(outs[0][0], outs[0][1], outs[1][1]).astype(BF16)


def _fox(proj_big, gcol, crow, norm_g2, batch, seq):
    nq = seq // FOX_TQ
    first = (GDN_QKV + GDN_WIDTH) // FOX_WIDTH
    return pl.pallas_call(
        _fox_kernel,
        grid=(batch, nq),
        in_specs=[
            pl.BlockSpec((FOX_TQ, FOX_WIDTH), lambda b, i: (b * nq + i, first)),
            pl.BlockSpec((seq, FOX_WIDTH), lambda b, i: (b, first + 1)),
            pl.BlockSpec((seq, FOX_WIDTH), lambda b, i: (b, first + 2)),
            pl.BlockSpec((FOX_TQ, N_SMALL), lambda b, i: (b * nq + i, 0)),
            pl.BlockSpec((nq, FOX_HEADS, FOX_TQ), lambda b, i: (b, 0, 0)),
            _resident((1, 2 * FOX_DH)),
        ],
        out_specs=pl.BlockSpec((FOX_TQ, FOX_WIDTH), lambda b, i: (b * nq + i, 0)),
        out_shape=jax.ShapeDtypeStruct((batch * seq, FOX_WIDTH), BF16),
        compiler_params=pltpu.CompilerParams(
            dimension_semantics=("arbitrary", "arbitrary"), vmem_limit_bytes=VMEM_LIMIT),
        name="fox",
    )(proj_big, proj_big, proj_big, gcol, crow, norm_g2)


def _tail_kernel(x_ref, p_ref, og_ref, of_ref, lin_g, lin_b, wo_ref, l1g, l1b, wu_ref, wd_ref,
                 wp_ref, wg_ref, bg_ref, l2g, l2b, o_ref):
    h = _layer_norm(x_ref[...], lin_g[...], lin_b[...])
    mix = _dot(og_ref[...], wo_ref[0:GDN_WIDTH, :]) + _dot(of_ref[...], wo_ref[GDN_WIDTH:, :])
    h1 = _layer_norm(ALPHA * h + mix, l1g[...], l1b[...])
    h1b = h1.astype(BF16)
    gate = _sigmoid(_dot(h1b, wg_ref[...]) + bg_ref[...])
    acc = ALPHA * h1 + _dot(p_ref[...].astype(BF16), wp_ref[...]) * gate
    for j in range(D_FF // TAIL_TF):
        cols = slice(j * TAIL_TF, (j + 1) * TAIL_TF)
        a = jnp.maximum(_dot(h1b, wu_ref[:, cols]), 0.0)
        acc = acc + _dot((a * a).astype(BF16), wd_ref[cols, :])
    o_ref[...] = _layer_norm(acc, l2g[...], l2b[...])


def _tail(x2, p2, o_gdn, o_fox, lin_g, lin_b, w_out, l1g, l1b, w_up, w_down, w_ple, w_gate,
          b_gate, l2g, l2b):
    n = x2.shape[0]
    row = lambda width: pl.BlockSpec((TAIL_TM, width), lambda i: (i, 0))
    vec = _resident((1, D_MODEL))
    return pl.pallas_call(
        _tail_kernel,
        grid=(n // TAIL_TM,),
        in_specs=[
            row(D_MODEL), row(D_PLE), row(GDN_WIDTH), row(FOX_WIDTH),
            vec, vec, _resident((D_MODEL, D_MODEL)), vec, vec,
            _resident((D_MODEL, D_FF)), _resident((D_FF, D_MODEL)),
            _resident((D_PLE, D_MODEL)), _resident((D_MODEL, D_MODEL)), vec, vec, vec,
        ],
        out_specs=row(D_MODEL),
        out_shape=jax.ShapeDtypeStruct((n, D_MODEL), F32),
        compiler_params=pltpu.CompilerParams(
            dimension_semantics=("arbitrary",), vmem_limit_bytes=VMEM_LIMIT),
        name="tail",
    )(x2, p2, o_gdn, o_fox, lin_g, lin_b, w_out, l1g, l1b, w_up, w_down, w_ple, w_gate,
      b_gate, l2g, l2b)


def kernel(x, p, ln_in_g, ln_in_b, w_in, conv_w, a_log, dt_bias, gdn_norm_g, b_f, fox_norm_g,
           w_out, ln1_g, ln1_b, w_up, w_down, w_ple, w_ple_gate, b_ple_gate, ln2_g, ln2_b):
    batch, seq, _ = x.shape
    assert x.shape[2] == D_MODEL and w_in.shape[0] == 1
    assert seq % FOX_TQ == 0 and seq % GDN_TB == 0 and (batch * seq) % PROJ_TM == 0
    n = batch * seq
    x2 = x.reshape(n, D_MODEL)
    p2 = p[0].reshape(n, D_PLE)
    row = lambda a: a.reshape(1, -1).astype(F32)

    w0 = w_in[0]
    w_big = jnp.concatenate([w0[:, :OFF_BETA], w0[:, OFF_FOX:OFF_F]], axis=1).astype(BF16)
    w_small = jnp.concatenate(
        [w0[:, OFF_BETA:OFF_FOX], w0[:, OFF_F:], jnp.zeros((D_MODEL, N_SMALL - 16), F32)],
        axis=1).astype(BF16)
    zeros4 = jnp.zeros((GDN_HEADS,), F32)
    pad = jnp.zeros((N_SMALL - 16,), F32)
    pvec = jnp.zeros((8, N_SMALL), F32)
    pvec = pvec.at[0].set(jnp.concatenate([zeros4, dt_bias[0], b_f[0], pad]))
    pvec = pvec.at[1].set(jnp.concatenate([zeros4, a_log[0], jnp.zeros((FOX_HEADS,), F32), pad]))

    proj_big, small = _proj(x2, row(ln_in_g), row(ln_in_b), w_big, w_small)
    gcol = _gates(small, pvec, batch, seq)

    g16 = gcol[:, :16]
    grow = jnp.transpose(g16.reshape(n // CHUNK, CHUNK, 16), (0, 2, 1))
    crow = jnp.transpose(
        g16[:, LANE_C:].reshape(n // FOX_TQ, FOX_TQ, FOX_HEADS), (0, 2, 1))

    o_gdn = _gdn(proj_big, gcol, grow, conv_w[0], row(gdn_norm_g[0]), batch, seq)
    o_fox = _fox(proj_big, gcol, crow, row(jnp.tile(fox_norm_g[0], 2)), batch, seq)

    out = _tail(x2, p2, o_gdn, o_fox, row(ln_in_g), row(ln_in_b), w_out[0].astype(BF16),
                row(ln1_g[0]), row(ln1_b[0]), w_up[0].astype(BF16), w_down[0].astype(BF16),
                w_ple[0].astype(BF16), w_ple_gate[0].astype(BF16), row(b_ple_gate[0]),
                row(ln2_g[0]), row(ln2_b[0]))
    return out.reshape(batch, seq, D_MODEL)
```

```python
import functools

import jax
import jax.numpy as jnp
from jax import lax
from jax.experimental import pallas as pl
from jax.experimental.pallas import tpu as pltpu

F32 = jnp.float32
BF16 = jnp.bfloat16
HIGHEST = lax.Precision.HIGHEST

D_MODEL = 1024
CHUNK = 64
GDN_HEADS = 4
GDN_DK = 128
GDN_DV = 128
GDN_QK = GDN_HEADS * GDN_DK
GDN_QKV = 3 * GDN_QK
GDN_WIDTH = GDN_HEADS * GDN_DV
FOX_HEADS = 8
FOX_DH = 64
FOX_WIDTH = FOX_HEADS * FOX_DH
CONV_W = 4
D_FF = 4 * D_MODEL
D_PLE = 256
LN_EPS = 1e-5
NORM_EPS = 1e-6
ALPHA = 2.0 ** 0.25

OFF_Z = GDN_QKV
OFF_BETA = OFF_Z + GDN_WIDTH
OFF_FOX = OFF_BETA + 2 * GDN_HEADS
OFF_F = OFF_FOX + 3 * FOX_WIDTH
N_BIG = GDN_QKV + GDN_WIDTH + 3 * FOX_WIDTH
N_SMALL = 128
LANE_BETA = 0
LANE_GAM = GDN_HEADS
LANE_C = 2 * GDN_HEADS
LANE_GTOT = LANE_C + FOX_HEADS

VMEM_LIMIT = 56 * 1024 * 1024

PROJ_TM = 512
PROJ_TN = 512
GATE_TB = 256
GDN_TB = 256
FOX_TQ = 256
TAIL_TM = 512
TAIL_TF = 1024


def _layer_norm(x, g, b):
    mu = jnp.mean(x, -1, keepdims=True)
    xc = x - mu
    var = jnp.mean(xc * xc, -1, keepdims=True)
    return xc * lax.rsqrt(var + LN_EPS) * g + b


def _softplus(x):
    return jnp.maximum(x, 0.0) + jnp.log(1.0 + jnp.exp(-jnp.abs(x)))


def _sigmoid(x):
    return 1.0 / (1.0 + jnp.exp(-x))


def _dot(a, b):
    return jnp.dot(a, b, preferred_element_type=F32)


def _dot_hi(a, b):
    return jnp.dot(a, b, precision=HIGHEST, preferred_element_type=F32)


def _dot_nt(a, b):
    return lax.dot_general(a, b, (((1,), (1,)), ((), ())), preferred_element_type=F32)


def _dot_tn(a, b):
    return lax.dot_general(a, b, (((0,), (0,)), ((), ())), preferred_element_type=F32)


def _resident(shape):
    return pl.BlockSpec(shape, lambda *_: (0,) * len(shape), pipeline_mode=pl.Buffered(1))


def _proj_kernel(x_ref, g_ref, b_ref, wb_ref, ws_ref, ob_ref, os_ref):
    h = _layer_norm(x_ref[...], g_ref[...], b_ref[...]).astype(BF16)
    for j in range(N_BIG // PROJ_TN):
        cols = slice(j * PROJ_TN, (j + 1) * PROJ_TN)
        ob_ref[:, cols] = _dot(h, wb_ref[:, cols]).astype(BF16)
    os_ref[...] = _dot(h, ws_ref[...])


def _proj(x2, ln_g, ln_b, w_big, w_small):
    n = x2.shape[0]
    return pl.pallas_call(
        _proj_kernel,
        grid=(n // PROJ_TM,),
        in_specs=[
            pl.BlockSpec((PROJ_TM, D_MODEL), lambda i: (i, 0)),
            _resident((1, D_MODEL)),
            _resident((1, D_MODEL)),
            _resident((D_MODEL, N_BIG)),
            _resident((D_MODEL, N_SMALL)),
        ],
        out_specs=[
            pl.BlockSpec((PROJ_TM, N_BIG), lambda i: (i, 0)),
            pl.BlockSpec((PROJ_TM, N_SMALL), lambda i: (i, 0)),
        ],
        out_shape=[
            jax.ShapeDtypeStruct((n, N_BIG), BF16),
            jax.ShapeDtypeStruct((n, N_SMALL), F32),
        ],
        compiler_params=pltpu.CompilerParams(
            dimension_semantics=("arbitrary",), vmem_limit_bytes=VMEM_LIMIT),
        name="proj",
    )(x2, ln_g, ln_b, w_big, w_small)


def _gates_kernel(s_ref, pv_ref, o_ref, *, seq):
    lane = lax.broadcasted_iota(jnp.int32, (1, N_SMALL), 1)
    bias = pv_ref[0:1, :]
    neg_a = -jnp.exp(pv_ref[1:2, :])
    r = lax.broadcasted_iota(jnp.int32, (GATE_TB, GATE_TB), 0)
    c = lax.broadcasted_iota(jnp.int32, (GATE_TB, GATE_TB), 1)
    tri = r >= c
    l_full = jnp.where(tri, 1.0, 0.0).astype(F32)
    same_chunk = jnp.right_shift(r, 6) == jnp.right_shift(c, 6)
    l_chunk = jnp.where(same_chunk, l_full, 0.0)
    l_total = jnp.where(same_chunk, 1.0, 0.0).astype(F32)
    carry = jnp.zeros((1, N_SMALL), F32)
    for t in range(seq // GATE_TB):
        rows = slice(t * GATE_TB, (t + 1) * GATE_TB)
        x = s_ref[rows, :] + bias
        beta = _sigmoid(x)
        log_g = neg_a * _softplus(x)
        log_f = -_softplus(-x)
        gam = _dot_hi(l_chunk, log_g)
        gtot = _dot_hi(l_total, log_g)
        cum = _dot_hi(l_full, log_f) + carry
        carry = cum[GATE_TB - 1:GATE_TB, :]
        o_ref[rows, :] = jnp.where(
            lane < LANE_GAM, beta,
            jnp.where(lane < LANE_C, gam, jnp.where(lane < LANE_GTOT, cum, gtot)))


def _gates(small, pvec, batch, seq):
    return pl.pallas_call(
        functools.partial(_gates_kernel, seq=seq),
        grid=(batch,),
        in_specs=[
            pl.BlockSpec((seq, N_SMALL), lambda b: (b, 0)),
            _resident((8, N_SMALL)),
        ],
        out_specs=pl.BlockSpec((seq, N_SMALL), lambda b: (b, 0)),
        out_shape=jax.ShapeDtypeStruct((batch * seq, N_SMALL), F32),
        compiler_params=pltpu.CompilerParams(
            dimension_semantics=("arbitrary",), vmem_limit_bytes=VMEM_LIMIT),
        name="gates",
    )(small, pvec)


def _gdn_kernel(x_ref, z_ref, gc_ref, gr_ref, cw_ref, ng_ref, o_ref, xs_ref, s_ref):
    tb = x_ref.shape[0]
    nch = tb // CHUNK
    halo = 8
    heads = range(GDN_HEADS)

    @pl.when(pl.program_id(1) == 0)
    def _():
        xs_ref[0:halo, :] = jnp.zeros((halo, GDN_QKV), F32)
        s_ref[...] = jnp.zeros_like(s_ref)

    xs_ref[halo:halo + tb, :] = x_ref[...].astype(F32)

    def conv_silu(c0):
        cols = slice(c0, c0 + GDN_DK)
        acc = xs_ref[halo:halo + tb, cols] * cw_ref[CONV_W - 1:CONV_W, cols]
        for j in range(1, CONV_W):
            acc = acc + xs_ref[halo - j:halo - j + tb, cols] * cw_ref[CONV_W - 1 - j:CONV_W - j, cols]
        return acc * _sigmoid(acc)

    def l2norm(v):
        return v * lax.rsqrt(jnp.sum(v * v, -1, keepdims=True) + NORM_EPS)

    ri = lax.broadcasted_iota(jnp.int32, (tb, tb), 0)
    ci = lax.broadcasted_iota(jnp.int32, (tb, tb), 1)
    same = jnp.right_shift(ri, 6) == jnp.right_shift(ci, 6)
    keep = jnp.logical_and(same, ri >= ci)
    diag = ri == ci
    pr = lax.broadcasted_iota(jnp.int32, (CHUNK, tb), 0)
    pc = lax.broadcasted_iota(jnp.int32, (CHUNK, tb), 1)
    eye_packed = jnp.where(jnp.bitwise_and(pc, CHUNK - 1) == pr, 1.0, 0.0).astype(F32)
    lane_chunk = jnp.right_shift(lax.broadcasted_iota(jnp.int32, (1, tb), 1), 6)
    row_chunk = jnp.right_shift(lax.broadcasted_iota(jnp.int32, (tb, 1), 0), 6)

    def to_bd(packed):
        return jnp.where(same, jnp.concatenate([packed] * nch, axis=0), jnp.zeros((), packed.dtype))

    def to_packed(bd):
        out = bd[0:CHUNK]
        for c in range(1, nch):
            out = out + bd[c * CHUNK:(c + 1) * CHUNK]
        return out

    ng = ng_ref[...]
    q = [l2norm(conv_silu(h * GDN_DK)) * (GDN_DK ** -0.5) for h in heads]
    k = [l2norm(conv_silu(GDN_QK + h * GDN_DK)) for h in heads]
    v = [conv_silu(2 * GDN_QK + h * GDN_DV) for h in heads]
    beta = [gc_ref[:, LANE_BETA + h:LANE_BETA + h + 1] for h in heads]
    gam = [gc_ref[:, LANE_GAM + h:LANE_GAM + h + 1] for h in heads]
    gtot = [gc_ref[:, LANE_GTOT + h:LANE_GTOT + h + 1] for h in heads]
    grow = [gr_ref[0, h:h + 1, :] for h in heads]
    egam = [jnp.exp(g) for g in gam]
    kb = [a.astype(BF16) for a in k]
    decay = [jnp.exp(jnp.where(keep, gam[h] - grow[h], -jnp.inf)) for h in heads]
    gram = [_dot_nt(kb[h], kb[h]) for h in heads]
    qk = [(_dot_nt(q[h].astype(BF16), kb[h]) * decay[h]).astype(BF16) for h in heads]

    x_bd = [jnp.where(diag, 0.0, -(gram[h] * beta[h]) * decay[h]) for h in heads]
    x_p = [to_packed(a) for a in x_bd]
    p_p = [eye_packed + a for a in x_p]
    x_p = [_dot(x_p[h].astype(BF16), x_bd[h].astype(BF16)) for h in heads]
    for _ in range(4):
        w_bd = [to_bd(a.astype(BF16)) for a in x_p]
        r = [_dot(jnp.concatenate([p_p[h], x_p[h]], axis=0).astype(BF16), w_bd[h]) for h in heads]
        p_p = [p_p[h] + r[h][:CHUNK] for h in heads]
        x_p = [r[h][CHUNK:] for h in heads]
    p_p = [p_p[h] + _dot(p_p[h].astype(BF16), to_bd(x_p[h].astype(BF16))) for h in heads]

    rhs = [jnp.concatenate([v[h] * beta[h], k[h] * (beta[h] * egam[h])], axis=1).astype(BF16)
           for h in heads]
    sol = [_dot(to_bd(p_p[h].astype(BF16)), rhs[h]) for h in heads]
    u = [a[:, :GDN_DV] for a in sol]
    w = [a[:, GDN_DV:].astype(BF16) for a in sol]
    qd = [(q[h] * egam[h]).astype(BF16) for h in heads]
    kd_t = [jnp.transpose(k[h] * jnp.exp(gtot[h] - gam[h])).astype(BF16) for h in heads]
    z = [z_ref[:, h * GDN_DV:(h + 1) * GDN_DV].astype(F32) for h in heads]
    zgate = [a * _sigmoid(a) for a in z]

    state = [s_ref[h] for h in heads]
    for c in range(nch):
        rows = slice(c * CHUNK, (c + 1) * CHUNK)
        sb = [a.astype(BF16) for a in state]
        r1 = [_dot(jnp.concatenate([w[h][rows], qd[h][rows]], axis=0), sb[h]) for h in heads]
        v_new = [u[h][rows] - r1[h][:CHUNK] for h in heads]
        v_full = [jnp.where(row_chunk == c, jnp.concatenate([a.astype(BF16)] * nch, axis=0),
                            jnp.zeros((), BF16)) for a in v_new]
        kd_c = [jnp.where(lane_chunk == c, a, jnp.zeros((), BF16)) for a in kd_t]
        r2 = [_dot(jnp.concatenate([qk[h][rows], kd_c[h]], axis=0), v_full[h]) for h in heads]
        for h in heads:
            o = r1[h][CHUNK:] + r2[h][:CHUNK]
            on = o * lax.rsqrt(jnp.mean(o * o, -1, keepdims=True) + NORM_EPS) * ng
            o_ref[rows, h * GDN_DV:(h + 1) * GDN_DV] = (on * zgate[h][rows]).astype(BF16)
        state = [state[h] * jnp.exp(gtot[h][c * CHUNK:c * CHUNK + 1]) + r2[h][CHUNK:] for h in heads]
    for h in heads:
        s_ref[h] = state[h]

    xs_ref[0:halo, :] = xs_ref[tb:tb + halo, :]


def _gdn(proj_big, gcol, grow, conv_w, norm_g, batch, seq):
    nt = seq // GDN_TB
    return pl.pallas_call(
        _gdn_kernel,
        grid=(batch, nt),
        in_specs=[
            pl.BlockSpec((GDN_TB, GDN_QKV), lambda b, t: (b * nt + t, 0)),
            pl.BlockSpec((GDN_TB, GDN_WIDTH), lambda b, t: (b * nt + t, OFF_Z // GDN_WIDTH)),
            pl.BlockSpec((GDN_TB, N_SMALL), lambda b, t: (b * nt + t, 0)),
            pl.BlockSpec((1, GDN_HEADS, GDN_TB), lambda b, t: (b * nt + t, 0, 0)),
            _resident((CONV_W, GDN_QKV)),
            _resident((1, GDN_DV)),
        ],
        out_specs=pl.BlockSpec((GDN_TB, GDN_WIDTH), lambda b, t: (b * nt + t, 0)),
        out_shape=jax.ShapeDtypeStruct((batch * seq, GDN_WIDTH), BF16),
        scratch_shapes=[
            pltpu.VMEM((8 + GDN_TB, GDN_QKV), F32),
            pltpu.VMEM((GDN_HEADS, GDN_DK, GDN_DV), F32),
        ],
        compiler_params=pltpu.CompilerParams(
            dimension_semantics=("arbitrary", "arbitrary"), vmem_limit_bytes=VMEM_LIMIT),
        name="gdn",
    )(proj_big, proj_big, gcol, grow, conv_w, norm_g)


def _fox_kernel(q_ref, k_ref, v_ref, cc_ref, cr_ref, ng_ref, o_ref):
    tq = q_ref.shape[0]
    i = pl.program_id(1)
    lane = lax.broadcasted_iota(jnp.int32, (1, 2 * FOX_DH), 1)
    ri = lax.broadcasted_iota(jnp.int32, (tq, tq), 0)
    ci = lax.broadcasted_iota(jnp.int32, (tq, tq), 1)
    causal = ri >= ci
    ng = ng_ref[...]

    for pair in range(FOX_HEADS // 2):
        cols = slice(pair * 2 * FOX_DH, (pair + 1) * 2 * FOX_DH)
        q_pair = q_ref[:, cols]
        outs = []
        for half in range(2):
            head = 2 * pair + half
            mine = (lane >= half * FOX_DH) & (lane < (half + 1) * FOX_DH)
            qm = jnp.where(mine, q_pair * (FOX_DH ** -0.5), 0.0).astype(BF16)
            ct = cc_ref[:, LANE_C + head:LANE_C + head + 1]

            def scores(j):
                start = pl.multiple_of(j * tq, tq)
                kj = k_ref[pl.ds(start, tq), cols]
                cs = cr_ref[j, head:head + 1, :]
                return _dot_nt(qm, kj) + (ct - cs), start

            def update(carry, s, start):
                m, l, acc = carry
                m_new = jnp.maximum(m, jnp.max(s, -1, keepdims=True))
                scale = jnp.exp(m - m_new)
                p = jnp.exp(s - m_new)
                l = scale * l + jnp.sum(p, -1, keepdims=True)
                vj = v_ref[pl.ds(start, tq), cols]
                acc = scale * acc + _dot(p.astype(BF16), vj)
                return m_new, l, acc

            def body(j, carry):
                s, start = scores(j)
                return update(carry, s, start)

            init = (jnp.full((tq, 1), -jnp.inf, F32), jnp.zeros((tq, 1), F32),
                    jnp.zeros((tq, 2 * FOX_DH), F32))
            carry = lax.fori_loop(0, i, body, init)
            s, start = scores(i)
            _, l, acc = update(carry, jnp.where(causal, s, -jnp.inf), start)
            o = acc / l
            ms = jnp.sum(jnp.where(mine, o * o, 0.0), -1, keepdims=True) * (1.0 / FOX_DH)
            outs.append((mine, o * lax.rsqrt(ms + NORM_EPS) * ng))
        o_ref[:, cols] = jnp.where(outs[0][0], outs[0][1], outs[1][1]).astype(BF16)


def _fox(proj_big, gcol, crow, norm_g2, batch, seq):
    nq = seq // FOX_TQ
    first = (GDN_QKV + GDN_WIDTH) // FOX_WIDTH
    return pl.pallas_call(
        _fox_kernel,
        grid=(batch, nq),
        in_specs=[
            pl.BlockSpec((FOX_TQ, FOX_WIDTH), lambda b, i: (b * nq + i, first)),
            pl.BlockSpec((seq, FOX_WIDTH), lambda b, i: (b, first + 1)),
            pl.BlockSpec((seq, FOX_WIDTH), lambda b, i: (b, first + 2)),
            pl.BlockSpec((FOX_TQ, N_SMALL), lambda b, i: (b * nq + i, 0)),
            pl.BlockSpec((nq, FOX_HEADS, FOX_TQ), lambda b, i: (b, 0, 0)),
            _resident((1, 2 * FOX_DH)),
        ],
        out_specs=pl.BlockSpec((FOX_TQ, FOX_WIDTH), lambda b, i: (b * nq + i, 0)),
        out_shape=jax.ShapeDtypeStruct((batch * seq, FOX_WIDTH), BF16),
        compiler_params=pltpu.CompilerParams(
            dimension_semantics=("arbitrary", "arbitrary"), vmem_limit_bytes=VMEM_LIMIT),
        name="fox",
    )(proj_big, proj_big, proj_big, gcol, crow, norm_g2)


def _tail_kernel(x_ref, p_ref, og_ref, of_ref, lin_g, lin_b, wo_ref, l1g, l1b, wu_ref, wd_ref,
                 wp_ref, wg_ref, bg_ref, l2g, l2b, o_ref):
    h = _layer_norm(x_ref[...], lin_g[...], lin_b[...])
    mix = _dot(og_ref[...], wo_ref[0:GDN_WIDTH, :]) + _dot(of_ref[...], wo_ref[GDN_WIDTH:, :])
    h1 = _layer_norm(ALPHA * h + mix, l1g[...], l1b[...])
    h1b = h1.astype(BF16)
    gate = _sigmoid(_dot(h1b, wg_ref[...]) + bg_ref[...])
    acc = ALPHA * h1 + _dot(p_ref[...].astype(BF16), wp_ref[...]) * gate
    for j in range(D_FF // TAIL_TF):
        cols = slice(j * TAIL_TF, (j + 1) * TAIL_TF)
        a = jnp.maximum(_dot(h1b, wu_ref[:, cols]), 0.0)
        acc = acc + _dot((a * a).astype(BF16), wd_ref[cols, :])
    o_ref[...] = _layer_norm(acc, l2g[...], l2b[...])


def _tail(x2, p2, o_gdn, o_fox, lin_g, lin_b, w_out, l1g, l1b, w_up, w_down, w_ple, w_gate,
          b_gate, l2g, l2b):
    n = x2.shape[0]
    row = lambda width: pl.BlockSpec((TAIL_TM, width), lambda i: (i, 0))
    vec = _resident((1, D_MODEL))
    return pl.pallas_call(
        _tail_kernel,
        grid=(n // TAIL_TM,),
        in_specs=[
            row(D_MODEL), row(D_PLE), row(GDN_WIDTH), row(FOX_WIDTH),
            vec, vec, _resident((D_MODEL, D_MODEL)), vec, vec,
            _resident((D_MODEL, D_FF)), _resident((D_FF, D_MODEL)),
            _resident((D_PLE, D_MODEL)), _resident((D_MODEL, D_MODEL)), vec, vec, vec,
        ],
        out_specs=row(D_MODEL),
        out_shape=jax.ShapeDtypeStruct((n, D_MODEL), F32),
        compiler_params=pltpu.CompilerParams(
            dimension_semantics=("arbitrary",), vmem_limit_bytes=VMEM_LIMIT),
        name="tail",
    )(x2, p2, o_gdn, o_fox, lin_g, lin_b, w_out, l1g, l1b, w_up, w_down, w_ple, w_gate,
      b_gate, l2g, l2b)


def kernel(x, p, ln_in_g, ln_in_b, w_in, conv_w, a_log, dt_bias, gdn_norm_g, b_f, fox_norm_g,
           w_out, ln1_g, ln1_b, w_up, w_down, w_ple, w_ple_gate, b_ple_gate, ln2_g, ln2_b):
    batch, seq, _ = x.shape
    assert x.shape[2] == D_MODEL and w_in.shape[0] == 1
    assert seq % FOX_TQ == 0 and seq % GDN_TB == 0 and (batch * seq) % PROJ_TM == 0
    n = batch * seq
    x2 = x.reshape(n, D_MODEL)
    p2 = p[0].reshape(n, D_PLE)
    row = lambda a: a.reshape(1, -1).astype(F32)

    w0 = w_in[0]
    w_big = jnp.concatenate([w0[:, :OFF_BETA], w0[:, OFF_FOX:OFF_F]], axis=1).astype(BF16)
    w_decay = w0[:, OFF_BETA + GDN_HEADS:OFF_FOX]
    n_gate = LANE_GTOT + GDN_HEADS
    w_small = jnp.concatenate(
        [w0[:, OFF_BETA:OFF_FOX], w0[:, OFF_F:], w_decay, jnp.zeros((D_MODEL, N_SMALL - n_gate), F32)],
        axis=1).astype(BF16)
    zeros4 = jnp.zeros((GDN_HEADS,), F32)
    pad = jnp.zeros((N_SMALL - n_gate,), F32)
    pvec = jnp.zeros((8, N_SMALL), F32)
    pvec = pvec.at[0].set(jnp.concatenate([zeros4, dt_bias[0], b_f[0], dt_bias[0], pad]))
    pvec = pvec.at[1].set(
        jnp.concatenate([zeros4, a_log[0], jnp.zeros((FOX_HEADS,), F32), a_log[0], pad]))

    proj_big, small = _proj(x2, row(ln_in_g), row(ln_in_b), w_big, w_small)
    gcol = _gates(small, pvec, batch, seq)

    grow = jnp.transpose(
        gcol[:, LANE_GAM:LANE_C].reshape(n // GDN_TB, GDN_TB, GDN_HEADS), (0, 2, 1))
    crow = jnp.transpose(
        gcol[:, LANE_C:LANE_GTOT].reshape(n // FOX_TQ, FOX_TQ, FOX_HEADS), (0, 2, 1))

    o_gdn = _gdn(proj_big, gcol, grow, conv_w[0], row(gdn_norm_g[0]), batch, seq)
    o_fox = _fox(proj_big, gcol, crow, row(jnp.tile(fox_norm_g[0], 2)), batch, seq)

    out = _tail(x2, p2, o_gdn, o_fox, row(ln_in_g), row(ln_in_b), w_out[0].astype(BF16),
                row(ln1_g[0]), row(ln1_b[0]), w_up[0].astype(BF16), w_down[0].astype(BF16),
                w_ple[0].astype(BF16), w_ple_gate[0].astype(BF16), row(b_ple_gate[0]),
                row(ln2_g[0]), row(ln2_b[0]))
    return out.reshape(batch, seq, D_MODEL)
```

```python
import functools

import jax
import jax.numpy as jnp
from jax import lax
from jax.experimental import pallas as pl
from jax.experimental.pallas import tpu as pltpu

F32 = jnp.float32
BF16 = jnp.bfloat16
HIGHEST = lax.Precision.HIGHEST

D_MODEL = 1024
CHUNK = 64
GDN_HEADS = 4
GDN_DK = 128
GDN_DV = 128
GDN_QK = GDN_HEADS * GDN_DK
GDN_QKV = 3 * GDN_QK
GDN_WIDTH = GDN_HEADS * GDN_DV
FOX_HEADS = 8
FOX_DH = 64
FOX_WIDTH = FOX_HEADS * FOX_DH
CONV_W = 4
D_FF = 4 * D_MODEL
D_PLE = 256
LN_EPS = 1e-5
NORM_EPS = 1e-6
ALPHA = 2.0 ** 0.25

OFF_Z = GDN_QKV
OFF_BETA = OFF_Z + GDN_WIDTH
OFF_FOX = OFF_BETA + 2 * GDN_HEADS
OFF_F = OFF_FOX + 3 * FOX_WIDTH
N_BIG = GDN_QKV + GDN_WIDTH + 3 * FOX_WIDTH
N_SMALL = 128
LANE_BETA = 0
LANE_GAM = GDN_HEADS
LANE_C = 2 * GDN_HEADS
LANE_GTOT = LANE_C + FOX_HEADS

VMEM_LIMIT = 56 * 1024 * 1024

PROJ_TM = 512
PROJ_TN = 512
GATE_TB = 256
GDN_TB = 256
FOX_TQ = 256
TAIL_TM = 512
TAIL_TF = 1024


def _layer_norm(x, g, b):
    mu = jnp.mean(x, -1, keepdims=True)
    xc = x - mu
    var = jnp.mean(xc * xc, -1, keepdims=True)
    return xc * lax.rsqrt(var + LN_EPS) * g + b


def _softplus(x):
    return jnp.maximum(x, 0.0) + jnp.log(1.0 + jnp.exp(-jnp.abs(x)))


def _sigmoid(x):
    return 1.0 / (1.0 + jnp.exp(-x))


def _dot(a, b):
    return jnp.dot(a, b, preferred_element_type=F32)


def _dot_hi(a, b):
    return jnp.dot(a, b, precision=HIGHEST, preferred_element_type=F32)


def _dot_nt(a, b):
    return lax.dot_general(a, b, (((1,), (1,)), ((), ())), preferred_element_type=F32)


def _dot_tn(a, b):
    return lax.dot_general(a, b, (((0,), (0,)), ((), ())), preferred_element_type=F32)


def _resident(shape):
    return pl.BlockSpec(shape, lambda *_: (0,) * len(shape), pipeline_mode=pl.Buffered(1))


def _proj_kernel(x_ref, g_ref, b_ref, wb_ref, ws_ref, ob_ref, os_ref):
    h = _layer_norm(x_ref[...], g_ref[...], b_ref[...]).astype(BF16)
    for j in range(N_BIG // PROJ_TN):
        cols = slice(j * PROJ_TN, (j + 1) * PROJ_TN)
        ob_ref[:, cols] = _dot(h, wb_ref[:, cols]).astype(BF16)
    os_ref[...] = _dot(h, ws_ref[...])


def _proj(x2, ln_g, ln_b, w_big, w_small):
    n = x2.shape[0]
    return pl.pallas_call(
        _proj_kernel,
        grid=(n // PROJ_TM,),
        in_specs=[
            pl.BlockSpec((PROJ_TM, D_MODEL), lambda i: (i, 0)),
            _resident((1, D_MODEL)),
            _resident((1, D_MODEL)),
            _resident((D_MODEL, N_BIG)),
            _resident((D_MODEL, N_SMALL)),
        ],
        out_specs=[
            pl.BlockSpec((PROJ_TM, N_BIG), lambda i: (i, 0)),
            pl.BlockSpec((PROJ_TM, N_SMALL), lambda i: (i, 0)),
        ],
        out_shape=[
            jax.ShapeDtypeStruct((n, N_BIG), BF16),
            jax.ShapeDtypeStruct((n, N_SMALL), F32),
        ],
        compiler_params=pltpu.CompilerParams(
            dimension_semantics=("arbitrary",), vmem_limit_bytes=VMEM_LIMIT),
        name="proj",
    )(x2, ln_g, ln_b, w_big, w_small)


def _gates_kernel(s_ref, pv_ref, o_ref, *, seq):
    lane = lax.broadcasted_iota(jnp.int32, (1, N_SMALL), 1)
    bias = pv_ref[0:1, :]
    neg_a = -jnp.exp(pv_ref[1:2, :])
    r = lax.broadcasted_iota(jnp.int32, (GATE_TB, GATE_TB), 0)
    c = lax.broadcasted_iota(jnp.int32, (GATE_TB, GATE_TB), 1)
    tri = r >= c
    l_full = jnp.where(tri, 1.0, 0.0).astype(F32)
    same_chunk = jnp.right_shift(r, 6) == jnp.right_shift(c, 6)
    l_chunk = jnp.where(same_chunk, l_full, 0.0)
    l_total = jnp.where(same_chunk, 1.0, 0.0).astype(F32)
    carry = jnp.zeros((1, N_SMALL), F32)
    for t in range(seq // GATE_TB):
        rows = slice(t * GATE_TB, (t + 1) * GATE_TB)
        x = s_ref[rows, :] + bias
        beta = _sigmoid(x)
        log_g = neg_a * _softplus(x)
        log_f = -_softplus(-x)
        gam = _dot_hi(l_chunk, log_g)
        gtot = _dot_hi(l_total, log_g)
        cum = _dot_hi(l_full, log_f) + carry
        carry = cum[GATE_TB - 1:GATE_TB, :]
        o_ref[rows, :] = jnp.where(
            lane < LANE_GAM, beta,
            jnp.where(lane < LANE_C, gam, jnp.where(lane < LANE_GTOT, cum, gtot)))


def _gates(small, pvec, batch, seq):
    return pl.pallas_call(
        functools.partial(_gates_kernel, seq=seq),
        grid=(batch,),
        in_specs=[
            pl.BlockSpec((seq, N_SMALL), lambda b: (b, 0)),
            _resident((8, N_SMALL)),
        ],
        out_specs=pl.BlockSpec((seq, N_SMALL), lambda b: (b, 0)),
        out_shape=jax.ShapeDtypeStruct((batch * seq, N_SMALL), F32),
        compiler_params=pltpu.CompilerParams(
            dimension_semantics=("arbitrary",), vmem_limit_bytes=VMEM_LIMIT),
        name="gates",
    )(small, pvec)


def _gdn_kernel(x_ref, z_ref, gc_ref, gr_ref, cw_ref, ng_ref, o_ref, xs_ref, s_ref):
    tb = x_ref.shape[0]
    nch = tb // CHUNK
    halo = 8
    heads = range(GDN_HEADS)

    @pl.when(pl.program_id(1) == 0)
    def _():
        xs_ref[0:halo, :] = jnp.zeros((halo, GDN_QKV), F32)
        s_ref[...] = jnp.zeros_like(s_ref)

    xs_ref[halo:halo + tb, :] = x_ref[...].astype(F32)

    def conv_silu(c0):
        cols = slice(c0, c0 + GDN_DK)
        acc = xs_ref[halo:halo + tb, cols] * cw_ref[CONV_W - 1:CONV_W, cols]
        for j in range(1, CONV_W):
            acc = acc + xs_ref[halo - j:halo - j + tb, cols] * cw_ref[CONV_W - 1 - j:CONV_W - j, cols]
        return acc * _sigmoid(acc)

    def l2norm(v):
        return v * lax.rsqrt(jnp.sum(v * v, -1, keepdims=True) + NORM_EPS)

    ri = lax.broadcasted_iota(jnp.int32, (tb, tb), 0)
    ci = lax.broadcasted_iota(jnp.int32, (tb, tb), 1)
    same = jnp.right_shift(ri, 6) == jnp.right_shift(ci, 6)
    keep = jnp.logical_and(same, ri >= ci)
    diag = ri == ci
    pr = lax.broadcasted_iota(jnp.int32, (CHUNK, tb), 0)
    pc = lax.broadcasted_iota(jnp.int32, (CHUNK, tb), 1)
    eye_packed = jnp.where(jnp.bitwise_and(pc, CHUNK - 1) == pr, 1.0, 0.0).astype(F32)
    lane_chunk = jnp.right_shift(lax.broadcasted_iota(jnp.int32, (1, tb), 1), 6)
    row_chunk = jnp.right_shift(lax.broadcasted_iota(jnp.int32, (tb, 1), 0), 6)

    def to_bd(packed):
        return jnp.where(same, jnp.concatenate([packed] * nch, axis=0), jnp.zeros((), packed.dtype))

    def to_packed(bd):
        out = bd[0:CHUNK]
        for c in range(1, nch):
            out = out + bd[c * CHUNK:(c + 1) * CHUNK]
        return out

    ng = ng_ref[...]
    q = [l2norm(conv_silu(h * GDN_DK)) * (GDN_DK ** -0.5) for h in heads]
    k = [l2norm(conv_silu(GDN_QK + h * GDN_DK)) for h in heads]
    v = [conv_silu(2 * GDN_QK + h * GDN_DV) for h in heads]
    beta = [gc_ref[:, LANE_BETA + h:LANE_BETA + h + 1] for h in heads]
    gam = [gc_ref[:, LANE_GAM + h:LANE_GAM + h + 1] for h in heads]
    gtot = [gc_ref[:, LANE_GTOT + h:LANE_GTOT + h + 1] for h in heads]
    grow = [gr_ref[0, h:h + 1, :] for h in heads]
    egam = [jnp.exp(g) for g in gam]
    kb = [a.astype(BF16) for a in k]
    decay = [jnp.exp(jnp.where(keep, gam[h] - grow[h], -jnp.inf)) for h in heads]
    gram = [_dot_nt(kb[h], kb[h]) for h in heads]
    qk = [(_dot_nt(q[h].astype(BF16), kb[h]) * decay[h]).astype(BF16) for h in heads]

    x_bd = [jnp.where(diag, 0.0, -(gram[h] * beta[h]) * decay[h]) for h in heads]
    x_p = [to_packed(a) for a in x_bd]
    p_p = [eye_packed + a for a in x_p]
    x_p = [_dot(x_p[h].astype(BF16), x_bd[h].astype(BF16)) for h in heads]
    for _ in range(4):
        w_bd = [to_bd(a.astype(BF16)) for a in x_p]
        r = [_dot(jnp.concatenate([p_p[h], x_p[h]], axis=0).astype(BF16), w_bd[h]) for h in heads]
        p_p = [p_p[h] + r[h][:CHUNK] for h in heads]
        x_p = [r[h][CHUNK:] for h in heads]
    p_p = [p_p[h] + _dot(p_p[h].astype(BF16), to_bd(x_p[h].astype(BF16))) for h in heads]

    rhs = [jnp.concatenate([v[h] * beta[h], k[h] * (beta[h] * egam[h])], axis=1).astype(BF16)
           for h in heads]
    sol = [_dot(to_bd(p_p[h].astype(BF16)), rhs[h]) for h in heads]
    u = [a[:, :GDN_DV] for a in sol]
    w = [a[:, GDN_DV:].astype(BF16) for a in sol]
    qd = [(q[h] * egam[h]).astype(BF16) for h in heads]
    kd_t = [jnp.transpose(k[h] * jnp.exp(gtot[h] - gam[h])).astype(BF16) for h in heads]
    z = [z_ref[:, h * GDN_DV:(h + 1) * GDN_DV].astype(F32) for h in heads]
    zgate = [a * _sigmoid(a) for a in z]

    state = [s_ref[h] for h in heads]
    for c in range(nch):
        rows = slice(c * CHUNK, (c + 1) * CHUNK)
        sb = [a.astype(BF16) for a in state]
        r1 = [_dot(jnp.concatenate([w[h][rows], qd[h][rows]], axis=0), sb[h]) for h in heads]
        v_new = [u[h][rows] - r1[h][:CHUNK] for h in heads]
        v_full = [jnp.where(row_chunk == c, jnp.concatenate([a.astype(BF16)] * nch, axis=0),
                            jnp.zeros((), BF16)) for a in v_new]
        kd_c = [jnp.where(lane_chunk == c, a, jnp.zeros((), BF16)) for a in kd_t]
        r2 = [_dot(jnp.concatenate([qk[h][rows], kd_c[h]], axis=0), v_full[h]) for h in heads]
        for h in heads:
            o = r1[h][CHUNK:] + r2[h][:CHUNK]
            on = o * lax.rsqrt(jnp.mean(o * o, -1, keepdims=True) + NORM_EPS) * ng
            o_ref[rows, h * GDN_DV:(h + 1) * GDN_DV] = (on * zgate[h][rows]).astype(BF16)
        state = [state[h] * jnp.exp(gtot[h][c * CHUNK:c * CHUNK + 1]) + r2[h][CHUNK:] for h in heads]
    for h in heads:
        s_ref[h] = state[h]

    xs_ref[0:halo, :] = xs_ref[tb:tb + halo, :]


def _gdn(proj_big, gcol, grow, conv_w, norm_g, batch, seq):
    nt = seq // GDN_TB
    return pl.pallas_call(
        _gdn_kernel,
        grid=(batch, nt),
        in_specs=[
            pl.BlockSpec((GDN_TB, GDN_QKV), lambda b, t: (b * nt + t, 0)),
            pl.BlockSpec((GDN_TB, GDN_WIDTH), lambda b, t: (b * nt + t, OFF_Z // GDN_WIDTH)),
            pl.BlockSpec((GDN_TB, N_SMALL), lambda b, t: (b * nt + t, 0)),
            pl.BlockSpec((1, GDN_HEADS, GDN_TB), lambda b, t: (b * nt + t, 0, 0)),
            _resident((CONV_W, GDN_QKV)),
            _resident((1, GDN_DV)),
        ],
        out_specs=pl.BlockSpec((GDN_TB, GDN_WIDTH), lambda b, t: (b * nt + t, 0)),
        out_shape=jax.ShapeDtypeStruct((batch * seq, GDN_WIDTH), BF16),
        scratch_shapes=[
            pltpu.VMEM((8 + GDN_TB, GDN_QKV), F32),
            pltpu.VMEM((GDN_HEADS, GDN_DK, GDN_DV), F32),
        ],
        compiler_params=pltpu.CompilerParams(
            dimension_semantics=("arbitrary", "arbitrary"), vmem_limit_bytes=VMEM_LIMIT),
        name="gdn",
    )(proj_big, proj_big, gcol, grow, conv_w, norm_g)


FOX_X = 128
FOX_QC = (0, 6)
FOX_KC = (3, 9)


def _split3(c):
    hi = c.astype(BF16)
    r1 = c - hi.astype(F32)
    mid = r1.astype(BF16)
    lo = (r1 - mid.astype(F32)).astype(BF16)
    return jnp.concatenate([hi, mid, lo], axis=1)


def _select3(pair, base):
    r = lax.broadcasted_iota(jnp.int32, (3 * N_SMALL, FOX_X), 0)
    c = lax.broadcasted_iota(jnp.int32, (3 * N_SMALL, FOX_X), 1)
    d = jnp.right_shift(r, 7)
    head = jnp.bitwise_and(r, N_SMALL - 1) - (LANE_C + 2 * pair)
    e = jnp.where(head == 0, jnp.where(c == base[0] + d, 1.0, 0.0),
                  jnp.where(head == 1, jnp.where(c == base[1] + d, 1.0, 0.0), 0.0))
    return e.astype(BF16)


def _lane_ones(first):
    lane = lax.broadcasted_iota(jnp.int32, (1, FOX_X), 1)
    out = jnp.zeros((1, FOX_X), F32)
    for f in first:
        out = jnp.where(lane < f, out, jnp.where(lane < f + 3, 1.0, out))
    return out


def _fox_kernel(q_ref, k_ref, v_ref, gc_ref, ng_ref, o_ref, qa_ref, ka_ref, m_ref, l_ref, acc_ref):
    tq = q_ref.shape[0]
    seq = k_ref.shape[0]
    i = pl.program_id(1)
    pairs = range(FOX_HEADS // 2)
    lane = lax.broadcasted_iota(jnp.int32, (1, 2 * FOX_DH), 1)
    lo_half = lane < FOX_DH
    pcols = [slice(p * 2 * FOX_DH, (p + 1) * 2 * FOX_DH) for p in pairs]

    @pl.when(i == 0)
    def _():
        ones_k = _lane_ones(FOX_QC)
        sel_k = [_select3(p, FOX_KC) for p in pairs]

        def fill(t, carry):
            rows = pl.ds(pl.multiple_of(t * tq, tq), tq)
            pieces = _split3(gc_ref[rows, :])
            for p in pairs:
                ka_ref[p, rows, 0:2 * FOX_DH] = k_ref[rows, pcols[p]]
                ka_ref[p, rows, 2 * FOX_DH:] = (ones_k - _dot(pieces, sel_k[p])).astype(BF16)
            return carry

        lax.fori_loop(0, seq // tq, fill, 0)

    qrows = pl.ds(pl.multiple_of(i * tq, tq), tq)
    pieces_q = _split3(gc_ref[qrows, :])
    ones_q = _lane_ones(FOX_KC)
    xlane = lax.broadcasted_iota(jnp.int32, (1, FOX_X), 1)
    for p in pairs:
        qp = q_ref[:, pcols[p]] * (FOX_DH ** -0.5)
        qx = _dot(pieces_q, _select3(p, FOX_QC)) + ones_q
        for half in range(2):
            rows = slice(half * tq, (half + 1) * tq)
            mine = lo_half if half == 0 else jnp.logical_not(lo_half)
            xmine = (xlane < FOX_QC[1]) if half == 0 else (xlane >= FOX_QC[1])
            qa_ref[p, rows, 0:2 * FOX_DH] = jnp.where(mine, qp, jnp.zeros((), BF16))
            qa_ref[p, rows, 2 * FOX_DH:] = jnp.where(xmine, qx, 0.0).astype(BF16)
    m_ref[...] = jnp.full(m_ref.shape, -jnp.inf, F32)
    l_ref[...] = jnp.zeros(l_ref.shape, F32)
    acc_ref[...] = jnp.zeros(acc_ref.shape, F32)

    ri = jnp.bitwise_and(lax.broadcasted_iota(jnp.int32, (2 * tq, tq), 0), tq - 1)
    ci = lax.broadcasted_iota(jnp.int32, (2 * tq, tq), 1)
    causal = ri >= ci

    def tile(j, masked):
        krows = pl.ds(pl.multiple_of(j * tq, tq), tq)
        s = [_dot_nt(qa_ref[p], ka_ref[p, krows, :]) for p in pairs]
        if masked:
            s = [jnp.where(causal, a, -jnp.inf) for a in s]
        m_prev = [m_ref[p] for p in pairs]
        m_new = [jnp.maximum(m_prev[p], jnp.max(s[p], -1, keepdims=True)) for p in pairs]
        scale = [jnp.exp(m_prev[p] - m_new[p]) for p in pairs]
        prob = [jnp.exp(s[p] - jnp.concatenate([m_new[p]] * (tq // (2 * FOX_DH)), axis=1))
                for p in pairs]
        for p in pairs:
            m_ref[p] = m_new[p]
            l_ref[p] = scale[p] * l_ref[p] + jnp.sum(prob[p], -1, keepdims=True)
            acc_ref[p] = scale[p] * acc_ref[p] + _dot(prob[p].astype(BF16), v_ref[krows, pcols[p]])

    def body(j, carry):
        tile(j, False)
        return carry

    lax.fori_loop(0, i, body, 0)
    tile(i, True)

    ng = ng_ref[...]
    for p in pairs:
        o = acc_ref[p] / l_ref[p]
        outs = []
        for half in range(2):
            oh = o[half * tq:(half + 1) * tq]
            mine = lo_half if half == 0 else jnp.logical_not(lo_half)
            ms = jnp.sum(jnp.where(mine, oh * oh, 0.0), -1, keepdims=True) * (1.0 / FOX_DH)
            outs.append(oh * lax.rsqrt(ms + NORM_EPS) * ng)
        o_ref[:, pcols[p]] = jnp.where(lo_half, outs[0], outs[1]).astype(BF16)


def _fox(proj_big, gcol, norm_g2, batch, seq):
    nq = seq // FOX_TQ
    first = (GDN_QKV + GDN_WIDTH) // FOX_WIDTH
    npair = FOX_HEADS // 2
    return pl.pallas_call(
        _fox_kernel,
        grid=(batch, nq),
        in_specs=[
            pl.BlockSpec((FOX_TQ, FOX_WIDTH), lambda b, i: (b * nq + i, first)),
            pl.BlockSpec((seq, FOX_WIDTH), lambda b, i: (b, first + 1)),
            pl.BlockSpec((seq, FOX_WIDTH), lambda b, i: (b, first + 2)),
            pl.BlockSpec((seq, N_SMALL), lambda b, i: (b, 0)),
            _resident((1, 2 * FOX_DH)),
        ],
        out_specs=pl.BlockSpec((FOX_TQ, FOX_WIDTH), lambda b, i: (b * nq + i, 0)),
        out_shape=jax.ShapeDtypeStruct((batch * seq, FOX_WIDTH), BF16),
        scratch_shapes=[
            pltpu.VMEM((npair, 2 * FOX_TQ, 2 * FOX_DH + FOX_X), BF16),
            pltpu.VMEM((npair, seq, 2 * FOX_DH + FOX_X), BF16),
            pltpu.VMEM((npair, 2 * FOX_TQ, 2 * FOX_DH), F32),
            pltpu.VMEM((npair, 2 * FOX_TQ, 2 * FOX_DH), F32),
            pltpu.VMEM((npair, 2 * FOX_TQ, 2 * FOX_DH), F32),
        ],
        compiler_params=pltpu.CompilerParams(
            dimension_semantics=("arbitrary", "arbitrary"), vmem_limit_bytes=VMEM_LIMIT),
        name="fox",
    )(proj_big, proj_big, proj_big, gcol, norm_g2)


def _tail_kernel(x_ref, p_ref, og_ref, of_ref, lin_g, lin_b, wo_ref, l1g, l1b, wu_ref, wd_ref,
                 wp_ref, wg_ref, bg_ref, l2g, l2b, o_ref):
    h = _layer_norm(x_ref[...], lin_g[...], lin_b[...])
    mix = _dot(og_ref[...], wo_ref[0:GDN_WIDTH, :]) + _dot(of_ref[...], wo_ref[GDN_WIDTH:, :])
    h1 = _layer_norm(ALPHA * h + mix, l1g[...], l1b[...])
    h1b = h1.astype(BF16)
    gate = _sigmoid(_dot(h1b, wg_ref[...]) + bg_ref[...])
    acc = ALPHA * h1 + _dot(p_ref[...].astype(BF16), wp_ref[...]) * gate
    for j in range(D_FF // TAIL_TF):
        cols = slice(j * TAIL_TF, (j + 1) * TAIL_TF)
        a = jnp.maximum(_dot(h1b, wu_ref[:, cols]), 0.0)
        acc = acc + _dot((a * a).astype(BF16), wd_ref[cols, :])
    o_ref[...] = _layer_norm(acc, l2g[...], l2b[...])


def _tail(x2, p2, o_gdn, o_fox, lin_g, lin_b, w_out, l1g, l1b, w_up, w_down, w_ple, w_gate,
          b_gate, l2g, l2b):
    n = x2.shape[0]
    row = lambda width: pl.BlockSpec((TAIL_TM, width), lambda i: (i, 0))
    vec = _resident((1, D_MODEL))
    return pl.pallas_call(
        _tail_kernel,
        grid=(n // TAIL_TM,),
        in_specs=[
            row(D_MODEL), row(D_PLE), row(GDN_WIDTH), row(FOX_WIDTH),
            vec, vec, _resident((D_MODEL, D_MODEL)), vec, vec,
            _resident((D_MODEL, D_FF)), _resident((D_FF, D_MODEL)),
            _resident((D_PLE, D_MODEL)), _resident((D_MODEL, D_MODEL)), vec, vec, vec,
        ],
        out_specs=row(D_MODEL),
        out_shape=jax.ShapeDtypeStruct((n, D_MODEL), F32),
        compiler_params=pltpu.CompilerParams(
            dimension_semantics=("arbitrary",), vmem_limit_bytes=VMEM_LIMIT),
        name="tail",
    )(x2, p2, o_gdn, o_fox, lin_g, lin_b, w_out, l1g, l1b, w_up, w_down, w_ple, w_gate,
      b_gate, l2g, l2b)


def kernel(x, p, ln_in_g, ln_in_b, w_in, conv_w, a_log, dt_bias, gdn_norm_g, b_f, fox_norm_g,
           w_out, ln1_g, ln1_b, w_up, w_down, w_ple, w_ple_gate, b_ple_gate, ln2_g, ln2_b):
    batch, seq, _ = x.shape
    assert x.shape[2] == D_MODEL and w_in.shape[0] == 1
    assert seq % FOX_TQ == 0 and seq % GDN_TB == 0 and (batch * seq) % PROJ_TM == 0
    n = batch * seq
    x2 = x.reshape(n, D_MODEL)
    p2 = p[0].reshape(n, D_PLE)
    row = lambda a: a.reshape(1, -1).astype(F32)

    w0 = w_in[0]
    w_big = jnp.concatenate([w0[:, :OFF_BETA], w0[:, OFF_FOX:OFF_F]], axis=1).astype(BF16)
    w_decay = w0[:, OFF_BETA + GDN_HEADS:OFF_FOX]
    n_gate = LANE_GTOT + GDN_HEADS
    w_small = jnp.concatenate(
        [w0[:, OFF_BETA:OFF_FOX], w0[:, OFF_F:], w_decay, jnp.zeros((D_MODEL, N_SMALL - n_gate), F32)],
        axis=1).astype(BF16)
    zeros4 = jnp.zeros((GDN_HEADS,), F32)
    pad = jnp.zeros((N_SMALL - n_gate,), F32)
    pvec = jnp.zeros((8, N_SMALL), F32)
    pvec = pvec.at[0].set(jnp.concatenate([zeros4, dt_bias[0], b_f[0], dt_bias[0], pad]))
    pvec = pvec.at[1].set(
        jnp.concatenate([zeros4, a_log[0], jnp.zeros((FOX_HEADS,), F32), a_log[0], pad]))

    proj_big, small = _proj(x2, row(ln_in_g), row(ln_in_b), w_big, w_small)
    gcol = _gates(small, pvec, batch, seq)

    grow = jnp.transpose(
        gcol[:, LANE_GAM:LANE_C].reshape(n // GDN_TB, GDN_TB, GDN_HEADS), (0, 2, 1))

    o_gdn = _gdn(proj_big, gcol, grow, conv_w[0], row(gdn_norm_g[0]), batch, seq)
    o_fox = _fox(proj_big, gcol, row(jnp.tile(fox_norm_g[0], 2)), batch, seq)

    out = _tail(x2, p2, o_gdn, o_fox, row(ln_in_g), row(ln_in_b), w_out[0].astype(BF16),
                row(ln1_g[0]), row(ln1_b[0]), w_up[0].astype(BF16), w_down[0].astype(BF16),
                w_ple[0].astype(BF16), w_ple_gate[0].astype(BF16), row(b_ple_gate[0]),
                row(ln2_g[0]), row(ln2_b[0]))
    return out.reshape(batch, seq, D_MODEL)
```

```python
import functools

import jax
import jax.numpy as jnp
from jax import lax
from jax.experimental import pallas as pl
from jax.experimental.pallas import tpu as pltpu

F32 = jnp.float32
BF16 = jnp.bfloat16
HIGHEST = lax.Precision.HIGHEST

D_MODEL = 1024
CHUNK = 64
GDN_HEADS = 4
GDN_DK = 128
GDN_DV = 128
GDN_QK = GDN_HEADS * GDN_DK
GDN_QKV = 3 * GDN_QK
GDN_WIDTH = GDN_HEADS * GDN_DV
FOX_HEADS = 8
FOX_DH = 64
FOX_WIDTH = FOX_HEADS * FOX_DH
CONV_W = 4
D_FF = 4 * D_MODEL
D_PLE = 256
LN_EPS = 1e-5
NORM_EPS = 1e-6
ALPHA = 2.0 ** 0.25

OFF_Z = GDN_QKV
OFF_BETA = OFF_Z + GDN_WIDTH
OFF_FOX = OFF_BETA + 2 * GDN_HEADS
OFF_F = OFF_FOX + 3 * FOX_WIDTH
N_BIG = GDN_QKV + GDN_WIDTH + 3 * FOX_WIDTH
N_SMALL = 128
LANE_BETA = 0
LANE_GAM = GDN_HEADS
LANE_C = 2 * GDN_HEADS
LANE_GTOT = LANE_C + FOX_HEADS

VMEM_LIMIT = 56 * 1024 * 1024

PROJ_TM = 512
PROJ_TN = 256
GATE_TB = 256
GDN_TB = 256
FOX_TQ = 256
TAIL_TM = 512
TAIL_TF = 1024


def _layer_norm(x, g, b):
    mu = jnp.mean(x, -1, keepdims=True)
    xc = x - mu
    var = jnp.mean(xc * xc, -1, keepdims=True)
    return xc * lax.rsqrt(var + LN_EPS) * g + b


def _softplus(x):
    return jnp.maximum(x, 0.0) + jnp.log(1.0 + jnp.exp(-jnp.abs(x)))


def _sigmoid(x):
    return 1.0 / (1.0 + jnp.exp(-x))


def _dot(a, b):
    return jnp.dot(a, b, preferred_element_type=F32)


def _dot_hi(a, b):
    return jnp.dot(a, b, precision=HIGHEST, preferred_element_type=F32)


def _dot_nt(a, b):
    return lax.dot_general(a, b, (((1,), (1,)), ((), ())), preferred_element_type=F32)


def _dot_tn(a, b):
    return lax.dot_general(a, b, (((0,), (0,)), ((), ())), preferred_element_type=F32)


def _resident(shape):
    return pl.BlockSpec(shape, lambda *_: (0,) * len(shape), pipeline_mode=pl.Buffered(1))


def _proj_kernel(x_ref, g_ref, b_ref, wb_ref, ws_ref, cw_ref, ob_ref, os_ref, halo_ref, *,
                 tiles_per_seq):
    tm = x_ref.shape[0]
    halo = halo_ref.shape[0]

    @pl.when(pl.program_id(0) % tiles_per_seq == 0)
    def _():
        halo_ref[...] = jnp.zeros_like(halo_ref)

    h = _layer_norm(x_ref[...], g_ref[...], b_ref[...]).astype(BF16)
    nchunk = N_BIG // PROJ_TN
    nconv = GDN_QKV // PROJ_TN
    order = [c for pair in zip(range(nconv), range(nconv, 2 * nconv)) for c in pair]
    order += list(range(2 * nconv, nchunk))
    pending = _dot(h, wb_ref[:, order[0] * PROJ_TN:(order[0] + 1) * PROJ_TN])
    for pos, j in enumerate(order):
        cols = slice(j * PROJ_TN, (j + 1) * PROJ_TN)
        acc = pending
        if pos + 1 < nchunk:
            nxt = order[pos + 1]
            pending = _dot(h, wb_ref[:, nxt * PROJ_TN:(nxt + 1) * PROJ_TN])
        if (j + 1) * PROJ_TN <= GDN_QKV:
            ext = jnp.concatenate([halo_ref[:, cols], acc], axis=0)
            y = acc * cw_ref[CONV_W - 1:CONV_W, cols]
            for d in range(1, CONV_W):
                y = y + pltpu.roll(ext, d, 0)[halo:] * cw_ref[CONV_W - 1 - d:CONV_W - d, cols]
            halo_ref[:, cols] = acc[tm - halo:]
            acc = y * _sigmoid(y)
        ob_ref[:, cols] = acc.astype(BF16)
    os_ref[...] = _dot(h, ws_ref[...])


def _proj(x2, ln_g, ln_b, w_big, w_small, conv_w, seq):
    n = x2.shape[0]
    assert GDN_QKV % PROJ_TN == 0 and seq % PROJ_TM == 0
    return pl.pallas_call(
        functools.partial(_proj_kernel, tiles_per_seq=seq // PROJ_TM),
        grid=(n // PROJ_TM,),
        in_specs=[
            pl.BlockSpec((PROJ_TM, D_MODEL), lambda i: (i, 0)),
            _resident((1, D_MODEL)),
            _resident((1, D_MODEL)),
            _resident((D_MODEL, N_BIG)),
            _resident((D_MODEL, N_SMALL)),
            _resident((CONV_W, GDN_QKV)),
        ],
        out_specs=[
            pl.BlockSpec((PROJ_TM, N_BIG), lambda i: (i, 0)),
            pl.BlockSpec((PROJ_TM, N_SMALL), lambda i: (i, 0)),
        ],
        out_shape=[
            jax.ShapeDtypeStruct((n, N_BIG), BF16),
            jax.ShapeDtypeStruct((n, N_SMALL), F32),
        ],
        scratch_shapes=[pltpu.VMEM((8, GDN_QKV), F32)],
        compiler_params=pltpu.CompilerParams(
            dimension_semantics=("arbitrary",), vmem_limit_bytes=VMEM_LIMIT),
        name="proj",
    )(x2, ln_g, ln_b, w_big, w_small, conv_w)


def _gates_kernel(s_ref, pv_ref, o_ref, *, seq):
    lane = lax.broadcasted_iota(jnp.int32, (1, N_SMALL), 1)
    bias = pv_ref[0:1, :]
    neg_a = -jnp.exp(pv_ref[1:2, :])
    r = lax.broadcasted_iota(jnp.int32, (GATE_TB, GATE_TB), 0)
    c = lax.broadcasted_iota(jnp.int32, (GATE_TB, GATE_TB), 1)
    tri = r >= c
    l_full = jnp.where(tri, 1.0, 0.0).astype(F32)
    same_chunk = jnp.right_shift(r, 6) == jnp.right_shift(c, 6)
    l_chunk = jnp.where(same_chunk, l_full, 0.0)
    l_total = jnp.where(same_chunk, 1.0, 0.0).astype(F32)
    carry = jnp.zeros((1, N_SMALL), F32)
    for t in range(seq // GATE_TB):
        rows = slice(t * GATE_TB, (t + 1) * GATE_TB)
        x = s_ref[rows, :] + bias
        beta = _sigmoid(x)
        log_g = neg_a * _softplus(x)
        log_f = -_softplus(-x)
        gam = _dot_hi(l_chunk, log_g)
        gtot = _dot_hi(l_total, log_g)
        cum = _dot_hi(l_full, log_f) + carry
        carry = cum[GATE_TB - 1:GATE_TB, :]
        o_ref[rows, :] = jnp.where(
            lane < LANE_GAM, beta,
            jnp.where(lane < LANE_C, gam, jnp.where(lane < LANE_GTOT, cum, gtot)))


def _gates(small, pvec, batch, seq):
    return pl.pallas_call(
        functools.partial(_gates_kernel, seq=seq),
        grid=(batch,),
        in_specs=[
            pl.BlockSpec((seq, N_SMALL), lambda b: (b, 0)),
            _resident((8, N_SMALL)),
        ],
        out_specs=pl.BlockSpec((seq, N_SMALL), lambda b: (b, 0)),
        out_shape=jax.ShapeDtypeStruct((batch * seq, N_SMALL), F32),
        compiler_params=pltpu.CompilerParams(
            dimension_semantics=("arbitrary",), vmem_limit_bytes=VMEM_LIMIT),
        name="gates",
    )(small, pvec)


def _gdn_kernel(x_ref, z_ref, gc_ref, gr_ref, ng_ref, o_ref, s_ref):
    tb = x_ref.shape[0]
    nch = tb // CHUNK
    heads = range(GDN_HEADS)

    @pl.when(pl.program_id(1) == 0)
    def _():
        s_ref[...] = jnp.zeros_like(s_ref)

    def act(c0):
        return x_ref[:, c0:c0 + GDN_DK].astype(F32)

    def l2norm(v):
        return v * lax.rsqrt(jnp.sum(v * v, -1, keepdims=True) + NORM_EPS)

    ri = lax.broadcasted_iota(jnp.int32, (tb, tb), 0)
    ci = lax.broadcasted_iota(jnp.int32, (tb, tb), 1)
    same = jnp.right_shift(ri, 6) == jnp.right_shift(ci, 6)
    keep = jnp.logical_and(same, ri >= ci)
    diag = ri == ci
    pr = lax.broadcasted_iota(jnp.int32, (CHUNK, tb), 0)
    pc = lax.broadcasted_iota(jnp.int32, (CHUNK, tb), 1)
    eye_packed = jnp.where(jnp.bitwise_and(pc, CHUNK - 1) == pr, 1.0, 0.0).astype(F32)
    lane_chunk = jnp.right_shift(lax.broadcasted_iota(jnp.int32, (1, tb), 1), 6)
    row_chunk = jnp.right_shift(lax.broadcasted_iota(jnp.int32, (tb, 1), 0), 6)

    def to_bd(packed):
        return jnp.where(same, jnp.concatenate([packed] * nch, axis=0), jnp.zeros((), packed.dtype))

    def to_packed(bd):
        out = bd[0:CHUNK]
        for c in range(1, nch):
            out = out + bd[c * CHUNK:(c + 1) * CHUNK]
        return out

    ng = ng_ref[...]
    q = [l2norm(act(h * GDN_DK)) * (GDN_DK ** -0.5) for h in heads]
    k = [l2norm(act(GDN_QK + h * GDN_DK)) for h in heads]
    v = [act(2 * GDN_QK + h * GDN_DV) for h in heads]
    beta = [gc_ref[:, LANE_BETA + h:LANE_BETA + h + 1] for h in heads]
    gam = [gc_ref[:, LANE_GAM + h:LANE_GAM + h + 1] for h in heads]
    gtot = [gc_ref[:, LANE_GTOT + h:LANE_GTOT + h + 1] for h in heads]
    grow = [gr_ref[0, h:h + 1, :] for h in heads]
    egam = [jnp.exp(g) for g in gam]
    kb = [a.astype(BF16) for a in k]
    decay = [jnp.exp(jnp.where(keep, gam[h] - grow[h], -jnp.inf)) for h in heads]
    gram = [_dot_nt(kb[h], kb[h]) for h in heads]
    qk = [(_dot_nt(q[h].astype(BF16), kb[h]) * decay[h]).astype(BF16) for h in heads]

    x_bd = [jnp.where(diag, 0.0, -(gram[h] * beta[h]) * decay[h]) for h in heads]
    x_p = [to_packed(a) for a in x_bd]
    p_p = [eye_packed + a for a in x_p]
    x_p = [_dot(x_p[h].astype(BF16), x_bd[h].astype(BF16)) for h in heads]
    for _ in range(4):
        w_bd = [to_bd(a.astype(BF16)) for a in x_p]
        r = [_dot(jnp.concatenate([p_p[h], x_p[h]], axis=0).astype(BF16), w_bd[h]) for h in heads]
        p_p = [p_p[h] + r[h][:CHUNK] for h in heads]
        x_p = [r[h][CHUNK:] for h in heads]
    p_p = [p_p[h] + _dot(p_p[h].astype(BF16), to_bd(x_p[h].astype(BF16))) for h in heads]

    rhs = [jnp.concatenate([v[h] * beta[h], k[h] * (beta[h] * egam[h])], axis=1).astype(BF16)
           for h in heads]
    sol = [_dot(to_bd(p_p[h].astype(BF16)), rhs[h]) for h in heads]
    u = [a[:, :GDN_DV] for a in sol]
    w = [a[:, GDN_DV:].astype(BF16) for a in sol]
    qd = [(q[h] * egam[h]).astype(BF16) for h in heads]
    kd_t = [jnp.transpose(k[h] * jnp.exp(gtot[h] - gam[h])).astype(BF16) for h in heads]
    z = [z_ref[:, h * GDN_DV:(h + 1) * GDN_DV].astype(F32) for h in heads]
    zgate = [a * _sigmoid(a) for a in z]

    state = [s_ref[h] for h in heads]
    for c in range(nch):
        rows = slice(c * CHUNK, (c + 1) * CHUNK)
        sb = [a.astype(BF16) for a in state]
        r1 = [_dot(jnp.concatenate([w[h][rows], qd[h][rows]], axis=0), sb[h]) for h in heads]
        v_new = [u[h][rows] - r1[h][:CHUNK] for h in heads]
        v_full = [jnp.where(row_chunk == c, jnp.concatenate([a.astype(BF16)] * nch, axis=0),
                            jnp.zeros((), BF16)) for a in v_new]
        kd_c = [jnp.where(lane_chunk == c, a, jnp.zeros((), BF16)) for a in kd_t]
        r2 = [_dot(jnp.concatenate([qk[h][rows], kd_c[h]], axis=0), v_full[h]) for h in heads]
        for h in heads:
            o = r1[h][CHUNK:] + r2[h][:CHUNK]
            on = o * lax.rsqrt(jnp.mean(o * o, -1, keepdims=True) + NORM_EPS) * ng
            o_ref[rows, h * GDN_DV:(h + 1) * GDN_DV] = (on * zgate[h][rows]).astype(BF16)
        state = [state[h] * jnp.exp(gtot[h][c * CHUNK:c * CHUNK + 1]) + r2[h][CHUNK:] for h in heads]
    for h in heads:
        s_ref[h] = state[h]


def _gdn(proj_big, gcol, grow, norm_g, batch, seq):
    nt = seq // GDN_TB
    return pl.pallas_call(
        _gdn_kernel,
        grid=(batch, nt),
        in_specs=[
            pl.BlockSpec((GDN_TB, GDN_QKV), lambda b, t: (b * nt + t, 0)),
            pl.BlockSpec((GDN_TB, GDN_WIDTH), lambda b, t: (b * nt + t, OFF_Z // GDN_WIDTH)),
            pl.BlockSpec((GDN_TB, N_SMALL), lambda b, t: (b * nt + t, 0)),
            pl.BlockSpec((1, GDN_HEADS, GDN_TB), lambda b, t: (b * nt + t, 0, 0)),
            _resident((1, GDN_DV)),
        ],
        out_specs=pl.BlockSpec((GDN_TB, GDN_WIDTH), lambda b, t: (b * nt + t, 0)),
        out_shape=jax.ShapeDtypeStruct((batch * seq, GDN_WIDTH), BF16),
        scratch_shapes=[
            pltpu.VMEM((GDN_HEADS, GDN_DK, GDN_DV), F32),
        ],
        compiler_params=pltpu.CompilerParams(
            dimension_semantics=("arbitrary", "arbitrary"), vmem_limit_bytes=VMEM_LIMIT),
        name="gdn",
    )(proj_big, proj_big, gcol, grow, norm_g)


FOX_X = 128
FOX_QC = (0, 6)
FOX_KC = (3, 9)


def _split3(c):
    hi = c.astype(BF16)
    r1 = c - hi.astype(F32)
    mid = r1.astype(BF16)
    lo = (r1 - mid.astype(F32)).astype(BF16)
    return jnp.concatenate([hi, mid, lo], axis=1)


def _select3(pair, base):
    r = lax.broadcasted_iota(jnp.int32, (3 * N_SMALL, FOX_X), 0)
    c = lax.broadcasted_iota(jnp.int32, (3 * N_SMALL, FOX_X), 1)
    d = jnp.right_shift(r, 7)
    head = jnp.bitwise_and(r, N_SMALL - 1) - (LANE_C + 2 * pair)
    e = jnp.where(head == 0, jnp.where(c == base[0] + d, 1.0, 0.0),
                  jnp.where(head == 1, jnp.where(c == base[1] + d, 1.0, 0.0), 0.0))
    return e.astype(BF16)


def _lane_ones(first):
    lane = lax.broadcasted_iota(jnp.int32, (1, FOX_X), 1)
    out = jnp.zeros((1, FOX_X), F32)
    for f in first:
        out = jnp.where(lane < f, out, jnp.where(lane < f + 3, 1.0, out))
    return out


def _fox_kernel(q_ref, k_ref, v_ref, gc_ref, ng_ref, o_ref, qa_ref, ka_ref, m_ref, l_ref, acc_ref):
    tq = q_ref.shape[0]
    seq = k_ref.shape[0]
    i = pl.program_id(1)
    pairs = range(FOX_HEADS // 2)
    lane = lax.broadcasted_iota(jnp.int32, (1, 2 * FOX_DH), 1)
    lo_half = lane < FOX_DH
    pcols = [slice(p * 2 * FOX_DH, (p + 1) * 2 * FOX_DH) for p in pairs]

    @pl.when(i == 0)
    def _():
        ones_k = _lane_ones(FOX_QC)
        sel_k = [_select3(p, FOX_KC) for p in pairs]

        def fill(t, carry):
            rows = pl.ds(pl.multiple_of(t * tq, tq), tq)
            pieces = _split3(gc_ref[rows, :])
            for p in pairs:
                ka_ref[p, rows, 0:2 * FOX_DH] = k_ref[rows, pcols[p]]
                ka_ref[p, rows, 2 * FOX_DH:] = (ones_k - _dot(pieces, sel_k[p])).astype(BF16)
            return carry

        lax.fori_loop(0, seq // tq, fill, 0)

    qrows = pl.ds(pl.multiple_of(i * tq, tq), tq)
    pieces_q = _split3(gc_ref[qrows, :])
    ones_q = _lane_ones(FOX_KC)
    xlane = lax.broadcasted_iota(jnp.int32, (1, FOX_X), 1)
    for p in pairs:
        qp = q_ref[:, pcols[p]] * (FOX_DH ** -0.5)
        qx = _dot(pieces_q, _select3(p, FOX_QC)) + ones_q
        for half in range(2):
            rows = slice(half * tq, (half + 1) * tq)
            mine = lo_half if half == 0 else jnp.logical_not(lo_half)
            xmine = (xlane < FOX_QC[1]) if half == 0 else (xlane >= FOX_QC[1])
            qa_ref[p, rows, 0:2 * FOX_DH] = jnp.where(mine, qp, jnp.zeros((), BF16))
            qa_ref[p, rows, 2 * FOX_DH:] = jnp.where(xmine, qx, 0.0).astype(BF16)
    m_ref[...] = jnp.full(m_ref.shape, -jnp.inf, F32)
    l_ref[...] = jnp.zeros(l_ref.shape, F32)
    acc_ref[...] = jnp.zeros(acc_ref.shape, F32)

    ri = jnp.bitwise_and(lax.broadcasted_iota(jnp.int32, (2 * tq, tq), 0), tq - 1)
    ci = lax.broadcasted_iota(jnp.int32, (2 * tq, tq), 1)
    causal = ri >= ci

    def tile(j, masked):
        krows = pl.ds(pl.multiple_of(j * tq, tq), tq)
        s = [_dot_nt(qa_ref[p], ka_ref[p, krows, :]) for p in pairs]
        if masked:
            s = [jnp.where(causal, a, -jnp.inf) for a in s]
        m_prev = [m_ref[p] for p in pairs]
        m_new = [jnp.maximum(m_prev[p], jnp.max(s[p], -1, keepdims=True)) for p in pairs]
        scale = [jnp.exp(m_prev[p] - m_new[p]) for p in pairs]
        prob = [jnp.exp(s[p] - jnp.concatenate([m_new[p]] * (tq // (2 * FOX_DH)), axis=1))
                for p in pairs]
        for p in pairs:
            m_ref[p] = m_new[p]
            l_ref[p] = scale[p] * l_ref[p] + jnp.sum(prob[p], -1, keepdims=True)
            acc_ref[p] = scale[p] * acc_ref[p] + _dot(prob[p].astype(BF16), v_ref[krows, pcols[p]])

    def body(j, carry):
        tile(j, False)
        return carry

    lax.fori_loop(0, i, body, 0)
    tile(i, True)

    ng = ng_ref[...]
    for p in pairs:
        o = acc_ref[p] / l_ref[p]
        outs = []
        for half in range(2):
            oh = o[half * tq:(half + 1) * tq]
            mine = lo_half if half == 0 else jnp.logical_not(lo_half)
            ms = jnp.sum(jnp.where(mine, oh * oh, 0.0), -1, keepdims=True) * (1.0 / FOX_DH)
            outs.append(oh * lax.rsqrt(ms + NORM_EPS) * ng)
        o_ref[:, pcols[p]] = jnp.where(lo_half, outs[0], outs[1]).astype(BF16)


def _fox(proj_big, gcol, norm_g2, batch, seq):
    nq = seq // FOX_TQ
    first = (GDN_QKV + GDN_WIDTH) // FOX_WIDTH
    npair = FOX_HEADS // 2
    return pl.pallas_call(
        _fox_kernel,
        grid=(batch, nq),
        in_specs=[
            pl.BlockSpec((FOX_TQ, FOX_WIDTH), lambda b, i: (b * nq + i, first)),
            pl.BlockSpec((seq, FOX_WIDTH), lambda b, i: (b, first + 1)),
            pl.BlockSpec((seq, FOX_WIDTH), lambda b, i: (b, first + 2)),
            pl.BlockSpec((seq, N_SMALL), lambda b, i: (b, 0)),
            _resident((1, 2 * FOX_DH)),
        ],
        out_specs=pl.BlockSpec((FOX_TQ, FOX_WIDTH), lambda b, i: (b * nq + i, 0)),
        out_shape=jax.ShapeDtypeStruct((batch * seq, FOX_WIDTH), BF16),
        scratch_shapes=[
            pltpu.VMEM((npair, 2 * FOX_TQ, 2 * FOX_DH + FOX_X), BF16),
            pltpu.VMEM((npair, seq, 2 * FOX_DH + FOX_X), BF16),
            pltpu.VMEM((npair, 2 * FOX_TQ, 2 * FOX_DH), F32),
            pltpu.VMEM((npair, 2 * FOX_TQ, 2 * FOX_DH), F32),
            pltpu.VMEM((npair, 2 * FOX_TQ, 2 * FOX_DH), F32),
        ],
        compiler_params=pltpu.CompilerParams(
            dimension_semantics=("arbitrary", "arbitrary"), vmem_limit_bytes=VMEM_LIMIT),
        name="fox",
    )(proj_big, proj_big, proj_big, gcol, norm_g2)


def _tail_kernel(x_ref, p_ref, og_ref, of_ref, lin_g, lin_b, wo_ref, l1g, l1b, wu_ref, wd_ref,
                 wp_ref, wg_ref, bg_ref, l2g, l2b, o_ref):
    half = x_ref.shape[0] // 2
    ra, rb = slice(0, half), slice(half, 2 * half)

    def head(rows):
        h = _layer_norm(x_ref[rows, :], lin_g[...], lin_b[...])
        mix = (_dot(og_ref[rows, :], wo_ref[0:GDN_WIDTH, :])
               + _dot(of_ref[rows, :], wo_ref[GDN_WIDTH:, :]))
        return _layer_norm(ALPHA * h + mix, l1g[...], l1b[...])

    def mlp(rows, h1):
        h1b = h1.astype(BF16)
        gate = _sigmoid(_dot(h1b, wg_ref[...]) + bg_ref[...])
        acc = ALPHA * h1 + _dot(p_ref[rows, :].astype(BF16), wp_ref[...]) * gate
        for j in range(D_FF // TAIL_TF):
            cols = slice(j * TAIL_TF, (j + 1) * TAIL_TF)
            a = jnp.maximum(_dot(h1b, wu_ref[:, cols]), 0.0)
            acc = acc + _dot((a * a).astype(BF16), wd_ref[cols, :])
        return acc

    h1a = head(ra)
    h1b_ = head(rb)
    acc_a = mlp(ra, h1a)
    o_ref[ra, :] = _layer_norm(acc_a, l2g[...], l2b[...])
    acc_b = mlp(rb, h1b_)
    o_ref[rb, :] = _layer_norm(acc_b, l2g[...], l2b[...])


def _tail(x2, p2, o_gdn, o_fox, lin_g, lin_b, w_out, l1g, l1b, w_up, w_down, w_ple, w_gate,
          b_gate, l2g, l2b):
    n = x2.shape[0]
    row = lambda width: pl.BlockSpec((TAIL_TM, width), lambda i: (i, 0))
    vec = _resident((1, D_MODEL))
    return pl.pallas_call(
        _tail_kernel,
        grid=(n // TAIL_TM,),
        in_specs=[
            row(D_MODEL), row(D_PLE), row(GDN_WIDTH), row(FOX_WIDTH),
            vec, vec, _resident((D_MODEL, D_MODEL)), vec, vec,
            _resident((D_MODEL, D_FF)), _resident((D_FF, D_MODEL)),
            _resident((D_PLE, D_MODEL)), _resident((D_MODEL, D_MODEL)), vec, vec, vec,
        ],
        out_specs=row(D_MODEL),
        out_shape=jax.ShapeDtypeStruct((n, D_MODEL), F32),
        compiler_params=pltpu.CompilerParams(
            dimension_semantics=("arbitrary",), vmem_limit_bytes=VMEM_LIMIT),
        name="tail",
    )(x2, p2, o_gdn, o_fox, lin_g, lin_b, w_out, l1g, l1b, w_up, w_down, w_ple, w_gate,
      b_gate, l2g, l2b)


def kernel(x, p, ln_in_g, ln_in_b, w_in, conv_w, a_log, dt_bias, gdn_norm_g, b_f, fox_norm_g,
           w_out, ln1_g, ln1_b, w_up, w_down, w_ple, w_ple_gate, b_ple_gate, ln2_g, ln2_b):
    batch, seq, _ = x.shape
    assert x.shape[2] == D_MODEL and w_in.shape[0] == 1
    assert seq % FOX_TQ == 0 and seq % GDN_TB == 0 and (batch * seq) % PROJ_TM == 0
    n = batch * seq
    x2 = x.reshape(n, D_MODEL)
    p2 = p[0].reshape(n, D_PLE)
    row = lambda a: a.reshape(1, -1).astype(F32)

    w0 = w_in[0]
    w_big = jnp.concatenate([w0[:, :OFF_BETA], w0[:, OFF_FOX:OFF_F]], axis=1).astype(BF16)
    w_decay = w0[:, OFF_BETA + GDN_HEADS:OFF_FOX]
    n_gate = LANE_GTOT + GDN_HEADS
    w_small = jnp.concatenate(
        [w0[:, OFF_BETA:OFF_FOX], w0[:, OFF_F:], w_decay, jnp.zeros((D_MODEL, N_SMALL - n_gate), F32)],
        axis=1).astype(BF16)
    zeros4 = jnp.zeros((GDN_HEADS,), F32)
    pad = jnp.zeros((N_SMALL - n_gate,), F32)
    pvec = jnp.zeros((8, N_SMALL), F32)
    pvec = pvec.at[0].set(jnp.concatenate([zeros4, dt_bias[0], b_f[0], dt_bias[0], pad]))
    pvec = pvec.at[1].set(
        jnp.concatenate([zeros4, a_log[0], jnp.zeros((FOX_HEADS,), F32), a_log[0], pad]))

    proj_big, small = _proj(x2, row(ln_in_g), row(ln_in_b), w_big, w_small, conv_w[0], seq)
    gcol = _gates(small, pvec, batch, seq)

    grow = jnp.transpose(
        gcol[:, LANE_GAM:LANE_C].reshape(n // GDN_TB, GDN_TB, GDN_HEADS), (0, 2, 1))

    o_gdn = _gdn(proj_big, gcol, grow, row(gdn_norm_g[0]), batch, seq)
    o_fox = _fox(proj_big, gcol, row(jnp.tile(fox_norm_g[0], 2)), batch, seq)

    out = _tail(x2, p2, o_gdn, o_fox, row(ln_in_g), row(ln_in_b), w_out[0].astype(BF16),
                row(ln1_g[0]), row(ln1_b[0]), w_up[0].astype(BF16), w_down[0].astype(BF16),
                w_ple[0].astype(BF16), w_ple_gate[0].astype(BF16), row(b_ple_gate[0]),
                row(ln2_g[0]), row(ln2_b[0]))
    return out.reshape(batch, seq, D_MODEL)
```

```python
import functools

import jax
import jax.numpy as jnp
from jax import lax
from jax.experimental import pallas as pl
from jax.experimental.pallas import tpu as pltpu

F32 = jnp.float32
BF16 = jnp.bfloat16
HIGHEST = lax.Precision.HIGHEST

D_MODEL = 1024
CHUNK = 64
GDN_HEADS = 4
GDN_DK = 128
GDN_DV = 128
GDN_QK = GDN_HEADS * GDN_DK
GDN_QKV = 3 * GDN_QK
GDN_WIDTH = GDN_HEADS * GDN_DV
FOX_HEADS = 8
FOX_DH = 64
FOX_WIDTH = FOX_HEADS * FOX_DH
CONV_W = 4
D_FF = 4 * D_MODEL
D_PLE = 256
LN_EPS = 1e-5
NORM_EPS = 1e-6
ALPHA = 2.0 ** 0.25

OFF_Z = GDN_QKV
OFF_BETA = OFF_Z + GDN_WIDTH
OFF_FOX = OFF_BETA + 2 * GDN_HEADS
OFF_F = OFF_FOX + 3 * FOX_WIDTH
N_BIG = GDN_QKV + GDN_WIDTH + 3 * FOX_WIDTH
N_SMALL = 128
LANE_BETA = 0
LANE_GAM = GDN_HEADS
LANE_C = 2 * GDN_HEADS
LANE_GTOT = LANE_C + FOX_HEADS

VMEM_LIMIT = 56 * 1024 * 1024

PROJ_TM = 512
PROJ_TN = 256
GATE_TB = 256
GDN_TB = 256
FOX_TQ = 256
TAIL_TM = 512
TAIL_TF = 1024


def _layer_norm(x, g, b):
    mu = jnp.mean(x, -1, keepdims=True)
    xc = x - mu
    var = jnp.mean(xc * xc, -1, keepdims=True)
    return xc * lax.rsqrt(var + LN_EPS) * g + b


def _softplus(x):
    return jnp.maximum(x, 0.0) + jnp.log(1.0 + jnp.exp(-jnp.abs(x)))


def _sigmoid(x):
    return 1.0 / (1.0 + jnp.exp(-x))


def _dot(a, b):
    return jnp.dot(a, b, preferred_element_type=F32)


def _dot_hi(a, b):
    return jnp.dot(a, b, precision=HIGHEST, preferred_element_type=F32)


def _dot_nt(a, b):
    return lax.dot_general(a, b, (((1,), (1,)), ((), ())), preferred_element_type=F32)


def _dot_tn(a, b):
    return lax.dot_general(a, b, (((0,), (0,)), ((), ())), preferred_element_type=F32)


def _resident(shape):
    return pl.BlockSpec(shape, lambda *_: (0,) * len(shape), pipeline_mode=pl.Buffered(1))


def _proj_kernel(x_ref, g_ref, b_ref, wb_ref, ws_ref, cw_ref, ob_ref, os_ref, halo_ref, *,
                 tiles_per_seq):
    tm = x_ref.shape[0]
    halo = halo_ref.shape[0]

    @pl.when(pl.program_id(0) % tiles_per_seq == 0)
    def _():
        halo_ref[...] = jnp.zeros_like(halo_ref)

    h = _layer_norm(x_ref[...], g_ref[...], b_ref[...]).astype(BF16)
    nchunk = N_BIG // PROJ_TN
    nconv = GDN_QKV // PROJ_TN
    order = [c for pair in zip(range(nconv), range(nconv, 2 * nconv)) for c in pair]
    order += list(range(2 * nconv, nchunk))
    pending = _dot(h, wb_ref[:, order[0] * PROJ_TN:(order[0] + 1) * PROJ_TN])
    for pos, j in enumerate(order):
        cols = slice(j * PROJ_TN, (j + 1) * PROJ_TN)
        acc = pending
        if pos + 1 < nchunk:
            nxt = order[pos + 1]
            pending = _dot(h, wb_ref[:, nxt * PROJ_TN:(nxt + 1) * PROJ_TN])
        if (j + 1) * PROJ_TN <= GDN_QKV:
            ext = jnp.concatenate([halo_ref[:, cols], acc], axis=0)
            y = acc * cw_ref[CONV_W - 1:CONV_W, cols]
            for d in range(1, CONV_W):
                y = y + pltpu.roll(ext, d, 0)[halo:] * cw_ref[CONV_W - 1 - d:CONV_W - d, cols]
            halo_ref[:, cols] = acc[tm - halo:]
            acc = y * _sigmoid(y)
        ob_ref[:, cols] = acc.astype(BF16)
    os_ref[...] = _dot(h, ws_ref[...])


def _proj(x2, ln_g, ln_b, w_big, w_small, conv_w, seq):
    n = x2.shape[0]
    assert GDN_QKV % PROJ_TN == 0 and seq % PROJ_TM == 0
    return pl.pallas_call(
        functools.partial(_proj_kernel, tiles_per_seq=seq // PROJ_TM),
        grid=(n // PROJ_TM,),
        in_specs=[
            pl.BlockSpec((PROJ_TM, D_MODEL), lambda i: (i, 0)),
            _resident((1, D_MODEL)),
            _resident((1, D_MODEL)),
            _resident((D_MODEL, N_BIG)),
            _resident((D_MODEL, N_SMALL)),
            _resident((CONV_W, GDN_QKV)),
        ],
        out_specs=[
            pl.BlockSpec((PROJ_TM, N_BIG), lambda i: (i, 0)),
            pl.BlockSpec((PROJ_TM, N_SMALL), lambda i: (i, 0)),
        ],
        out_shape=[
            jax.ShapeDtypeStruct((n, N_BIG), BF16),
            jax.ShapeDtypeStruct((n, N_SMALL), F32),
        ],
        scratch_shapes=[pltpu.VMEM((8, GDN_QKV), F32)],
        compiler_params=pltpu.CompilerParams(
            dimension_semantics=("arbitrary",), vmem_limit_bytes=VMEM_LIMIT),
        name="proj",
    )(x2, ln_g, ln_b, w_big, w_small, conv_w)


def _gates_kernel(s_ref, pv_ref, o_ref, *, seq):
    lane = lax.broadcasted_iota(jnp.int32, (1, N_SMALL), 1)
    bias = pv_ref[0:1, :]
    neg_a = -jnp.exp(pv_ref[1:2, :])
    r = lax.broadcasted_iota(jnp.int32, (GATE_TB, GATE_TB), 0)
    c = lax.broadcasted_iota(jnp.int32, (GATE_TB, GATE_TB), 1)
    tri = r >= c
    l_full = jnp.where(tri, 1.0, 0.0).astype(F32)
    same_chunk = jnp.right_shift(r, 6) == jnp.right_shift(c, 6)
    l_chunk = jnp.where(same_chunk, l_full, 0.0)
    l_total = jnp.where(same_chunk, 1.0, 0.0).astype(F32)
    carry = jnp.zeros((1, N_SMALL), F32)
    for t in range(seq // GATE_TB):
        rows = slice(t * GATE_TB, (t + 1) * GATE_TB)
        x = s_ref[rows, :] + bias
        beta = _sigmoid(x)
        log_g = neg_a * _softplus(x)
        log_f = -_softplus(-x)
        gam = _dot_hi(l_chunk, log_g)
        gtot = _dot_hi(l_total, log_g)
        cum = _dot_hi(l_full, log_f) + carry
        carry = cum[GATE_TB - 1:GATE_TB, :]
        o_ref[rows, :] = jnp.where(
            lane < LANE_GAM, beta,
            jnp.where(lane < LANE_C, gam, jnp.where(lane < LANE_GTOT, cum, gtot)))


def _gates(small, pvec, batch, seq):
    return pl.pallas_call(
        functools.partial(_gates_kernel, seq=seq),
        grid=(batch,),
        in_specs=[
            pl.BlockSpec((seq, N_SMALL), lambda b: (b, 0)),
            _resident((8, N_SMALL)),
        ],
        out_specs=pl.BlockSpec((seq, N_SMALL), lambda b: (b, 0)),
        out_shape=jax.ShapeDtypeStruct((batch * seq, N_SMALL), F32),
        compiler_params=pltpu.CompilerParams(
            dimension_semantics=("arbitrary",), vmem_limit_bytes=VMEM_LIMIT),
        name="gates",
    )(small, pvec)


def _gdn_kernel(x_ref, z_ref, gc_ref, gr_ref, ng_ref, o_ref, s_ref):
    tb = x_ref.shape[0]
    nch = tb // CHUNK
    heads = range(GDN_HEADS)

    @pl.when(pl.program_id(1) == 0)
    def _():
        s_ref[...] = jnp.zeros_like(s_ref)

    def act(c0):
        return x_ref[:, c0:c0 + GDN_DK].astype(F32)

    def l2norm(v):
        return v * lax.rsqrt(jnp.sum(v * v, -1, keepdims=True) + NORM_EPS)

    ri = lax.broadcasted_iota(jnp.int32, (tb, tb), 0)
    ci = lax.broadcasted_iota(jnp.int32, (tb, tb), 1)
    same = jnp.right_shift(ri, 6) == jnp.right_shift(ci, 6)
    keep = jnp.logical_and(same, ri >= ci)
    diag = ri == ci
    pr = lax.broadcasted_iota(jnp.int32, (CHUNK, tb), 0)
    pc = lax.broadcasted_iota(jnp.int32, (CHUNK, tb), 1)
    eye_packed = jnp.where(jnp.bitwise_and(pc, CHUNK - 1) == pr, 1.0, 0.0).astype(F32)
    lane_chunk = jnp.right_shift(lax.broadcasted_iota(jnp.int32, (1, tb), 1), 6)
    row_chunk = jnp.right_shift(lax.broadcasted_iota(jnp.int32, (tb, 1), 0), 6)

    def to_bd(packed):
        return jnp.where(same, jnp.concatenate([packed] * nch, axis=0), jnp.zeros((), packed.dtype))

    def to_packed(bd):
        out = bd[0:CHUNK]
        for c in range(1, nch):
            out = out + bd[c * CHUNK:(c + 1) * CHUNK]
        return out

    ng = ng_ref[...]
    q = [l2norm(act(h * GDN_DK)) * (GDN_DK ** -0.5) for h in heads]
    k = [l2norm(act(GDN_QK + h * GDN_DK)) for h in heads]
    v = [act(2 * GDN_QK + h * GDN_DV) for h in heads]
    beta = [gc_ref[:, LANE_BETA + h:LANE_BETA + h + 1] for h in heads]
    gam = [gc_ref[:, LANE_GAM + h:LANE_GAM + h + 1] for h in heads]
    gtot = [gc_ref[:, LANE_GTOT + h:LANE_GTOT + h + 1] for h in heads]
    grow = [gr_ref[0, h:h + 1, :] for h in heads]
    egam = [jnp.exp(g) for g in gam]
    kb = [a.astype(BF16) for a in k]
    decay = [jnp.exp(jnp.where(keep, gam[h] - grow[h], -jnp.inf)) for h in heads]
    gram = [_dot_nt(kb[h], kb[h]) for h in heads]
    qk = [(_dot_nt(q[h].astype(BF16), kb[h]) * decay[h]).astype(BF16) for h in heads]

    x_bd = [jnp.where(diag, 0.0, -(gram[h] * beta[h]) * decay[h]) for h in heads]
    x_p = [to_packed(a) for a in x_bd]
    p_p = [eye_packed + a for a in x_p]
    x_p = [_dot(x_p[h].astype(BF16), x_bd[h].astype(BF16)) for h in heads]
    for _ in range(4):
        w_bd = [to_bd(a.astype(BF16)) for a in x_p]
        r = [_dot(jnp.concatenate([p_p[h], x_p[h]], axis=0).astype(BF16), w_bd[h]) for h in heads]
        p_p = [p_p[h] + r[h][:CHUNK] for h in heads]
        x_p = [r[h][CHUNK:] for h in heads]
    p_p = [p_p[h] + _dot(p_p[h].astype(BF16), to_bd(x_p[h].astype(BF16))) for h in heads]

    rhs = [jnp.concatenate([v[h] * beta[h], k[h] * (beta[h] * egam[h])], axis=1).astype(BF16)
           for h in heads]
    sol = [_dot(to_bd(p_p[h].astype(BF16)), rhs[h]) for h in heads]
    u = [a[:, :GDN_DV] for a in sol]
    w = [a[:, GDN_DV:].astype(BF16) for a in sol]
    qd = [(q[h] * egam[h]).astype(BF16) for h in heads]
    kd_t = [jnp.transpose(k[h] * jnp.exp(gtot[h] - gam[h])).astype(BF16) for h in heads]
    z = [z_ref[:, h * GDN_DV:(h + 1) * GDN_DV].astype(F32) for h in heads]
    zgate = [a * _sigmoid(a) for a in z]

    state = [s_ref[h] for h in heads]
    for c in range(nch):
        rows = slice(c * CHUNK, (c + 1) * CHUNK)
        sb = [a.astype(BF16) for a in state]
        r1 = [_dot(jnp.concatenate([w[h][rows], qd[h][rows]], axis=0), sb[h]) for h in heads]
        v_new = [u[h][rows] - r1[h][:CHUNK] for h in heads]
        v_full = [jnp.where(row_chunk == c, jnp.concatenate([a.astype(BF16)] * nch, axis=0),
                            jnp.zeros((), BF16)) for a in v_new]
        kd_c = [jnp.where(lane_chunk == c, a, jnp.zeros((), BF16)) for a in kd_t]
        r2 = [_dot(jnp.concatenate([qk[h][rows], kd_c[h]], axis=0), v_full[h]) for h in heads]
        for h in heads:
            o = r1[h][CHUNK:] + r2[h][:CHUNK]
            on = o * lax.rsqrt(jnp.mean(o * o, -1, keepdims=True) + NORM_EPS) * ng
            o_ref[rows, h * GDN_DV:(h + 1) * GDN_DV] = (on * zgate[h][rows]).astype(BF16)
        state = [state[h] * jnp.exp(gtot[h][c * CHUNK:c * CHUNK + 1]) + r2[h][CHUNK:] for h in heads]
    for h in heads:
        s_ref[h] = state[h]


def _gdn(proj_big, gcol, grow, norm_g, batch, seq):
    nt = seq // GDN_TB
    return pl.pallas_call(
        _gdn_kernel,
        grid=(batch, nt),
        in_specs=[
            pl.BlockSpec((GDN_TB, GDN_QKV), lambda b, t: (b * nt + t, 0)),
            pl.BlockSpec((GDN_TB, GDN_WIDTH), lambda b, t: (b * nt + t, OFF_Z // GDN_WIDTH)),
            pl.BlockSpec((GDN_TB, N_SMALL), lambda b, t: (b * nt + t, 0)),
            pl.BlockSpec((1, GDN_HEADS, GDN_TB), lambda b, t: (b * nt + t, 0, 0)),
            _resident((1, GDN_DV)),
        ],
        out_specs=pl.BlockSpec((GDN_TB, GDN_WIDTH), lambda b, t: (b * nt + t, 0)),
        out_shape=jax.ShapeDtypeStruct((batch * seq, GDN_WIDTH), BF16),
        scratch_shapes=[
            pltpu.VMEM((GDN_HEADS, GDN_DK, GDN_DV), F32),
        ],
        compiler_params=pltpu.CompilerParams(
            dimension_semantics=("arbitrary", "arbitrary"), vmem_limit_bytes=VMEM_LIMIT),
        name="gdn",
    )(proj_big, proj_big, gcol, grow, norm_g)


FOX_X = 128
FOX_QC = (0, 6)
FOX_KC = (3, 9)


def _split3(c):
    hi = c.astype(BF16)
    r1 = c - hi.astype(F32)
    mid = r1.astype(BF16)
    lo = (r1 - mid.astype(F32)).astype(BF16)
    return jnp.concatenate([hi, mid, lo], axis=1)


def _select3(pair, base):
    r = lax.broadcasted_iota(jnp.int32, (3 * N_SMALL, FOX_X), 0)
    c = lax.broadcasted_iota(jnp.int32, (3 * N_SMALL, FOX_X), 1)
    d = jnp.right_shift(r, 7)
    head = jnp.bitwise_and(r, N_SMALL - 1) - (LANE_C + 2 * pair)
    e = jnp.where(head == 0, jnp.where(c == base[0] + d, 1.0, 0.0),
                  jnp.where(head == 1, jnp.where(c == base[1] + d, 1.0, 0.0), 0.0))
    return e.astype(BF16)


def _lane_ones(first):
    lane = lax.broadcasted_iota(jnp.int32, (1, FOX_X), 1)
    out = jnp.zeros((1, FOX_X), F32)
    for f in first:
        out = jnp.where(lane < f, out, jnp.where(lane < f + 3, 1.0, out))
    return out


def _fox_kernel(q_ref, k_ref, v_ref, gc_ref, ng_ref, o_ref, qa_ref, ka_ref, vt_ref, m_ref, l_ref,
                acc_ref):
    tq = q_ref.shape[0]
    seq = k_ref.shape[0]
    i = pl.program_id(1)
    pairs = range(FOX_HEADS // 2)
    lane = lax.broadcasted_iota(jnp.int32, (1, 2 * FOX_DH), 1)
    lo_half = lane < FOX_DH
    pcols = [slice(p * 2 * FOX_DH, (p + 1) * 2 * FOX_DH) for p in pairs]

    @pl.when(i == 0)
    def _():
        ones_k = _lane_ones(FOX_QC)
        sel_k = [_select3(p, FOX_KC) for p in pairs]

        def fill(t, carry):
            rows = pl.ds(pl.multiple_of(t * tq, tq), tq)
            pieces = _split3(gc_ref[rows, :])
            for p in pairs:
                ka_ref[p, rows, 0:2 * FOX_DH] = k_ref[rows, pcols[p]]
                ka_ref[p, rows, 2 * FOX_DH:] = (ones_k - _dot(pieces, sel_k[p])).astype(BF16)
                vt_ref[p, t] = jnp.transpose(v_ref[rows, pcols[p]].astype(F32)).astype(BF16)
            return carry

        lax.fori_loop(0, seq // tq, fill, 0)

    qrows = pl.ds(pl.multiple_of(i * tq, tq), tq)
    pieces_q = _split3(gc_ref[qrows, :])
    ones_q = _lane_ones(FOX_KC)
    xlane = lax.broadcasted_iota(jnp.int32, (1, FOX_X), 1)
    for p in pairs:
        qp = q_ref[:, pcols[p]] * (FOX_DH ** -0.5)
        qx = _dot(pieces_q, _select3(p, FOX_QC)) + ones_q
        for half in range(2):
            rows = slice(half * tq, (half + 1) * tq)
            mine = lo_half if half == 0 else jnp.logical_not(lo_half)
            xmine = (xlane < FOX_QC[1]) if half == 0 else (xlane >= FOX_QC[1])
            qa_ref[p, rows, 0:2 * FOX_DH] = jnp.where(mine, qp, jnp.zeros((), BF16))
            qa_ref[p, rows, 2 * FOX_DH:] = jnp.where(xmine, qx, 0.0).astype(BF16)

    ki = lax.broadcasted_iota(jnp.int32, (tq, 2 * tq), 0)
    qi = jnp.bitwise_and(lax.broadcasted_iota(jnp.int32, (tq, 2 * tq), 1), tq - 1)
    causal = ki <= qi

    def tile(blocks, first):
        krows = [pl.ds(pl.multiple_of(j * tq, tq), tq) for j in blocks]
        s = [[_dot_nt(ka_ref[p, r, :], qa_ref[p]) for r in krows] for p in pairs]
        for p in pairs:
            sp = s[p]
            if first:
                sp = [jnp.where(causal, a, -jnp.inf) for a in sp]
            m_new = functools.reduce(jnp.maximum, [jnp.max(a, 0, keepdims=True) for a in sp])
            if not first:
                m_prev = m_ref[p]
                m_new = jnp.maximum(m_prev, m_new)
                scale = jnp.exp(m_prev - m_new)
            prob = [jnp.exp(a - m_new) for a in sp]
            psum = sum(jnp.sum(a, 0, keepdims=True) for a in prob)
            pv = sum(_dot(vt_ref[p, j], a.astype(BF16)) for j, a in zip(blocks, prob))
            m_ref[p] = m_new
            if first:
                l_ref[p] = psum
                acc_ref[p] = pv
            else:
                l_ref[p] = scale * l_ref[p] + psum
                acc_ref[p] = scale * acc_ref[p] + pv

    tile([i], True)

    def body(t, carry):
        tile([2 * t, 2 * t + 1], False)
        return carry

    lax.fori_loop(0, i // 2, body, 0)

    @pl.when(i % 2 == 1)
    def _():
        tile([i - 1], False)

    ng = ng_ref[...]
    for p in pairs:
        acc = acc_ref[p]
        l = l_ref[p]
        outs = []
        for half in range(2):
            oh = (acc[half * FOX_DH:(half + 1) * FOX_DH, half * tq:(half + 1) * tq]
                  / l[:, half * tq:(half + 1) * tq])
            ms = jnp.mean(oh * oh, 0, keepdims=True)
            outs.append(oh * lax.rsqrt(ms + NORM_EPS))
        o_ref[:, pcols[p]] = (jnp.transpose(jnp.concatenate(outs, axis=0)) * ng).astype(BF16)


def _fox(proj_big, gcol, norm_g2, batch, seq):
    nq = seq // FOX_TQ
    first = (GDN_QKV + GDN_WIDTH) // FOX_WIDTH
    npair = FOX_HEADS // 2
    return pl.pallas_call(
        _fox_kernel,
        grid=(batch, nq),
        in_specs=[
            pl.BlockSpec((FOX_TQ, FOX_WIDTH), lambda b, i: (b * nq + i, first)),
            pl.BlockSpec((seq, FOX_WIDTH), lambda b, i: (b, first + 1)),
            pl.BlockSpec((seq, FOX_WIDTH), lambda b, i: (b, first + 2)),
            pl.BlockSpec((seq, N_SMALL), lambda b, i: (b, 0)),
            _resident((1, 2 * FOX_DH)),
        ],
        out_specs=pl.BlockSpec((FOX_TQ, FOX_WIDTH), lambda b, i: (b * nq + i, 0)),
        out_shape=jax.ShapeDtypeStruct((batch * seq, FOX_WIDTH), BF16),
        scratch_shapes=[
            pltpu.VMEM((npair, 2 * FOX_TQ, 2 * FOX_DH + FOX_X), BF16),
            pltpu.VMEM((npair, seq, 2 * FOX_DH + FOX_X), BF16),
            pltpu.VMEM((npair, nq, 2 * FOX_DH, FOX_TQ), BF16),
            pltpu.VMEM((npair, 1, 2 * FOX_TQ), F32),
            pltpu.VMEM((npair, 1, 2 * FOX_TQ), F32),
            pltpu.VMEM((npair, 2 * FOX_DH, 2 * FOX_TQ), F32),
        ],
        compiler_params=pltpu.CompilerParams(
            dimension_semantics=("arbitrary", "arbitrary"), vmem_limit_bytes=VMEM_LIMIT),
        name="fox",
    )(proj_big, proj_big, proj_big, gcol, norm_g2)


def _tail_kernel(x_ref, p_ref, og_ref, of_ref, lin_g, lin_b, wo_ref, l1g, l1b, wu_ref, wd_ref,
                 wp_ref, wg_ref, bg_ref, l2g, l2b, o_ref):
    half = x_ref.shape[0] // 2
    ra, rb = slice(0, half), slice(half, 2 * half)

    def head(rows):
        h = _layer_norm(x_ref[rows, :], lin_g[...], lin_b[...])
        mix = (_dot(og_ref[rows, :], wo_ref[0:GDN_WIDTH, :])
               + _dot(of_ref[rows, :], wo_ref[GDN_WIDTH:, :]))
        return _layer_norm(ALPHA * h + mix, l1g[...], l1b[...])

    def mlp(rows, h1):
        h1b = h1.astype(BF16)
        gate = _sigmoid(_dot(h1b, wg_ref[...]) + bg_ref[...])
        acc = ALPHA * h1 + _dot(p_ref[rows, :].astype(BF16), wp_ref[...]) * gate
        for j in range(D_FF // TAIL_TF):
            cols = slice(j * TAIL_TF, (j + 1) * TAIL_TF)
            a = jnp.maximum(_dot(h1b, wu_ref[:, cols]), 0.0)
            acc = acc + _dot((a * a).astype(BF16), wd_ref[cols, :])
        return acc

    h1a = head(ra)
    h1b_ = head(rb)
    acc_a = mlp(ra, h1a)
    o_ref[ra, :] = _layer_norm(acc_a, l2g[...], l2b[...])
    acc_b = mlp(rb, h1b_)
    o_ref[rb, :] = _layer_norm(acc_b, l2g[...], l2b[...])


def _tail(x2, p2, o_gdn, o_fox, lin_g, lin_b, w_out, l1g, l1b, w_up, w_down, w_ple, w_gate,
          b_gate, l2g, l2b):
    n = x2.shape[0]
    row = lambda width: pl.BlockSpec((TAIL_TM, width), lambda i: (i, 0))
    vec = _resident((1, D_MODEL))
    return pl.pallas_call(
        _tail_kernel,
        grid=(n // TAIL_TM,),
        in_specs=[
            row(D_MODEL), row(D_PLE), row(GDN_WIDTH), row(FOX_WIDTH),
            vec, vec, _resident((D_MODEL, D_MODEL)), vec, vec,
            _resident((D_MODEL, D_FF)), _resident((D_FF, D_MODEL)),
            _resident((D_PLE, D_MODEL)), _resident((D_MODEL, D_MODEL)), vec, vec, vec,
        ],
        out_specs=row(D_MODEL),
        out_shape=jax.ShapeDtypeStruct((n, D_MODEL), F32),
        compiler_params=pltpu.CompilerParams(
            dimension_semantics=("arbitrary",), vmem_limit_bytes=VMEM_LIMIT),
        name="tail",
    )(x2, p2, o_gdn, o_fox, lin_g, lin_b, w_out, l1g, l1b, w_up, w_down, w_ple, w_gate,
      b_gate, l2g, l2b)


def kernel(x, p, ln_in_g, ln_in_b, w_in, conv_w, a_log, dt_bias, gdn_norm_g, b_f, fox_norm_g,
           w_out, ln1_g, ln1_b, w_up, w_down, w_ple, w_ple_gate, b_ple_gate, ln2_g, ln2_b):
    batch, seq, _ = x.shape
    assert x.shape[2] == D_MODEL and w_in.shape[0] == 1
    assert seq % FOX_TQ == 0 and seq % GDN_TB == 0 and (batch * seq) % PROJ_TM == 0
    n = batch * seq
    x2 = x.reshape(n, D_MODEL)
    p2 = p[0].reshape(n, D_PLE)
    row = lambda a: a.reshape(1, -1).astype(F32)

    w0 = w_in[0]
    w_big = jnp.concatenate([w0[:, :OFF_BETA], w0[:, OFF_FOX:OFF_F]], axis=1).astype(BF16)
    w_decay = w0[:, OFF_BETA + GDN_HEADS:OFF_FOX]
    n_gate = LANE_GTOT + GDN_HEADS
    w_small = jnp.concatenate(
        [w0[:, OFF_BETA:OFF_FOX], w0[:, OFF_F:], w_decay, jnp.zeros((D_MODEL, N_SMALL - n_gate), F32)],
        axis=1).astype(BF16)
    zeros4 = jnp.zeros((GDN_HEADS,), F32)
    pad = jnp.zeros((N_SMALL - n_gate,), F32)
    pvec = jnp.zeros((8, N_SMALL), F32)
    pvec = pvec.at[0].set(jnp.concatenate([zeros4, dt_bias[0], b_f[0], dt_bias[0], pad]))
    pvec = pvec.at[1].set(
        jnp.concatenate([zeros4, a_log[0], jnp.zeros((FOX_HEADS,), F32), a_log[0], pad]))

    proj_big, small = _proj(x2, row(ln_in_g), row(ln_in_b), w_big, w_small, conv_w[0], seq)
    gcol = _gates(small, pvec, batch, seq)

    grow = jnp.transpose(
        gcol[:, LANE_GAM:LANE_C].reshape(n // GDN_TB, GDN_TB, GDN_HEADS), (0, 2, 1))

    o_gdn = _gdn(proj_big, gcol, grow, row(gdn_norm_g[0]), batch, seq)
    o_fox = _fox(proj_big, gcol, row(jnp.tile(fox_norm_g[0], 2)), batch, seq)

    out = _tail(x2, p2, o_gdn, o_fox, row(ln_in_g), row(ln_in_b), w_out[0].astype(BF16),
                row(ln1_g[0]), row(ln1_b[0]), w_up[0].astype(BF16), w_down[0].astype(BF16),
                w_ple[0].astype(BF16), w_ple_gate[0].astype(BF16), row(b_ple_gate[0]),
                row(ln2_g[0]), row(ln2_b[0]))
    return out.reshape(batch, seq, D_MODEL)
```

```python
import functools

import jax
import jax.numpy as jnp
from jax import lax
from jax.experimental import pallas as pl
from jax.experimental.pallas import tpu as pltpu

F32 = jnp.float32
BF16 = jnp.bfloat16
HIGHEST = lax.Precision.HIGHEST

D_MODEL = 1024
CHUNK = 64
GDN_HEADS = 4
GDN_DK = 128
GDN_DV = 128
GDN_QK = GDN_HEADS * GDN_DK
GDN_QKV = 3 * GDN_QK
GDN_WIDTH = GDN_HEADS * GDN_DV
FOX_HEADS = 8
FOX_DH = 64
FOX_WIDTH = FOX_HEADS * FOX_DH
CONV_W = 4
D_FF = 4 * D_MODEL
D_PLE = 256
LN_EPS = 1e-5
NORM_EPS = 1e-6
ALPHA = 2.0 ** 0.25

OFF_Z = GDN_QKV
OFF_BETA = OFF_Z + GDN_WIDTH
OFF_FOX = OFF_BETA + 2 * GDN_HEADS
OFF_F = OFF_FOX + 3 * FOX_WIDTH
N_BIG = GDN_QKV + GDN_WIDTH + 3 * FOX_WIDTH
N_SMALL = 128
LANE_BETA = 0
LANE_GAM = GDN_HEADS
LANE_C = 2 * GDN_HEADS
LANE_GTOT = LANE_C + FOX_HEADS

VMEM_LIMIT = 56 * 1024 * 1024

PROJ_TM = 512
PROJ_TN = 256
GATE_TB = 256
GDN_TB = 256
GDN_NB = 4
FOX_TQ = 256
TAIL_TM = 512
TAIL_TF = 1024


def _layer_norm(x, g, b):
    mu = jnp.mean(x, -1, keepdims=True)
    xc = x - mu
    var = jnp.mean(xc * xc, -1, keepdims=True)
    return xc * lax.rsqrt(var + LN_EPS) * g + b


def _softplus(x):
    return jnp.maximum(x, 0.0) + jnp.log(1.0 + jnp.exp(-jnp.abs(x)))


def _sigmoid(x):
    return 1.0 / (1.0 + jnp.exp(-x))


def _dot(a, b):
    return jnp.dot(a, b, preferred_element_type=F32)


def _dot_hi(a, b):
    return jnp.dot(a, b, precision=HIGHEST, preferred_element_type=F32)


def _dot_nt(a, b):
    return lax.dot_general(a, b, (((1,), (1,)), ((), ())), preferred_element_type=F32)


def _dot_tn(a, b):
    return lax.dot_general(a, b, (((0,), (0,)), ((), ())), preferred_element_type=F32)


def _resident(shape):
    return pl.BlockSpec(shape, lambda *_: (0,) * len(shape), pipeline_mode=pl.Buffered(1))


def _proj_kernel(x_ref, g_ref, b_ref, wb_ref, ws_ref, cw_ref, ob_ref, os_ref, halo_ref, *,
                 tiles_per_seq):
    tm = x_ref.shape[0]
    halo = halo_ref.shape[0]

    @pl.when(pl.program_id(0) % tiles_per_seq == 0)
    def _():
        halo_ref[...] = jnp.zeros_like(halo_ref)

    h = _layer_norm(x_ref[...], g_ref[...], b_ref[...]).astype(BF16)
    nchunk = N_BIG // PROJ_TN
    nconv = GDN_QKV // PROJ_TN
    order = [c for pair in zip(range(nconv), range(nconv, 2 * nconv)) for c in pair]
    order += list(range(2 * nconv, nchunk))
    pending = _dot(h, wb_ref[:, order[0] * PROJ_TN:(order[0] + 1) * PROJ_TN])
    for pos, j in enumerate(order):
        cols = slice(j * PROJ_TN, (j + 1) * PROJ_TN)
        acc = pending
        if pos + 1 < nchunk:
            nxt = order[pos + 1]
            pending = _dot(h, wb_ref[:, nxt * PROJ_TN:(nxt + 1) * PROJ_TN])
        if (j + 1) * PROJ_TN <= GDN_QKV:
            ext = jnp.concatenate([halo_ref[:, cols], acc], axis=0)
            y = acc * cw_ref[CONV_W - 1:CONV_W, cols]
            for d in range(1, CONV_W):
                y = y + pltpu.roll(ext, d, 0)[halo:] * cw_ref[CONV_W - 1 - d:CONV_W - d, cols]
            halo_ref[:, cols] = acc[tm - halo:]
            acc = y * _sigmoid(y)
        ob_ref[:, cols] = acc.astype(BF16)
    os_ref[...] = _dot(h, ws_ref[...])


def _proj(x2, ln_g, ln_b, w_big, w_small, conv_w, seq):
    n = x2.shape[0]
    assert GDN_QKV % PROJ_TN == 0 and seq % PROJ_TM == 0
    return pl.pallas_call(
        functools.partial(_proj_kernel, tiles_per_seq=seq // PROJ_TM),
        grid=(n // PROJ_TM,),
        in_specs=[
            pl.BlockSpec((PROJ_TM, D_MODEL), lambda i: (i, 0)),
            _resident((1, D_MODEL)),
            _resident((1, D_MODEL)),
            _resident((D_MODEL, N_BIG)),
            _resident((D_MODEL, N_SMALL)),
            _resident((CONV_W, GDN_QKV)),
        ],
        out_specs=[
            pl.BlockSpec((PROJ_TM, N_BIG), lambda i: (i, 0)),
            pl.BlockSpec((PROJ_TM, N_SMALL), lambda i: (i, 0)),
        ],
        out_shape=[
            jax.ShapeDtypeStruct((n, N_BIG), BF16),
            jax.ShapeDtypeStruct((n, N_SMALL), F32),
        ],
        scratch_shapes=[pltpu.VMEM((8, GDN_QKV), F32)],
        compiler_params=pltpu.CompilerParams(
            dimension_semantics=("arbitrary",), vmem_limit_bytes=VMEM_LIMIT),
        name="proj",
    )(x2, ln_g, ln_b, w_big, w_small, conv_w)


def _gates_kernel(s_ref, pv_ref, o_ref, *, seq):
    lane = lax.broadcasted_iota(jnp.int32, (1, N_SMALL), 1)
    bias = pv_ref[0:1, :]
    neg_a = -jnp.exp(pv_ref[1:2, :])
    r = lax.broadcasted_iota(jnp.int32, (GATE_TB, GATE_TB), 0)
    c = lax.broadcasted_iota(jnp.int32, (GATE_TB, GATE_TB), 1)
    tri = r >= c
    l_full = jnp.where(tri, 1.0, 0.0).astype(F32)
    same_chunk = jnp.right_shift(r, 6) == jnp.right_shift(c, 6)
    l_chunk = jnp.where(same_chunk, l_full, 0.0)
    l_total = jnp.where(same_chunk, 1.0, 0.0).astype(F32)
    carry = jnp.zeros((1, N_SMALL), F32)
    for t in range(seq // GATE_TB):
        rows = slice(t * GATE_TB, (t + 1) * GATE_TB)
        x = s_ref[rows, :] + bias
        beta = _sigmoid(x)
        log_g = neg_a * _softplus(x)
        log_f = -_softplus(-x)
        gam = _dot_hi(l_chunk, log_g)
        gtot = _dot_hi(l_total, log_g)
        cum = _dot_hi(l_full, log_f) + carry
        carry = cum[GATE_TB - 1:GATE_TB, :]
        o_ref[rows, :] = jnp.where(
            lane < LANE_GAM, beta,
            jnp.where(lane < LANE_C, gam, jnp.where(lane < LANE_GTOT, cum, gtot)))


def _gates(small, pvec, batch, seq):
    return pl.pallas_call(
        functools.partial(_gates_kernel, seq=seq),
        grid=(batch,),
        in_specs=[
            pl.BlockSpec((seq, N_SMALL), lambda b: (b, 0)),
            _resident((8, N_SMALL)),
        ],
        out_specs=pl.BlockSpec((seq, N_SMALL), lambda b: (b, 0)),
        out_shape=jax.ShapeDtypeStruct((batch * seq, N_SMALL), F32),
        compiler_params=pltpu.CompilerParams(
            dimension_semantics=("arbitrary",), vmem_limit_bytes=VMEM_LIMIT),
        name="gates",
    )(small, pvec)


def _gdn_kernel(x_ref, z_ref, gc_ref, gr_ref, ng_ref, o_ref, s_ref):
    nb, tb = x_ref.shape[0], x_ref.shape[1]
    nch = tb // CHUNK
    heads = range(nb * GDN_HEADS)
    nh = GDN_HEADS

    @pl.when(pl.program_id(1) == 0)
    def _():
        s_ref[...] = jnp.zeros_like(s_ref)

    def act(h, c0):
        c0 = c0 + (h % nh) * GDN_DK
        return x_ref[h // nh, :, c0:c0 + GDN_DK].astype(F32)

    def l2norm(v):
        return v * lax.rsqrt(jnp.sum(v * v, -1, keepdims=True) + NORM_EPS)

    ri = lax.broadcasted_iota(jnp.int32, (tb, tb), 0)
    ci = lax.broadcasted_iota(jnp.int32, (tb, tb), 1)
    same = jnp.right_shift(ri, 6) == jnp.right_shift(ci, 6)
    keep = jnp.logical_and(same, ri >= ci)
    diag = ri == ci
    pr = lax.broadcasted_iota(jnp.int32, (CHUNK, tb), 0)
    pc = lax.broadcasted_iota(jnp.int32, (CHUNK, tb), 1)
    eye_packed = jnp.where(jnp.bitwise_and(pc, CHUNK - 1) == pr, 1.0, 0.0).astype(F32)
    lane_chunk = jnp.right_shift(lax.broadcasted_iota(jnp.int32, (1, tb), 1), 6)
    row_chunk = jnp.right_shift(lax.broadcasted_iota(jnp.int32, (tb, 1), 0), 6)

    def to_bd(packed):
        return jnp.where(same, jnp.concatenate([packed] * nch, axis=0), jnp.zeros((), packed.dtype))

    def to_packed(bd):
        out = bd[0:CHUNK]
        for c in range(1, nch):
            out = out + bd[c * CHUNK:(c + 1) * CHUNK]
        return out

    ng = ng_ref[...]
    q = [l2norm(act(h, 0)) * (GDN_DK ** -0.5) for h in heads]
    k = [l2norm(act(h, GDN_QK)) for h in heads]
    v = [act(h, 2 * GDN_QK) for h in heads]

    def gate(h, lane0):
        return gc_ref[h // nh, :, lane0 + h % nh:lane0 + h % nh + 1]

    beta = [gate(h, LANE_BETA) for h in heads]
    gam = [gate(h, LANE_GAM) for h in heads]
    gtot = [gate(h, LANE_GTOT) for h in heads]
    grow = [gr_ref[h // nh, 0, h % nh:h % nh + 1, :] for h in heads]
    egam = [jnp.exp(g) for g in gam]
    kb = [a.astype(BF16) for a in k]
    decay = [jnp.exp(jnp.where(keep, gam[h] - grow[h], -jnp.inf)) for h in heads]
    gram = [_dot_nt(kb[h], kb[h]) for h in heads]
    qk = [(_dot_nt(q[h].astype(BF16), kb[h]) * decay[h]).astype(BF16) for h in heads]

    x_bd = [jnp.where(diag, 0.0, -(gram[h] * beta[h]) * decay[h]) for h in heads]
    x_p = [to_packed(a) for a in x_bd]
    p_p = [eye_packed + a for a in x_p]
    x_p = [_dot(x_p[h].astype(BF16), x_bd[h].astype(BF16)) for h in heads]
    for _ in range(4):
        w_bd = [to_bd(a.astype(BF16)) for a in x_p]
        r = [_dot(jnp.concatenate([p_p[h], x_p[h]], axis=0).astype(BF16), w_bd[h]) for h in heads]
        p_p = [p_p[h] + r[h][:CHUNK] for h in heads]
        x_p = [r[h][CHUNK:] for h in heads]
    p_p = [p_p[h] + _dot(p_p[h].astype(BF16), to_bd(x_p[h].astype(BF16))) for h in heads]

    rhs = [jnp.concatenate([v[h] * beta[h], k[h] * (beta[h] * egam[h])], axis=1).astype(BF16)
           for h in heads]
    sol = [_dot(to_bd(p_p[h].astype(BF16)), rhs[h]) for h in heads]
    u = [a[:, :GDN_DV] for a in sol]
    w = [a[:, GDN_DV:].astype(BF16) for a in sol]
    qd = [(q[h] * egam[h]).astype(BF16) for h in heads]
    kd_t = [jnp.transpose(k[h] * jnp.exp(gtot[h] - gam[h])).astype(BF16) for h in heads]
    z = [z_ref[h // nh, :, (h % nh) * GDN_DV:(h % nh + 1) * GDN_DV].astype(F32) for h in heads]
    zgate = [a * _sigmoid(a) for a in z]

    state = [s_ref[h] for h in heads]
    for c in range(nch):
        rows = slice(c * CHUNK, (c + 1) * CHUNK)
        sb = [a.astype(BF16) for a in state]
        r1 = [_dot(jnp.concatenate([w[h][rows], qd[h][rows]], axis=0), sb[h]) for h in heads]
        v_new = [u[h][rows] - r1[h][:CHUNK] for h in heads]
        v_full = [jnp.where(row_chunk == c, jnp.concatenate([a.astype(BF16)] * nch, axis=0),
                            jnp.zeros((), BF16)) for a in v_new]
        kd_c = [jnp.where(lane_chunk == c, a, jnp.zeros((), BF16)) for a in kd_t]
        r2 = [_dot(jnp.concatenate([qk[h][rows], kd_c[h]], axis=0), v_full[h]) for h in heads]
        for h in heads:
            o = r1[h][CHUNK:] + r2[h][:CHUNK]
            on = o * lax.rsqrt(jnp.mean(o * o, -1, keepdims=True) + NORM_EPS) * ng
            o_ref[h // nh, rows, (h % nh) * GDN_DV:(h % nh + 1) * GDN_DV] = (
                on * zgate[h][rows]).astype(BF16)
        state = [state[h] * jnp.exp(gtot[h][c * CHUNK:c * CHUNK + 1]) + r2[h][CHUNK:] for h in heads]
    for h in heads:
        s_ref[h] = state[h]


def _gdn(proj_big, gcol, grow, norm_g, batch, seq):
    nt = seq // GDN_TB
    nb = GDN_NB
    assert batch % nb == 0
    return pl.pallas_call(
        _gdn_kernel,
        grid=(batch // nb, nt),
        in_specs=[
            pl.BlockSpec((nb, GDN_TB, GDN_QKV), lambda g, t: (g, t, 0)),
            pl.BlockSpec((nb, GDN_TB, GDN_WIDTH), lambda g, t: (g, t, OFF_Z // GDN_WIDTH)),
            pl.BlockSpec((nb, GDN_TB, N_SMALL), lambda g, t: (g, t, 0)),
            pl.BlockSpec((nb, 1, GDN_HEADS, GDN_TB), lambda g, t: (g, t, 0, 0)),
            _resident((1, GDN_DV)),
        ],
        out_specs=pl.BlockSpec((nb, GDN_TB, GDN_WIDTH), lambda g, t: (g, t, 0)),
        out_shape=jax.ShapeDtypeStruct((batch, seq, GDN_WIDTH), BF16),
        scratch_shapes=[
            pltpu.VMEM((nb * GDN_HEADS, GDN_DK, GDN_DV), F32),
        ],
        compiler_params=pltpu.CompilerParams(
            dimension_semantics=("arbitrary", "arbitrary"), vmem_limit_bytes=VMEM_LIMIT),
        name="gdn",
    )(proj_big, proj_big, gcol, grow, norm_g)


FOX_X = 128
FOX_QC = (0, 6)
FOX_KC = (3, 9)


def _split3(c):
    hi = c.astype(BF16)
    r1 = c - hi.astype(F32)
    mid = r1.astype(BF16)
    lo = (r1 - mid.astype(F32)).astype(BF16)
    return jnp.concatenate([hi, mid, lo], axis=1)


def _select3(pair, base):
    r = lax.broadcasted_iota(jnp.int32, (3 * N_SMALL, FOX_X), 0)
    c = lax.broadcasted_iota(jnp.int32, (3 * N_SMALL, FOX_X), 1)
    d = jnp.right_shift(r, 7)
    head = jnp.bitwise_and(r, N_SMALL - 1) - (LANE_C + 2 * pair)
    e = jnp.where(head == 0, jnp.where(c == base[0] + d, 1.0, 0.0),
                  jnp.where(head == 1, jnp.where(c == base[1] + d, 1.0, 0.0), 0.0))
    return e.astype(BF16)


def _lane_ones(first):
    lane = lax.broadcasted_iota(jnp.int32, (1, FOX_X), 1)
    out = jnp.zeros((1, FOX_X), F32)
    for f in first:
        out = jnp.where(lane < f, out, jnp.where(lane < f + 3, 1.0, out))
    return out


def _fox_kernel(q_ref, k_ref, v_ref, gc_ref, ng_ref, o_ref, qa_ref, ka_ref, vt_ref, m_ref, l_ref,
                acc_ref):
    tq = q_ref.shape[0]
    seq = k_ref.shape[0]
    i = pl.program_id(1)
    pairs = range(FOX_HEADS // 2)
    lane = lax.broadcasted_iota(jnp.int32, (1, 2 * FOX_DH), 1)
    lo_half = lane < FOX_DH
    pcols = [slice(p * 2 * FOX_DH, (p + 1) * 2 * FOX_DH) for p in pairs]

    @pl.when(i == 0)
    def _():
        ones_k = _lane_ones(FOX_QC)
        sel_k = [_select3(p, FOX_KC) for p in pairs]

        def fill(t, carry):
            rows = pl.ds(pl.multiple_of(t * tq, tq), tq)
            pieces = _split3(gc_ref[rows, :])
            for p in pairs:
                ka_ref[p, rows, 0:2 * FOX_DH] = k_ref[rows, pcols[p]]
                ka_ref[p, rows, 2 * FOX_DH:] = (ones_k - _dot(pieces, sel_k[p])).astype(BF16)
                vt_ref[p, t] = jnp.transpose(v_ref[rows, pcols[p]].astype(F32)).astype(BF16)
            return carry

        lax.fori_loop(0, seq // tq, fill, 0)

    qrows = pl.ds(pl.multiple_of(i * tq, tq), tq)
    pieces_q = _split3(gc_ref[qrows, :])
    ones_q = _lane_ones(FOX_KC)
    xlane = lax.broadcasted_iota(jnp.int32, (1, FOX_X), 1)
    for p in pairs:
        qp = q_ref[:, pcols[p]] * (FOX_DH ** -0.5)
        qx = _dot(pieces_q, _select3(p, FOX_QC)) + ones_q
        for half in range(2):
            rows = slice(half * tq, (half + 1) * tq)
            mine = lo_half if half == 0 else jnp.logical_not(lo_half)
            xmine = (xlane < FOX_QC[1]) if half == 0 else (xlane >= FOX_QC[1])
            qa_ref[p, rows, 0:2 * FOX_DH] = jnp.where(mine, qp, jnp.zeros((), BF16))
            qa_ref[p, rows, 2 * FOX_DH:] = jnp.where(xmine, qx, 0.0).astype(BF16)

    ki = lax.broadcasted_iota(jnp.int32, (tq, 2 * tq), 0)
    qi = jnp.bitwise_and(lax.broadcasted_iota(jnp.int32, (tq, 2 * tq), 1), tq - 1)
    causal = ki <= qi

    def tile(blocks, first):
        krows = [pl.ds(pl.multiple_of(j * tq, tq), tq) for j in blocks]
        s = [[_dot_nt(ka_ref[p, r, :], qa_ref[p]) for r in krows] for p in pairs]
        for p in pairs:
            sp = s[p]
            if first:
                sp = [jnp.where(causal, a, -jnp.inf) for a in sp]
            m_new = functools.reduce(jnp.maximum, [jnp.max(a, 0, keepdims=True) for a in sp])
            if not first:
                m_prev = m_ref[p]
                m_new = jnp.maximum(m_prev, m_new)
                scale = jnp.exp(m_prev - m_new)
            prob = [jnp.exp(a - m_new) for a in sp]
            psum = sum(jnp.sum(a, 0, keepdims=True) for a in prob)
            pv = sum(_dot(vt_ref[p, j], a.astype(BF16)) for j, a in zip(blocks, prob))
            m_ref[p] = m_new
            if first:
                l_ref[p] = psum
                acc_ref[p] = pv
            else:
                l_ref[p] = scale * l_ref[p] + psum
                acc_ref[p] = scale * acc_ref[p] + pv

    tile([i], True)

    def body(t, carry):
        tile([2 * t, 2 * t + 1], False)
        return carry

    lax.fori_loop(0, i // 2, body, 0)

    @pl.when(i % 2 == 1)
    def _():
        tile([i - 1], False)

    ng = ng_ref[...]
    for p in pairs:
        acc = acc_ref[p]
        l = l_ref[p]
        outs = []
        for half in range(2):
            oh = (acc[half * FOX_DH:(half + 1) * FOX_DH, half * tq:(half + 1) * tq]
                  / l[:, half * tq:(half + 1) * tq])
            ms = jnp.mean(oh * oh, 0, keepdims=True)
            outs.append(oh * lax.rsqrt(ms + NORM_EPS))
        o_ref[:, pcols[p]] = (jnp.transpose(jnp.concatenate(outs, axis=0)) * ng).astype(BF16)


def _fox(proj_big, gcol, norm_g2, batch, seq):
    nq = seq // FOX_TQ
    first = (GDN_QKV + GDN_WIDTH) // FOX_WIDTH
    npair = FOX_HEADS // 2
    return pl.pallas_call(
        _fox_kernel,
        grid=(batch, nq),
        in_specs=[
            pl.BlockSpec((FOX_TQ, FOX_WIDTH), lambda b, i: (b * nq + i, first)),
            pl.BlockSpec((seq, FOX_WIDTH), lambda b, i: (b, first + 1)),
            pl.BlockSpec((seq, FOX_WIDTH), lambda b, i: (b, first + 2)),
            pl.BlockSpec((seq, N_SMALL), lambda b, i: (b, 0)),
            _resident((1, 2 * FOX_DH)),
        ],
        out_specs=pl.BlockSpec((FOX_TQ, FOX_WIDTH), lambda b, i: (b * nq + i, 0)),
        out_shape=jax.ShapeDtypeStruct((batch * seq, FOX_WIDTH), BF16),
        scratch_shapes=[
            pltpu.VMEM((npair, 2 * FOX_TQ, 2 * FOX_DH + FOX_X), BF16),
            pltpu.VMEM((npair, seq, 2 * FOX_DH + FOX_X), BF16),
            pltpu.VMEM((npair, nq, 2 * FOX_DH, FOX_TQ), BF16),
            pltpu.VMEM((npair, 1, 2 * FOX_TQ), F32),
            pltpu.VMEM((npair, 1, 2 * FOX_TQ), F32),
            pltpu.VMEM((npair, 2 * FOX_DH, 2 * FOX_TQ), F32),
        ],
        compiler_params=pltpu.CompilerParams(
            dimension_semantics=("arbitrary", "arbitrary"), vmem_limit_bytes=VMEM_LIMIT),
        name="fox",
    )(proj_big, proj_big, proj_big, gcol, norm_g2)


def _tail_kernel(x_ref, p_ref, og_ref, of_ref, lin_g, lin_b, wo_ref, l1g, l1b, wu_ref, wd_ref,
                 wp_ref, wg_ref, bg_ref, l2g, l2b, o_ref):
    half = x_ref.shape[0] // 2
    ra, rb = slice(0, half), slice(half, 2 * half)

    def head(rows):
        h = _layer_norm(x_ref[rows, :], lin_g[...], lin_b[...])
        mix = (_dot(og_ref[rows, :], wo_ref[0:GDN_WIDTH, :])
               + _dot(of_ref[rows, :], wo_ref[GDN_WIDTH:, :]))
        return _layer_norm(ALPHA * h + mix, l1g[...], l1b[...])

    def mlp(rows, h1):
        h1b = h1.astype(BF16)
        gate = _sigmoid(_dot(h1b, wg_ref[...]) + bg_ref[...])
        acc = ALPHA * h1 + _dot(p_ref[rows, :].astype(BF16), wp_ref[...]) * gate
        for j in range(D_FF // TAIL_TF):
            cols = slice(j * TAIL_TF, (j + 1) * TAIL_TF)
            a = jnp.maximum(_dot(h1b, wu_ref[:, cols]), 0.0)
            acc = acc + _dot((a * a).astype(BF16), wd_ref[cols, :])
        return acc

    h1a = head(ra)
    h1b_ = head(rb)
    acc_a = mlp(ra, h1a)
    o_ref[ra, :] = _layer_norm(acc_a, l2g[...], l2b[...])
    acc_b = mlp(rb, h1b_)
    o_ref[rb, :] = _layer_norm(acc_b, l2g[...], l2b[...])


def _tail(x2, p2, o_gdn, o_fox, lin_g, lin_b, w_out, l1g, l1b, w_up, w_down, w_ple, w_gate,
          b_gate, l2g, l2b):
    n = x2.shape[0]
    row = lambda width: pl.BlockSpec((TAIL_TM, width), lambda i: (i, 0))
    vec = _resident((1, D_MODEL))
    return pl.pallas_call(
        _tail_kernel,
        grid=(n // TAIL_TM,),
        in_specs=[
            row(D_MODEL), row(D_PLE), row(GDN_WIDTH), row(FOX_WIDTH),
            vec, vec, _resident((D_MODEL, D_MODEL)), vec, vec,
            _resident((D_MODEL, D_FF)), _resident((D_FF, D_MODEL)),
            _resident((D_PLE, D_MODEL)), _resident((D_MODEL, D_MODEL)), vec, vec, vec,
        ],
        out_specs=row(D_MODEL),
        out_shape=jax.ShapeDtypeStruct((n, D_MODEL), F32),
        compiler_params=pltpu.CompilerParams(
            dimension_semantics=("arbitrary",), vmem_limit_bytes=VMEM_LIMIT),
        name="tail",
    )(x2, p2, o_gdn, o_fox, lin_g, lin_b, w_out, l1g, l1b, w_up, w_down, w_ple, w_gate,
      b_gate, l2g, l2b)


def kernel(x, p, ln_in_g, ln_in_b, w_in, conv_w, a_log, dt_bias, gdn_norm_g, b_f, fox_norm_g,
           w_out, ln1_g, ln1_b, w_up, w_down, w_ple, w_ple_gate, b_ple_gate, ln2_g, ln2_b):
    batch, seq, _ = x.shape
    assert x.shape[2] == D_MODEL and w_in.shape[0] == 1
    assert seq % FOX_TQ == 0 and seq % GDN_TB == 0 and (batch * seq) % PROJ_TM == 0
    n = batch * seq
    x2 = x.reshape(n, D_MODEL)
    p2 = p[0].reshape(n, D_PLE)
    row = lambda a: a.reshape(1, -1).astype(F32)

    w0 = w_in[0]
    w_big = jnp.concatenate([w0[:, :OFF_BETA], w0[:, OFF_FOX:OFF_F]], axis=1).astype(BF16)
    w_decay = w0[:, OFF_BETA + GDN_HEADS:OFF_FOX]
    n_gate = LANE_GTOT + GDN_HEADS
    w_small = jnp.concatenate(
        [w0[:, OFF_BETA:OFF_FOX], w0[:, OFF_F:], w_decay, jnp.zeros((D_MODEL, N_SMALL - n_gate), F32)],
        axis=1).astype(BF16)
    zeros4 = jnp.zeros((GDN_HEADS,), F32)
    pad = jnp.zeros((N_SMALL - n_gate,), F32)
    pvec = jnp.zeros((8, N_SMALL), F32)
    pvec = pvec.at[0].set(jnp.concatenate([zeros4, dt_bias[0], b_f[0], dt_bias[0], pad]))
    pvec = pvec.at[1].set(
        jnp.concatenate([zeros4, a_log[0], jnp.zeros((FOX_HEADS,), F32), a_log[0], pad]))

    proj_big, small = _proj(x2, row(ln_in_g), row(ln_in_b), w_big, w_small, conv_w[0], seq)
    gcol = _gates(small, pvec, batch, seq)

    grow = jnp.transpose(
        gcol[:, LANE_GAM:LANE_C].reshape(batch, seq // GDN_TB, GDN_TB, GDN_HEADS), (0, 1, 3, 2))

    o_gdn = _gdn(proj_big.reshape(batch, seq, N_BIG), gcol.reshape(batch, seq, N_SMALL), grow,
                 row(gdn_norm_g[0]), batch, seq).reshape(n, GDN_WIDTH)
    o_fox = _fox(proj_big, gcol, row(jnp.tile(fox_norm_g[0], 2)), batch, seq)

    out = _tail(x2, p2, o_gdn, o_fox, row(ln_in_g), row(ln_in_b), w_out[0].astype(BF16),
                row(ln1_g[0]), row(ln1_b[0]), w_up[0].astype(BF16), w_down[0].astype(BF16),
                w_ple[0].astype(BF16), w_ple_gate[0].astype(BF16), row(b_ple_gate[0]),
                row(ln2_g[0]), row(ln2_b[0]))
    return out.reshape(batch, seq, D_MODEL)
```

```python
import functools

import jax
import jax.numpy as jnp
from jax import lax
from jax.experimental import pallas as pl
from jax.experimental.pallas import tpu as pltpu

F32 = jnp.float32
BF16 = jnp.bfloat16
HIGHEST = lax.Precision.HIGHEST

D_MODEL = 1024
CHUNK = 64
GDN_HEADS = 4
GDN_DK = 128
GDN_DV = 128
GDN_QK = GDN_HEADS * GDN_DK
GDN_QKV = 3 * GDN_QK
GDN_WIDTH = GDN_HEADS * GDN_DV
FOX_HEADS = 8
FOX_DH = 64
FOX_WIDTH = FOX_HEADS * FOX_DH
CONV_W = 4
D_FF = 4 * D_MODEL
D_PLE = 256
LN_EPS = 1e-5
NORM_EPS = 1e-6
ALPHA = 2.0 ** 0.25

OFF_Z = GDN_QKV
OFF_BETA = OFF_Z + GDN_WIDTH
OFF_FOX = OFF_BETA + 2 * GDN_HEADS
OFF_F = OFF_FOX + 3 * FOX_WIDTH
N_BIG = GDN_QKV + GDN_WIDTH + 3 * FOX_WIDTH
N_SMALL = 128
LANE_BETA = 0
LANE_GAM = GDN_HEADS
LANE_C = 2 * GDN_HEADS
LANE_GTOT = LANE_C + FOX_HEADS

VMEM_LIMIT = 56 * 1024 * 1024

PROJ_TM = 512
PROJ_TN = 256
GATE_TB = 256
GDN_TB = 256
GDN_NB = 4
FOX_TQ = 256
TAIL_TM = 512
TAIL_TF = 1024


def _layer_norm(x, g, b):
    mu = jnp.mean(x, -1, keepdims=True)
    xc = x - mu
    var = jnp.mean(xc * xc, -1, keepdims=True)
    return xc * lax.rsqrt(var + LN_EPS) * g + b


def _softplus(x):
    return jnp.maximum(x, 0.0) + jnp.log(1.0 + jnp.exp(-jnp.abs(x)))


def _sigmoid(x):
    return 1.0 / (1.0 + jnp.exp(-x))


def _dot(a, b):
    return jnp.dot(a, b, preferred_element_type=F32)


def _dot_hi(a, b):
    return jnp.dot(a, b, precision=HIGHEST, preferred_element_type=F32)


def _dot_nt(a, b):
    return lax.dot_general(a, b, (((1,), (1,)), ((), ())), preferred_element_type=F32)


def _dot_tn(a, b):
    return lax.dot_general(a, b, (((0,), (0,)), ((), ())), preferred_element_type=F32)


def _resident(shape):
    return pl.BlockSpec(shape, lambda *_: (0,) * len(shape), pipeline_mode=pl.Buffered(1))


def _proj_kernel(x_ref, g_ref, b_ref, wb_ref, ws_ref, cw_ref, ob_ref, os_ref, halo_ref, *,
                 tiles_per_seq):
    tm = x_ref.shape[0]
    halo = halo_ref.shape[0]

    @pl.when(pl.program_id(0) % tiles_per_seq == 0)
    def _():
        halo_ref[...] = jnp.zeros_like(halo_ref)

    h = _layer_norm(x_ref[...], g_ref[...], b_ref[...]).astype(BF16)
    nchunk = N_BIG // PROJ_TN
    nconv = GDN_QKV // PROJ_TN
    order = [c for pair in zip(range(nconv), range(nconv, 2 * nconv)) for c in pair]
    order += list(range(2 * nconv, nchunk))
    pending = _dot(h, wb_ref[:, order[0] * PROJ_TN:(order[0] + 1) * PROJ_TN])
    for pos, j in enumerate(order):
        cols = slice(j * PROJ_TN, (j + 1) * PROJ_TN)
        acc = pending
        if pos + 1 < nchunk:
            nxt = order[pos + 1]
            pending = _dot(h, wb_ref[:, nxt * PROJ_TN:(nxt + 1) * PROJ_TN])
        if (j + 1) * PROJ_TN <= GDN_QKV:
            ext = jnp.concatenate([halo_ref[:, cols], acc], axis=0)
            y = acc * cw_ref[CONV_W - 1:CONV_W, cols]
            for d in range(1, CONV_W):
                y = y + pltpu.roll(ext, d, 0)[halo:] * cw_ref[CONV_W - 1 - d:CONV_W - d, cols]
            halo_ref[:, cols] = acc[tm - halo:]
            acc = y * _sigmoid(y)
        ob_ref[:, cols] = acc.astype(BF16)
    os_ref[...] = _dot(h, ws_ref[...])


def _proj(x2, ln_g, ln_b, w_big, w_small, conv_w, seq):
    n = x2.shape[0]
    assert GDN_QKV % PROJ_TN == 0 and seq % PROJ_TM == 0
    return pl.pallas_call(
        functools.partial(_proj_kernel, tiles_per_seq=seq // PROJ_TM),
        grid=(n // PROJ_TM,),
        in_specs=[
            pl.BlockSpec((PROJ_TM, D_MODEL), lambda i: (i, 0)),
            _resident((1, D_MODEL)),
            _resident((1, D_MODEL)),
            _resident((D_MODEL, N_BIG)),
            _resident((D_MODEL, N_SMALL)),
            _resident((CONV_W, GDN_QKV)),
        ],
        out_specs=[
            pl.BlockSpec((PROJ_TM, N_BIG), lambda i: (i, 0)),
            pl.BlockSpec((PROJ_TM, N_SMALL), lambda i: (i, 0)),
        ],
        out_shape=[
            jax.ShapeDtypeStruct((n, N_BIG), BF16),
            jax.ShapeDtypeStruct((n, N_SMALL), F32),
        ],
        scratch_shapes=[pltpu.VMEM((8, GDN_QKV), F32)],
        compiler_params=pltpu.CompilerParams(
            dimension_semantics=("arbitrary",), vmem_limit_bytes=VMEM_LIMIT),
        name="proj",
    )(x2, ln_g, ln_b, w_big, w_small, conv_w)


def _gates_kernel(s_ref, pv_ref, o_ref, *, seq):
    lane = lax.broadcasted_iota(jnp.int32, (1, N_SMALL), 1)
    bias = pv_ref[0:1, :]
    neg_a = -jnp.exp(pv_ref[1:2, :])
    r = lax.broadcasted_iota(jnp.int32, (GATE_TB, GATE_TB), 0)
    c = lax.broadcasted_iota(jnp.int32, (GATE_TB, GATE_TB), 1)
    tri = r >= c
    l_full = jnp.where(tri, 1.0, 0.0).astype(F32)
    same_chunk = jnp.right_shift(r, 6) == jnp.right_shift(c, 6)
    l_chunk = jnp.where(same_chunk, l_full, 0.0)
    l_total = jnp.where(same_chunk, 1.0, 0.0).astype(F32)
    carry = jnp.zeros((1, N_SMALL), F32)
    for t in range(seq // GATE_TB):
        rows = slice(t * GATE_TB, (t + 1) * GATE_TB)
        x = s_ref[rows, :] + bias
        beta = _sigmoid(x)
        log_g = neg_a * _softplus(x)
        log_f = -_softplus(-x)
        gam = _dot_hi(l_chunk, log_g)
        gtot = _dot_hi(l_total, log_g)
        cum = _dot_hi(l_full, log_f) + carry
        carry = cum[GATE_TB - 1:GATE_TB, :]
        o_ref[rows, :] = jnp.where(
            lane < LANE_GAM, beta,
            jnp.where(lane < LANE_C, gam, jnp.where(lane < LANE_GTOT, cum, gtot)))


def _gates(small, pvec, batch, seq):
    return pl.pallas_call(
        functools.partial(_gates_kernel, seq=seq),
        grid=(batch,),
        in_specs=[
            pl.BlockSpec((seq, N_SMALL), lambda b: (b, 0)),
            _resident((8, N_SMALL)),
        ],
        out_specs=pl.BlockSpec((seq, N_SMALL), lambda b: (b, 0)),
        out_shape=jax.ShapeDtypeStruct((batch * seq, N_SMALL), F32),
        compiler_params=pltpu.CompilerParams(
            dimension_semantics=("arbitrary",), vmem_limit_bytes=VMEM_LIMIT),
        name="gates",
    )(small, pvec)


def _gdn_kernel(x_ref, z_ref, gc_ref, gr_ref, ng_ref, o_ref, s_ref):
    nb, tb = x_ref.shape[0], x_ref.shape[1]
    nch = tb // CHUNK
    heads = range(nb * GDN_HEADS)
    nh = GDN_HEADS

    @pl.when(pl.program_id(1) == 0)
    def _():
        s_ref[...] = jnp.zeros_like(s_ref)

    def act(h, c0):
        c0 = c0 + (h % nh) * GDN_DK
        return x_ref[h // nh, :, c0:c0 + GDN_DK].astype(F32)

    def l2norm(v):
        return v * lax.rsqrt(jnp.sum(v * v, -1, keepdims=True) + NORM_EPS)

    ri = lax.broadcasted_iota(jnp.int32, (tb, tb), 0)
    ci = lax.broadcasted_iota(jnp.int32, (tb, tb), 1)
    same = jnp.right_shift(ri, 6) == jnp.right_shift(ci, 6)
    keep = jnp.logical_and(same, ri >= ci)
    diag = ri == ci
    pr = lax.broadcasted_iota(jnp.int32, (CHUNK, tb), 0)
    pc = lax.broadcasted_iota(jnp.int32, (CHUNK, tb), 1)
    eye_packed = jnp.where(jnp.bitwise_and(pc, CHUNK - 1) == pr, 1.0, 0.0).astype(F32)
    lane_chunk = jnp.right_shift(lax.broadcasted_iota(jnp.int32, (1, tb), 1), 6)
    row_chunk = jnp.right_shift(lax.broadcasted_iota(jnp.int32, (tb, 1), 0), 6)

    def to_bd(packed):
        return jnp.where(same, jnp.concatenate([packed] * nch, axis=0), jnp.zeros((), packed.dtype))

    def to_packed(bd):
        out = bd[0:CHUNK]
        for c in range(1, nch):
            out = out + bd[c * CHUNK:(c + 1) * CHUNK]
        return out

    ng = ng_ref[...]
    q = [l2norm(act(h, 0)) * (GDN_DK ** -0.5) for h in heads]
    k = [l2norm(act(h, GDN_QK)) for h in heads]
    v = [act(h, 2 * GDN_QK) for h in heads]

    def gate(h, lane0):
        return gc_ref[h // nh, :, lane0 + h % nh:lane0 + h % nh + 1]

    beta = [gate(h, LANE_BETA) for h in heads]
    gam = [gate(h, LANE_GAM) for h in heads]
    gtot = [gate(h, LANE_GTOT) for h in heads]
    grow = [gr_ref[h // nh, 0, h % nh:h % nh + 1, :] for h in heads]
    egam = [jnp.exp(g) for g in gam]
    kb = [a.astype(BF16) for a in k]
    decay = [jnp.exp(jnp.where(keep, gam[h] - grow[h], -jnp.inf)) for h in heads]
    gram = [_dot_nt(kb[h], kb[h]) for h in heads]
    qk = [(_dot_nt(q[h].astype(BF16), kb[h]) * decay[h]).astype(BF16) for h in heads]

    x_bd = [jnp.where(diag, 0.0, -(gram[h] * beta[h]) * decay[h]) for h in heads]
    x_p = [to_packed(a) for a in x_bd]
    p_p = [eye_packed + a for a in x_p]
    x_p = [_dot(x_p[h].astype(BF16), x_bd[h].astype(BF16)) for h in heads]
    for _ in range(4):
        w_bd = [to_bd(a.astype(BF16)) for a in x_p]
        r = [_dot(jnp.concatenate([p_p[h], x_p[h]], axis=0).astype(BF16), w_bd[h]) for h in heads]
        p_p = [p_p[h] + r[h][:CHUNK] for h in heads]
        x_p = [r[h][CHUNK:] for h in heads]
    p_p = [p_p[h] + _dot(p_p[h].astype(BF16), to_bd(x_p[h].astype(BF16))) for h in heads]

    rhs = [jnp.concatenate([v[h] * beta[h], k[h] * (beta[h] * egam[h])], axis=1).astype(BF16)
           for h in heads]
    sol = [_dot(to_bd(p_p[h].astype(BF16)), rhs[h]) for h in heads]
    u = [a[:, :GDN_DV] for a in sol]
    w = [a[:, GDN_DV:].astype(BF16) for a in sol]
    qd = [(q[h] * egam[h]).astype(BF16) for h in heads]
    kd_t = [jnp.transpose(k[h] * jnp.exp(gtot[h] - gam[h])).astype(BF16) for h in heads]
    z = [z_ref[h // nh, :, (h % nh) * GDN_DV:(h % nh + 1) * GDN_DV].astype(F32) for h in heads]
    zgate = [a * _sigmoid(a) for a in z]

    state = [s_ref[h] for h in heads]
    for c in range(nch):
        rows = slice(c * CHUNK, (c + 1) * CHUNK)
        sb = [a.astype(BF16) for a in state]
        r1 = [_dot(jnp.concatenate([w[h][rows], qd[h][rows]], axis=0), sb[h]) for h in heads]
        v_new = [u[h][rows] - r1[h][:CHUNK] for h in heads]
        v_full = [jnp.where(row_chunk == c, jnp.concatenate([a.astype(BF16)] * nch, axis=0),
                            jnp.zeros((), BF16)) for a in v_new]
        kd_c = [jnp.where(lane_chunk == c, a, jnp.zeros((), BF16)) for a in kd_t]
        r2 = [_dot(jnp.concatenate([qk[h][rows], kd_c[h]], axis=0), v_full[h]) for h in heads]
        for h in heads:
            o = r1[h][CHUNK:] + r2[h][:CHUNK]
            on = o * lax.rsqrt(jnp.mean(o * o, -1, keepdims=True) + NORM_EPS) * ng
            o_ref[h // nh, rows, (h % nh) * GDN_DV:(h % nh + 1) * GDN_DV] = (
                on * zgate[h][rows]).astype(BF16)
        state = [state[h] * jnp.exp(gtot[h][c * CHUNK:c * CHUNK + 1]) + r2[h][CHUNK:] for h in heads]
    for h in heads:
        s_ref[h] = state[h]


def _gdn(proj_big, gcol, grow, norm_g, batch, seq):
    nt = seq // GDN_TB
    nb = GDN_NB
    assert batch % nb == 0
    return pl.pallas_call(
        _gdn_kernel,
        grid=(batch // nb, nt),
        in_specs=[
            pl.BlockSpec((nb, GDN_TB, GDN_QKV), lambda g, t: (g, t, 0)),
            pl.BlockSpec((nb, GDN_TB, GDN_WIDTH), lambda g, t: (g, t, OFF_Z // GDN_WIDTH)),
            pl.BlockSpec((nb, GDN_TB, N_SMALL), lambda g, t: (g, t, 0)),
            pl.BlockSpec((nb, 1, GDN_HEADS, GDN_TB), lambda g, t: (g, t, 0, 0)),
            _resident((1, GDN_DV)),
        ],
        out_specs=pl.BlockSpec((nb, GDN_TB, GDN_WIDTH), lambda g, t: (g, t, 0)),
        out_shape=jax.ShapeDtypeStruct((batch, seq, GDN_WIDTH), BF16),
        scratch_shapes=[
            pltpu.VMEM((nb * GDN_HEADS, GDN_DK, GDN_DV), F32),
        ],
        compiler_params=pltpu.CompilerParams(
            dimension_semantics=("arbitrary", "arbitrary"), vmem_limit_bytes=VMEM_LIMIT),
        name="gdn",
    )(proj_big, proj_big, gcol, grow, norm_g)


FOX_X = 128
FOX_QC = (0, 6)
FOX_KC = (3, 9)


def _split3(c):
    hi = c.astype(BF16)
    r1 = c - hi.astype(F32)
    mid = r1.astype(BF16)
    lo = (r1 - mid.astype(F32)).astype(BF16)
    return jnp.concatenate([hi, mid, lo], axis=1)


def _select3(pair, base):
    r = lax.broadcasted_iota(jnp.int32, (3 * N_SMALL, FOX_X), 0)
    c = lax.broadcasted_iota(jnp.int32, (3 * N_SMALL, FOX_X), 1)
    d = jnp.right_shift(r, 7)
    head = jnp.bitwise_and(r, N_SMALL - 1) - (LANE_C + 2 * pair)
    e = jnp.where(head == 0, jnp.where(c == base[0] + d, 1.0, 0.0),
                  jnp.where(head == 1, jnp.where(c == base[1] + d, 1.0, 0.0), 0.0))
    return e.astype(BF16)


def _lane_ones(first):
    lane = lax.broadcasted_iota(jnp.int32, (1, FOX_X), 1)
    out = jnp.zeros((1, FOX_X), F32)
    for f in first:
        out = jnp.where(lane < f, out, jnp.where(lane < f + 3, 1.0, out))
    return out


def _fox_kernel(q_ref, k_ref, v_ref, gc_ref, ng_ref, o_ref, qa_ref, ka_ref, vt_ref, m_ref, l_ref,
                acc_ref, s_ref):
    tq = q_ref.shape[0]
    seq = k_ref.shape[0]
    i = pl.program_id(1)
    pairs = range(FOX_HEADS // 2)
    lane = lax.broadcasted_iota(jnp.int32, (1, 2 * FOX_DH), 1)
    lo_half = lane < FOX_DH
    pcols = [slice(p * 2 * FOX_DH, (p + 1) * 2 * FOX_DH) for p in pairs]

    @pl.when(i == 0)
    def _():
        ones_k = _lane_ones(FOX_QC)
        sel_k = [_select3(p, FOX_KC) for p in pairs]

        def fill(t, carry):
            rows = pl.ds(pl.multiple_of(t * tq, tq), tq)
            pieces = _split3(gc_ref[rows, :])
            for p in pairs:
                ka_ref[p, rows, 0:2 * FOX_DH] = k_ref[rows, pcols[p]]
                ka_ref[p, rows, 2 * FOX_DH:] = (ones_k - _dot(pieces, sel_k[p])).astype(BF16)
                vt_ref[p, t] = jnp.transpose(v_ref[rows, pcols[p]].astype(F32)).astype(BF16)
            return carry

        lax.fori_loop(0, seq // tq, fill, 0)

    qrows = pl.ds(pl.multiple_of(i * tq, tq), tq)
    pieces_q = _split3(gc_ref[qrows, :])
    ones_q = _lane_ones(FOX_KC)
    xlane = lax.broadcasted_iota(jnp.int32, (1, FOX_X), 1)
    for p in pairs:
        qp = q_ref[:, pcols[p]] * (FOX_DH ** -0.5)
        qx = _dot(pieces_q, _select3(p, FOX_QC)) + ones_q
        for half in range(2):
            rows = slice(half * tq, (half + 1) * tq)
            mine = lo_half if half == 0 else jnp.logical_not(lo_half)
            xmine = (xlane < FOX_QC[1]) if half == 0 else (xlane >= FOX_QC[1])
            qa_ref[p, rows, 0:2 * FOX_DH] = jnp.where(mine, qp, jnp.zeros((), BF16))
            qa_ref[p, rows, 2 * FOX_DH:] = jnp.where(xmine, qx, 0.0).astype(BF16)

    ki = lax.broadcasted_iota(jnp.int32, (tq, 2 * tq), 0)
    qi = jnp.bitwise_and(lax.broadcasted_iota(jnp.int32, (tq, 2 * tq), 1), tq - 1)
    causal = ki <= qi

    def scores(j):
        krows = pl.ds(pl.multiple_of(j * tq, tq), tq)
        return [_dot_nt(ka_ref[p, krows, :], qa_ref[p]) for p in pairs]

    def absorb(s, j, first):
        for p in pairs:
            sp = s[p]
            if first:
                sp = jnp.where(causal, sp, -jnp.inf)
            m_new = jnp.max(sp, 0, keepdims=True)
            if not first:
                m_prev = m_ref[p]
                m_new = jnp.maximum(m_prev, m_new)
                scale = jnp.exp(m_prev - m_new)
            prob = jnp.exp(sp - m_new)
            psum = jnp.sum(prob, 0, keepdims=True)
            pv = _dot(vt_ref[p, j], prob.astype(BF16))
            m_ref[p] = m_new
            if first:
                l_ref[p] = psum
                acc_ref[p] = pv
            else:
                l_ref[p] = scale * l_ref[p] + psum
                acc_ref[p] = scale * acc_ref[p] + pv

    last = jnp.maximum(i - 1, 0)

    def stage(slot, j_next):
        s_next = scores(jnp.minimum(j_next, last))
        for p in pairs:
            s_ref[slot, p] = s_next[p]

    def consume(slot, j):
        absorb([s_ref[slot, p] for p in pairs], j, False)

    s_diag = scores(i)
    stage(0, 0)
    absorb(s_diag, i, True)

    def body(t, carry):
        stage(1, 2 * t + 1)
        consume(0, 2 * t)
        stage(0, 2 * t + 2)
        consume(1, 2 * t + 1)
        return carry

    lax.fori_loop(0, i // 2, body, 0)

    @pl.when(i % 2 == 1)
    def _():
        consume(0, i - 1)

    ng = ng_ref[...]
    for p in pairs:
        acc = acc_ref[p]
        l = l_ref[p]
        outs = []
        for half in range(2):
            oh = (acc[half * FOX_DH:(half + 1) * FOX_DH, half * tq:(half + 1) * tq]
                  / l[:, half * tq:(half + 1) * tq])
            ms = jnp.mean(oh * oh, 0, keepdims=True)
            outs.append(oh * lax.rsqrt(ms + NORM_EPS))
        o_ref[:, pcols[p]] = (jnp.transpose(jnp.concatenate(outs, axis=0)) * ng).astype(BF16)


def _fox(proj_big, gcol, norm_g2, batch, seq):
    nq = seq // FOX_TQ
    first = (GDN_QKV + GDN_WIDTH) // FOX_WIDTH
    npair = FOX_HEADS // 2
    return pl.pallas_call(
        _fox_kernel,
        grid=(batch, nq),
        in_specs=[
            pl.BlockSpec((FOX_TQ, FOX_WIDTH), lambda b, i: (b * nq + i, first)),
            pl.BlockSpec((seq, FOX_WIDTH), lambda b, i: (b, first + 1)),
            pl.BlockSpec((seq, FOX_WIDTH), lambda b, i: (b, first + 2)),
            pl.BlockSpec((seq, N_SMALL), lambda b, i: (b, 0)),
            _resident((1, 2 * FOX_DH)),
        ],
        out_specs=pl.BlockSpec((FOX_TQ, FOX_WIDTH), lambda b, i: (b * nq + i, 0)),
        out_shape=jax.ShapeDtypeStruct((batch * seq, FOX_WIDTH), BF16),
        scratch_shapes=[
            pltpu.VMEM((npair, 2 * FOX_TQ, 2 * FOX_DH + FOX_X), BF16),
            pltpu.VMEM((npair, seq, 2 * FOX_DH + FOX_X), BF16),
            pltpu.VMEM((npair, nq, 2 * FOX_DH, FOX_TQ), BF16),
            pltpu.VMEM((npair, 1, 2 * FOX_TQ), F32),
            pltpu.VMEM((npair, 1, 2 * FOX_TQ), F32),
            pltpu.VMEM((npair, 2 * FOX_DH, 2 * FOX_TQ), F32),
            pltpu.VMEM((2, npair, FOX_TQ, 2 * FOX_TQ), F32),
        ],
        compiler_params=pltpu.CompilerParams(
            dimension_semantics=("arbitrary", "arbitrary"), vmem_limit_bytes=VMEM_LIMIT),
        name="fox",
    )(proj_big, proj_big, proj_big, gcol, norm_g2)


def _tail_kernel(x_ref, p_ref, og_ref, of_ref, lin_g, lin_b, wo_ref, l1g, l1b, wu_ref, wd_ref,
                 wp_ref, wg_ref, bg_ref, l2g, l2b, o_ref):
    half = x_ref.shape[0] // 2
    ra, rb = slice(0, half), slice(half, 2 * half)

    def head(rows):
        h = _layer_norm(x_ref[rows, :], lin_g[...], lin_b[...])
        mix = (_dot(og_ref[rows, :], wo_ref[0:GDN_WIDTH, :])
               + _dot(of_ref[rows, :], wo_ref[GDN_WIDTH:, :]))
        return _layer_norm(ALPHA * h + mix, l1g[...], l1b[...])

    def mlp(rows, h1):
        h1b = h1.astype(BF16)
        gate = _sigmoid(_dot(h1b, wg_ref[...]) + bg_ref[...])
        acc = ALPHA * h1 + _dot(p_ref[rows, :].astype(BF16), wp_ref[...]) * gate
        for j in range(D_FF // TAIL_TF):
            cols = slice(j * TAIL_TF, (j + 1) * TAIL_TF)
            a = jnp.maximum(_dot(h1b, wu_ref[:, cols]), 0.0)
            acc = acc + _dot((a * a).astype(BF16), wd_ref[cols, :])
        return acc

    h1a = head(ra)
    h1b_ = head(rb)
    acc_a = mlp(ra, h1a)
    o_ref[ra, :] = _layer_norm(acc_a, l2g[...], l2b[...])
    acc_b = mlp(rb, h1b_)
    o_ref[rb, :] = _layer_norm(acc_b, l2g[...], l2b[...])


def _tail(x2, p2, o_gdn, o_fox, lin_g, lin_b, w_out, l1g, l1b, w_up, w_down, w_ple, w_gate,
          b_gate, l2g, l2b):
    n = x2.shape[0]
    row = lambda width: pl.BlockSpec((TAIL_TM, width), lambda i: (i, 0))
    vec = _resident((1, D_MODEL))
    return pl.pallas_call(
        _tail_kernel,
        grid=(n // TAIL_TM,),
        in_specs=[
            row(D_MODEL), row(D_PLE), row(GDN_WIDTH), row(FOX_WIDTH),
            vec, vec, _resident((D_MODEL, D_MODEL)), vec, vec,
            _resident((D_MODEL, D_FF)), _resident((D_FF, D_MODEL)),
            _resident((D_PLE, D_MODEL)), _resident((D_MODEL, D_MODEL)), vec, vec, vec,
        ],
        out_specs=row(D_MODEL),
        out_shape=jax.ShapeDtypeStruct((n, D_MODEL), F32),
        compiler_params=pltpu.CompilerParams(
            dimension_semantics=("arbitrary",), vmem_limit_bytes=VMEM_LIMIT),
        name="tail",
    )(x2, p2, o_gdn, o_fox, lin_g, lin_b, w_out, l1g, l1b, w_up, w_down, w_ple, w_gate,
      b_gate, l2g, l2b)


def kernel(x, p, ln_in_g, ln_in_b, w_in, conv_w, a_log, dt_bias, gdn_norm_g, b_f, fox_norm_g,
           w_out, ln1_g, ln1_b, w_up, w_down, w_ple, w_ple_gate, b_ple_gate, ln2_g, ln2_b):
    batch, seq, _ = x.shape
    assert x.shape[2] == D_MODEL and w_in.shape[0] == 1
    assert seq % FOX_TQ == 0 and seq % GDN_TB == 0 and (batch * seq) % PROJ_TM == 0
    n = batch * seq
    x2 = x.reshape(n, D_MODEL)
    p2 = p[0].reshape(n, D_PLE)
    row = lambda a: a.reshape(1, -1).astype(F32)

    w0 = w_in[0]
    w_big = jnp.concatenate([w0[:, :OFF_BETA], w0[:, OFF_FOX:OFF_F]], axis=1).astype(BF16)
    w_decay = w0[:, OFF_BETA + GDN_HEADS:OFF_FOX]
    n_gate = LANE_GTOT + GDN_HEADS
    w_small = jnp.concatenate(
        [w0[:, OFF_BETA:OFF_FOX], w0[:, OFF_F:], w_decay, jnp.zeros((D_MODEL, N_SMALL - n_gate), F32)],
        axis=1).astype(BF16)
    zeros4 = jnp.zeros((GDN_HEADS,), F32)
    pad = jnp.zeros((N_SMALL - n_gate,), F32)
    pvec = jnp.zeros((8, N_SMALL), F32)
    pvec = pvec.at[0].set(jnp.concatenate([zeros4, dt_bias[0], b_f[0], dt_bias[0], pad]))
    pvec = pvec.at[1].set(
        jnp.concatenate([zeros4, a_log[0], jnp.zeros((FOX_HEADS,), F32), a_log[0], pad]))

    proj_big, small = _proj(x2, row(ln_in_g), row(ln_in_b), w_big, w_small, conv_w[0], seq)
    gcol = _gates(small, pvec, batch, seq)

    grow = jnp.transpose(
        gcol[:, LANE_GAM:LANE_C].reshape(batch, seq // GDN_TB, GDN_TB, GDN_HEADS), (0, 1, 3, 2))

    o_gdn = _gdn(proj_big.reshape(batch, seq, N_BIG), gcol.reshape(batch, seq, N_SMALL), grow,
                 row(gdn_norm_g[0]), batch, seq).reshape(n, GDN_WIDTH)
    o_fox = _fox(proj_big, gcol, row(jnp.tile(fox_norm_g[0], 2)), batch, seq)

    out = _tail(x2, p2, o_gdn, o_fox, row(ln_in_g), row(ln_in_b), w_out[0].astype(BF16),
                row(ln1_g[0]), row(ln1_b[0]), w_up[0].astype(BF16), w_down[0].astype(BF16),
                w_ple[0].astype(BF16), w_ple_gate[0].astype(BF16), row(b_ple_gate[0]),
                row(ln2_g[0]), row(ln2_b[0]))
    return out.reshape(batch, seq, D_MODEL)
```

```python
import functools

import jax
import jax.numpy as jnp
from jax import lax
from jax.experimental import pallas as pl
from jax.experimental.pallas import tpu as pltpu

F32 = jnp.float32
BF16 = jnp.bfloat16

D_MODEL = 1024
CHUNK = 64
GDN_HEADS = 4
GDN_DK = 128
GDN_DV = 128
GDN_QK = GDN_HEADS * GDN_DK
GDN_QKV = 3 * GDN_QK
GDN_WIDTH = GDN_HEADS * GDN_DV
FOX_HEADS = 8
FOX_DH = 64
FOX_WIDTH = FOX_HEADS * FOX_DH
CONV_W = 4
D_FF = 4 * D_MODEL
D_PLE = 256
LN_EPS = 1e-5
NORM_EPS = 1e-6
ALPHA = 2.0 ** 0.25

OFF_Z = GDN_QKV
OFF_BETA = OFF_Z + GDN_WIDTH
OFF_FOX = OFF_BETA + 2 * GDN_HEADS
OFF_F = OFF_FOX + 3 * FOX_WIDTH
N_BIG = GDN_QKV + GDN_WIDTH + 3 * FOX_WIDTH
N_SMALL = 128
FOX_Q0 = GDN_QKV + GDN_WIDTH
LOG2E = 1.4426950408889634
FOX_QSCALE = FOX_DH ** -0.5 * LOG2E
LANE_BETA = 0
LANE_GAM = GDN_HEADS
LANE_C = 2 * GDN_HEADS
LANE_GTOT = LANE_C + FOX_HEADS

VMEM_LIMIT = 56 * 1024 * 1024

PROJ_TM = 512
PROJ_TN = 256
GATE_TB = 256
GDN_TB = 256
GDN_NB = 4
FOX_TQ = 256
TAIL_TM = 512
TAIL_TF = 1024


def _layer_norm(x, g, b):
    mu = jnp.mean(x, -1, keepdims=True)
    xc = x - mu
    var = jnp.mean(xc * xc, -1, keepdims=True)
    return xc * lax.rsqrt(var + LN_EPS) * g + b


def _softplus(x):
    return jnp.maximum(x, 0.0) + jnp.log(1.0 + jnp.exp(-jnp.abs(x)))


def _sigmoid(x):
    return 1.0 / (1.0 + jnp.exp(-x))


def _dot(a, b):
    return jnp.dot(a, b, preferred_element_type=F32)


def _dot_nt(a, b):
    return lax.dot_general(a, b, (((1,), (1,)), ((), ())), preferred_element_type=F32)


def _resident(shape):
    return pl.BlockSpec(shape, lambda *_: (0,) * len(shape), pipeline_mode=pl.Buffered(1))


def _proj_kernel(x_ref, g_ref, b_ref, wb_ref, ws_ref, cw_ref, ob_ref, os_ref, halo_ref, *,
                 tiles_per_seq):
    tm = x_ref.shape[0]
    halo = halo_ref.shape[0]

    @pl.when(pl.program_id(0) % tiles_per_seq == 0)
    def _():
        halo_ref[...] = jnp.zeros_like(halo_ref)

    h = _layer_norm(x_ref[...], g_ref[...], b_ref[...]).astype(BF16)
    nchunk = N_BIG // PROJ_TN
    nconv = GDN_QKV // PROJ_TN
    order = [c for pair in zip(range(nconv), range(nconv, 2 * nconv)) for c in pair]
    order += list(range(2 * nconv, nchunk))
    pending = _dot(h, wb_ref[:, order[0] * PROJ_TN:(order[0] + 1) * PROJ_TN])
    for pos, j in enumerate(order):
        cols = slice(j * PROJ_TN, (j + 1) * PROJ_TN)
        acc = pending
        if pos + 1 < nchunk:
            nxt = order[pos + 1]
            pending = _dot(h, wb_ref[:, nxt * PROJ_TN:(nxt + 1) * PROJ_TN])
        if (j + 1) * PROJ_TN <= GDN_QKV:
            ext = jnp.concatenate([halo_ref[:, cols], acc], axis=0)
            y = acc * cw_ref[CONV_W - 1:CONV_W, cols]
            for d in range(1, CONV_W):
                y = y + pltpu.roll(ext, d, 0)[halo:] * cw_ref[CONV_W - 1 - d:CONV_W - d, cols]
            halo_ref[:, cols] = acc[tm - halo:]
            acc = y * _sigmoid(y)
        elif FOX_Q0 <= j * PROJ_TN < FOX_Q0 + FOX_WIDTH:
            acc = acc * FOX_QSCALE
        ob_ref[:, cols] = acc.astype(BF16)
    os_ref[...] = _dot(h, ws_ref[...])


def _proj(x2, ln_g, ln_b, w_big, w_small, conv_w, seq):
    n = x2.shape[0]
    assert GDN_QKV % PROJ_TN == 0 and seq % PROJ_TM == 0
    return pl.pallas_call(
        functools.partial(_proj_kernel, tiles_per_seq=seq // PROJ_TM),
        grid=(n // PROJ_TM,),
        in_specs=[
            pl.BlockSpec((PROJ_TM, D_MODEL), lambda i: (i, 0)),
            _resident((1, D_MODEL)),
            _resident((1, D_MODEL)),
            _resident((D_MODEL, N_BIG)),
            _resident((D_MODEL, N_SMALL)),
            _resident((CONV_W, GDN_QKV)),
        ],
        out_specs=[
            pl.BlockSpec((PROJ_TM, N_BIG), lambda i: (i, 0)),
            pl.BlockSpec((PROJ_TM, N_SMALL), lambda i: (i, 0)),
        ],
        out_shape=[
            jax.ShapeDtypeStruct((n, N_BIG), BF16),
            jax.ShapeDtypeStruct((n, N_SMALL), F32),
        ],
        scratch_shapes=[pltpu.VMEM((8, GDN_QKV), F32)],
        compiler_params=pltpu.CompilerParams(
            dimension_semantics=("arbitrary",), vmem_limit_bytes=VMEM_LIMIT),
        name="proj",
    )(x2, ln_g, ln_b, w_big, w_small, conv_w)


def _gates_kernel(s_ref, pv_ref, o_ref, gr_ref, *, seq):
    lane = lax.broadcasted_iota(jnp.int32, (1, N_SMALL), 1)
    bias = pv_ref[0:1, :]
    neg_a = -jnp.exp(pv_ref[1:2, :])
    r = lax.broadcasted_iota(jnp.int32, (GATE_TB, GATE_TB), 0)
    c = lax.broadcasted_iota(jnp.int32, (GATE_TB, GATE_TB), 1)
    l_full = jnp.where(r >= c, 1.0, 0.0).astype(F32)
    same_chunk = jnp.right_shift(r, 6) == jnp.right_shift(c, 6)
    l_chunk = jnp.where(same_chunk, l_full, 0.0)
    l_total = jnp.where(same_chunk, 1.0, 0.0).astype(F32)
    l_decay = jnp.concatenate([l_chunk, l_total], axis=0).astype(BF16)
    l_forget = l_full.astype(BF16)

    def pieces_sum(a):
        return a[:, 0:N_SMALL] + a[:, N_SMALL:2 * N_SMALL] + a[:, 2 * N_SMALL:]

    carry = jnp.zeros((1, N_SMALL), F32)
    for t in range(seq // GATE_TB):
        rows = slice(t * GATE_TB, (t + 1) * GATE_TB)
        x = s_ref[rows, :] + bias
        beta = _sigmoid(x)
        log_g = neg_a * _softplus(x)
        log_f = -_softplus(-x)
        decay = pieces_sum(_dot(l_decay, _split3(log_g)))
        gam, gtot = decay[:GATE_TB], decay[GATE_TB:]
        cum = pieces_sum(_dot(l_forget, _split3(log_f))) + carry
        carry = cum[GATE_TB - 1:GATE_TB, :]
        o_ref[rows, :] = jnp.where(
            lane < LANE_GAM, beta,
            jnp.where(lane < LANE_C, gam, jnp.where(lane < LANE_GTOT, cum, gtot)))
        gr_ref[0, t] = jnp.transpose(gam)[LANE_GAM:LANE_GAM + GDN_HEADS, :]


def _gates(small, pvec, batch, seq):
    assert GATE_TB == GDN_TB
    nt = seq // GATE_TB
    return pl.pallas_call(
        functools.partial(_gates_kernel, seq=seq),
        grid=(batch,),
        in_specs=[
            pl.BlockSpec((seq, N_SMALL), lambda b: (b, 0)),
            _resident((8, N_SMALL)),
        ],
        out_specs=[
            pl.BlockSpec((seq, N_SMALL), lambda b: (b, 0)),
            pl.BlockSpec((1, nt, GDN_HEADS, GATE_TB), lambda b: (b, 0, 0, 0)),
        ],
        out_shape=[
            jax.ShapeDtypeStruct((batch * seq, N_SMALL), F32),
            jax.ShapeDtypeStruct((batch, nt, GDN_HEADS, GATE_TB), F32),
        ],
        compiler_params=pltpu.CompilerParams(
            dimension_semantics=("arbitrary",), vmem_limit_bytes=VMEM_LIMIT),
        name="gates",
    )(small, pvec)


def _gdn_kernel(x_ref, z_ref, gc_ref, gr_ref, ng_ref, o_ref, s_ref):
    nb, tb = x_ref.shape[0], x_ref.shape[1]
    nch = tb // CHUNK
    heads = range(nb * GDN_HEADS)
    nh = GDN_HEADS

    @pl.when(pl.program_id(1) == 0)
    def _():
        s_ref[...] = jnp.zeros_like(s_ref)

    def act(h, c0):
        c0 = c0 + (h % nh) * GDN_DK
        return x_ref[h // nh, :, c0:c0 + GDN_DK].astype(F32)

    def l2norm(v):
        return v * lax.rsqrt(jnp.sum(v * v, -1, keepdims=True) + NORM_EPS)

    ri = lax.broadcasted_iota(jnp.int32, (tb, tb), 0)
    ci = lax.broadcasted_iota(jnp.int32, (tb, tb), 1)
    same = jnp.right_shift(ri, 6) == jnp.right_shift(ci, 6)
    keep = jnp.logical_and(same, ri >= ci)
    diag = ri == ci
    pr = lax.broadcasted_iota(jnp.int32, (CHUNK, tb), 0)
    pc = lax.broadcasted_iota(jnp.int32, (CHUNK, tb), 1)
    eye_packed = jnp.where(jnp.bitwise_and(pc, CHUNK - 1) == pr, 1.0, 0.0).astype(F32)
    lane_chunk = jnp.right_shift(lax.broadcasted_iota(jnp.int32, (1, tb), 1), 6)
    row_chunk = jnp.right_shift(lax.broadcasted_iota(jnp.int32, (tb, 1), 0), 6)

    def to_bd(packed):
        return jnp.where(same, jnp.concatenate([packed] * nch, axis=0), jnp.zeros((), packed.dtype))

    def to_packed(bd):
        out = bd[0:CHUNK]
        for c in range(1, nch):
            out = out + bd[c * CHUNK:(c + 1) * CHUNK]
        return out

    ng = ng_ref[...]
    q = [l2norm(act(h, 0)) * (GDN_DK ** -0.5) for h in heads]
    k = [l2norm(act(h, GDN_QK)) for h in heads]
    v = [act(h, 2 * GDN_QK) for h in heads]

    def gate(h, lane0):
        return gc_ref[h // nh, :, lane0 + h % nh:lane0 + h % nh + 1]

    beta = [gate(h, LANE_BETA) for h in heads]
    gam = [gate(h, LANE_GAM) for h in heads]
    gtot = [gate(h, LANE_GTOT) for h in heads]
    grow = [gr_ref[h // nh, 0, h % nh:h % nh + 1, :] for h in heads]
    egam = [jnp.exp(g) for g in gam]
    kb = [a.astype(BF16) for a in k]
    decay = [jnp.exp(jnp.where(keep, gam[h] - grow[h], -jnp.inf)) for h in heads]
    gram = [_dot_nt(kb[h], kb[h]) for h in heads]
    qk = [(_dot_nt(q[h].astype(BF16), kb[h]) * decay[h]).astype(BF16) for h in heads]

    x_bd = [jnp.where(diag, 0.0, -(gram[h] * beta[h]) * decay[h]) for h in heads]
    x_p = [to_packed(a) for a in x_bd]
    p_p = [eye_packed + a for a in x_p]
    x_p = [_dot(x_p[h].astype(BF16), x_bd[h].astype(BF16)) for h in heads]
    for _ in range(4):
        w_bd = [to_bd(a.astype(BF16)) for a in x_p]
        r = [_dot(jnp.concatenate([p_p[h], x_p[h]], axis=0).astype(BF16), w_bd[h]) for h in heads]
        p_p = [p_p[h] + r[h][:CHUNK] for h in heads]
        x_p = [r[h][CHUNK:] for h in heads]
    p_p = [p_p[h] + _dot(p_p[h].astype(BF16), to_bd(x_p[h].astype(BF16))) for h in heads]

    rhs = [jnp.concatenate([v[h] * beta[h], k[h] * (beta[h] * egam[h])], axis=1).astype(BF16)
           for h in heads]
    sol = [_dot(to_bd(p_p[h].astype(BF16)), rhs[h]) for h in heads]
    u = [a[:, :GDN_DV] for a in sol]
    w = [a[:, GDN_DV:].astype(BF16) for a in sol]
    qd = [(q[h] * egam[h]).astype(BF16) for h in heads]
    kd_t = [jnp.transpose(k[h] * jnp.exp(gtot[h] - gam[h])).astype(BF16) for h in heads]
    z = [z_ref[h // nh, :, (h % nh) * GDN_DV:(h % nh + 1) * GDN_DV].astype(F32) for h in heads]
    zgate = [a * _sigmoid(a) for a in z]

    state = [s_ref[h] for h in heads]
    for c in range(nch):
        rows = slice(c * CHUNK, (c + 1) * CHUNK)
        sb = [a.astype(BF16) for a in state]
        r1 = [_dot(jnp.concatenate([w[h][rows], qd[h][rows]], axis=0), sb[h]) for h in heads]
        v_new = [u[h][rows] - r1[h][:CHUNK] for h in heads]
        v_full = [jnp.where(row_chunk == c, jnp.concatenate([a.astype(BF16)] * nch, axis=0),
                            jnp.zeros((), BF16)) for a in v_new]
        kd_c = [jnp.where(lane_chunk == c, a, jnp.zeros((), BF16)) for a in kd_t]
        r2 = [_dot(jnp.concatenate([qk[h][rows], kd_c[h]], axis=0), v_full[h]) for h in heads]
        for h in heads:
            o = r1[h][CHUNK:] + r2[h][:CHUNK]
            on = o * lax.rsqrt(jnp.mean(o * o, -1, keepdims=True) + NORM_EPS) * ng
            o_ref[h // nh, rows, (h % nh) * GDN_DV:(h % nh + 1) * GDN_DV] = (
                on * zgate[h][rows]).astype(BF16)
        state = [state[h] * jnp.exp(gtot[h][c * CHUNK:c * CHUNK + 1]) + r2[h][CHUNK:] for h in heads]
    for h in heads:
        s_ref[h] = state[h]


def _gdn(proj_big, gcol, grow, norm_g, batch, seq):
    nt = seq // GDN_TB
    nb = GDN_NB
    assert batch % nb == 0
    return pl.pallas_call(
        _gdn_kernel,
        grid=(batch // nb, nt),
        in_specs=[
            pl.BlockSpec((nb, GDN_TB, GDN_QKV), lambda g, t: (g, t, 0)),
            pl.BlockSpec((nb, GDN_TB, GDN_WIDTH), lambda g, t: (g, t, OFF_Z // GDN_WIDTH)),
            pl.BlockSpec((nb, GDN_TB, N_SMALL), lambda g, t: (g, t, 0)),
            pl.BlockSpec((nb, 1, GDN_HEADS, GDN_TB), lambda g, t: (g, t, 0, 0)),
            _resident((1, GDN_DV)),
        ],
        out_specs=pl.BlockSpec((nb, GDN_TB, GDN_WIDTH), lambda g, t: (g, t, 0)),
        out_shape=jax.ShapeDtypeStruct((batch, seq, GDN_WIDTH), BF16),
        scratch_shapes=[
            pltpu.VMEM((nb * GDN_HEADS, GDN_DK, GDN_DV), F32),
        ],
        compiler_params=pltpu.CompilerParams(
            dimension_semantics=("arbitrary", "arbitrary"), vmem_limit_bytes=VMEM_LIMIT),
        name="gdn",
    )(proj_big, proj_big, gcol, grow, norm_g)


FOX_X = 128
FOX_QC = (0, 6)
FOX_KC = (3, 9)


def _split3(c):
    hi = c.astype(BF16)
    r1 = c - hi.astype(F32)
    mid = r1.astype(BF16)
    lo = (r1 - mid.astype(F32)).astype(BF16)
    return jnp.concatenate([hi, mid, lo], axis=1)


def _select3(pair, base):
    r = lax.broadcasted_iota(jnp.int32, (3 * N_SMALL, FOX_X), 0)
    c = lax.broadcasted_iota(jnp.int32, (3 * N_SMALL, FOX_X), 1)
    d = jnp.right_shift(r, 7)
    head = jnp.bitwise_and(r, N_SMALL - 1) - (LANE_C + 2 * pair)
    e = jnp.where(head == 0, jnp.where(c == base[0] + d, 1.0, 0.0),
                  jnp.where(head == 1, jnp.where(c == base[1] + d, 1.0, 0.0), 0.0))
    return e.astype(BF16)


def _lane_ones(first):
    lane = lax.broadcasted_iota(jnp.int32, (1, FOX_X), 1)
    out = jnp.zeros((1, FOX_X), F32)
    for f in first:
        out = jnp.where(lane < f, out, jnp.where(lane < f + 3, 1.0, out))
    return out


def _fox_kernel(q_ref, k_ref, v_ref, gc_ref, ng_ref, o_ref, qa_ref, ka_ref, vt_ref, m_ref, l_ref,
                acc_ref, s_ref):
    tq = q_ref.shape[0]
    seq = k_ref.shape[0]
    i = pl.program_id(1)
    pairs = range(FOX_HEADS // 2)
    lane = lax.broadcasted_iota(jnp.int32, (1, 2 * FOX_DH), 1)
    lo_half = lane < FOX_DH
    pcols = [slice(p * 2 * FOX_DH, (p + 1) * 2 * FOX_DH) for p in pairs]

    @pl.when(i == 0)
    def _():
        ones_k = _lane_ones(FOX_QC)
        sel_k = [_select3(p, FOX_KC) for p in pairs]

        def fill(t, carry):
            rows = pl.ds(pl.multiple_of(t * tq, tq), tq)
            pieces = _split3(gc_ref[rows, :] * LOG2E)
            for p in pairs:
                ka_ref[p, rows, 0:2 * FOX_DH] = k_ref[rows, pcols[p]]
                ka_ref[p, rows, 2 * FOX_DH:] = (ones_k - _dot(pieces, sel_k[p])).astype(BF16)
                vt_ref[p, t] = jnp.transpose(v_ref[rows, pcols[p]].astype(F32)).astype(BF16)
            return carry

        lax.fori_loop(0, seq // tq, fill, 0)

    qrows = pl.ds(pl.multiple_of(i * tq, tq), tq)
    pieces_q = _split3(gc_ref[qrows, :] * LOG2E)
    ones_q = _lane_ones(FOX_KC)
    xlane = lax.broadcasted_iota(jnp.int32, (1, FOX_X), 1)
    for p in pairs:
        qp = q_ref[:, pcols[p]]
        qx = _dot(pieces_q, _select3(p, FOX_QC)) + ones_q
        for half in range(2):
            rows = slice(half * tq, (half + 1) * tq)
            mine = lo_half if half == 0 else jnp.logical_not(lo_half)
            xmine = (xlane < FOX_QC[1]) if half == 0 else (xlane >= FOX_QC[1])
            qa_ref[p, rows, 0:2 * FOX_DH] = jnp.where(mine, qp, jnp.zeros((), BF16))
            qa_ref[p, rows, 2 * FOX_DH:] = jnp.where(xmine, qx, 0.0).astype(BF16)

    ki = lax.broadcasted_iota(jnp.int32, (tq, 2 * tq), 0)
    qi = jnp.bitwise_and(lax.broadcasted_iota(jnp.int32, (tq, 2 * tq), 1), tq - 1)
    causal = ki <= qi

    def scores(j):
        krows = pl.ds(pl.multiple_of(j * tq, tq), tq)
        return [_dot_nt(ka_ref[p, krows, :], qa_ref[p]) for p in pairs]

    def absorb(s, j, first):
        for p in pairs:
            sp = s[p]
            if first:
                sp = jnp.where(causal, sp, -jnp.inf)
            m_new = jnp.max(sp, 0, keepdims=True)
            if not first:
                m_prev = m_ref[p]
                m_new = jnp.maximum(m_prev, m_new)
                scale = jnp.exp2(m_prev - m_new)
            prob = jnp.exp2(sp - m_new)
            psum = jnp.sum(prob, 0, keepdims=True)
            pv = _dot(vt_ref[p, j], prob.astype(BF16))
            m_ref[p] = m_new
            if first:
                l_ref[p] = psum
                acc_ref[p] = pv
            else:
                l_ref[p] = scale * l_ref[p] + psum
                acc_ref[p] = scale * acc_ref[p] + pv

    last = jnp.maximum(i - 1, 0)

    def stage(slot, j_next):
        s_next = scores(jnp.minimum(j_next, last))
        for p in pairs:
            s_ref[slot, p] = s_next[p]

    def consume(slot, j):
        absorb([s_ref[slot, p] for p in pairs], j, False)

    s_diag = scores(i)
    stage(0, 0)
    absorb(s_diag, i, True)

    def body(t, carry):
        stage(1, 2 * t + 1)
        consume(0, 2 * t)
        stage(0, 2 * t + 2)
        consume(1, 2 * t + 1)
        return carry

    lax.fori_loop(0, i // 2, body, 0)

    @pl.when(i % 2 == 1)
    def _():
        consume(0, i - 1)

    ng = ng_ref[...]
    for p in pairs:
        acc = acc_ref[p]
        l = l_ref[p]
        outs = []
        for half in range(2):
            oh = (acc[half * FOX_DH:(half + 1) * FOX_DH, half * tq:(half + 1) * tq]
                  / l[:, half * tq:(half + 1) * tq])
            ms = jnp.mean(oh * oh, 0, keepdims=True)
            outs.append(oh * lax.rsqrt(ms + NORM_EPS))
        o_ref[:, pcols[p]] = (jnp.transpose(jnp.concatenate(outs, axis=0)) * ng).astype(BF16)


def _fox(proj_big, gcol, norm_g2, batch, seq):
    nq = seq // FOX_TQ
    first = (GDN_QKV + GDN_WIDTH) // FOX_WIDTH
    npair = FOX_HEADS // 2
    return pl.pallas_call(
        _fox_kernel,
        grid=(batch, nq),
        in_specs=[
            pl.BlockSpec((FOX_TQ, FOX_WIDTH), lambda b, i: (b * nq + i, first)),
            pl.BlockSpec((seq, FOX_WIDTH), lambda b, i: (b, first + 1)),
            pl.BlockSpec((seq, FOX_WIDTH), lambda b, i: (b, first + 2)),
            pl.BlockSpec((seq, N_SMALL), lambda b, i: (b, 0)),
            _resident((1, 2 * FOX_DH)),
        ],
        out_specs=pl.BlockSpec((FOX_TQ, FOX_WIDTH), lambda b, i: (b * nq + i, 0)),
        out_shape=jax.ShapeDtypeStruct((batch * seq, FOX_WIDTH), BF16),
        scratch_shapes=[
            pltpu.VMEM((npair, 2 * FOX_TQ, 2 * FOX_DH + FOX_X), BF16),
            pltpu.VMEM((npair, seq, 2 * FOX_DH + FOX_X), BF16),
            pltpu.VMEM((npair, nq, 2 * FOX_DH, FOX_TQ), BF16),
            pltpu.VMEM((npair, 1, 2 * FOX_TQ), F32),
            pltpu.VMEM((npair, 1, 2 * FOX_TQ), F32),
            pltpu.VMEM((npair, 2 * FOX_DH, 2 * FOX_TQ), F32),
            pltpu.VMEM((2, npair, FOX_TQ, 2 * FOX_TQ), F32),
        ],
        compiler_params=pltpu.CompilerParams(
            dimension_semantics=("arbitrary", "arbitrary"), vmem_limit_bytes=VMEM_LIMIT),
        name="fox",
    )(proj_big, proj_big, proj_big, gcol, norm_g2)


def _tail_kernel(x_ref, p_ref, og_ref, of_ref, lin_g, lin_b, wo_ref, l1g, l1b, wu_ref, wd_ref,
                 wp_ref, wg_ref, bg_ref, l2g, l2b, o_ref):
    half = x_ref.shape[0] // 2
    ra, rb = slice(0, half), slice(half, 2 * half)

    def head(rows):
        h = _layer_norm(x_ref[rows, :], lin_g[...], lin_b[...])
        mix = (_dot(og_ref[rows, :], wo_ref[0:GDN_WIDTH, :])
               + _dot(of_ref[rows, :], wo_ref[GDN_WIDTH:, :]))
        return _layer_norm(ALPHA * h + mix, l1g[...], l1b[...])

    def mlp(rows, h1):
        h1b = h1.astype(BF16)
        gate = _sigmoid(_dot(h1b, wg_ref[...]) + bg_ref[...])
        acc = ALPHA * h1 + _dot(p_ref[rows, :].astype(BF16), wp_ref[...]) * gate
        for j in range(D_FF // TAIL_TF):
            cols = slice(j * TAIL_TF, (j + 1) * TAIL_TF)
            a = jnp.maximum(_dot(h1b, wu_ref[:, cols]), 0.0)
            acc = acc + _dot((a * a).astype(BF16), wd_ref[cols, :])
        return acc

    h1a = head(ra)
    h1b_ = head(rb)
    acc_a = mlp(ra, h1a)
    o_ref[ra, :] = _layer_norm(acc_a, l2g[...], l2b[...])
    acc_b = mlp(rb, h1b_)
    o_ref[rb, :] = _layer_norm(acc_b, l2g[...], l2b[...])


def _tail(x2, p2, o_gdn, o_fox, lin_g, lin_b, w_out, l1g, l1b, w_up, w_down, w_ple, w_gate,
          b_gate, l2g, l2b):
    n = x2.shape[0]
    row = lambda width: pl.BlockSpec((TAIL_TM, width), lambda i: (i, 0))
    vec = _resident((1, D_MODEL))
    return pl.pallas_call(
        _tail_kernel,
        grid=(n // TAIL_TM,),
        in_specs=[
            row(D_MODEL), row(D_PLE), row(GDN_WIDTH), row(FOX_WIDTH),
            vec, vec, _resident((D_MODEL, D_MODEL)), vec, vec,
            _resident((D_MODEL, D_FF)), _resident((D_FF, D_MODEL)),
            _resident((D_PLE, D_MODEL)), _resident((D_MODEL, D_MODEL)), vec, vec, vec,
        ],
        out_specs=row(D_MODEL),
        out_shape=jax.ShapeDtypeStruct((n, D_MODEL), F32),
        compiler_params=pltpu.CompilerParams(
            dimension_semantics=("arbitrary",), vmem_limit_bytes=VMEM_LIMIT),
        name="tail",
    )(x2, p2, o_gdn, o_fox, lin_g, lin_b, w_out, l1g, l1b, w_up, w_down, w_ple, w_gate,
      b_gate, l2g, l2b)


def kernel(x, p, ln_in_g, ln_in_b, w_in, conv_w, a_log, dt_bias, gdn_norm_g, b_f, fox_norm_g,
           w_out, ln1_g, ln1_b, w_up, w_down, w_ple, w_ple_gate, b_ple_gate, ln2_g, ln2_b):
    batch, seq, _ = x.shape
    assert x.shape[2] == D_MODEL and w_in.shape[0] == 1
    assert seq % FOX_TQ == 0 and seq % GDN_TB == 0 and (batch * seq) % PROJ_TM == 0
    n = batch * seq
    x2 = x.reshape(n, D_MODEL)
    p2 = p[0].reshape(n, D_PLE)
    row = lambda a: a.reshape(1, -1).astype(F32)

    w0 = w_in[0]
    w_big = jnp.concatenate([w0[:, :OFF_BETA], w0[:, OFF_FOX:OFF_F]], axis=1).astype(BF16)
    w_decay = w0[:, OFF_BETA + GDN_HEADS:OFF_FOX]
    n_gate = LANE_GTOT + GDN_HEADS
    w_small = jnp.concatenate(
        [w0[:, OFF_BETA:OFF_FOX], w0[:, OFF_F:], w_decay, jnp.zeros((D_MODEL, N_SMALL - n_gate), F32)],
        axis=1).astype(BF16)
    zeros4 = jnp.zeros((GDN_HEADS,), F32)
    pad = jnp.zeros((N_SMALL - n_gate,), F32)
    pvec = jnp.zeros((8, N_SMALL), F32)
    pvec = pvec.at[0].set(jnp.concatenate([zeros4, dt_bias[0], b_f[0], dt_bias[0], pad]))
    pvec = pvec.at[1].set(
        jnp.concatenate([zeros4, a_log[0], jnp.zeros((FOX_HEADS,), F32), a_log[0], pad]))

    proj_big, small = _proj(x2, row(ln_in_g), row(ln_in_b), w_big, w_small, conv_w[0], seq)
    gcol, grow = _gates(small, pvec, batch, seq)

    o_gdn = _gdn(proj_big.reshape(batch, seq, N_BIG), gcol.reshape(batch, seq, N_SMALL), grow,
                 row(gdn_norm_g[0]), batch, seq).reshape(n, GDN_WIDTH)
    o_fox = _fox(proj_big, gcol, row(jnp.tile(fox_norm_g[0], 2)), batch, seq)

    out = _tail(x2, p2, o_gdn, o_fox, row(ln_in_g), row(ln_in_b), w_out[0].astype(BF16),
                row(ln1_g[0]), row(ln1_b[0]), w_up[0].astype(BF16), w_down[0].astype(BF16),
                w_ple[0].astype(BF16), w_ple_gate[0].astype(BF16), row(b_ple_gate[0]),
                row(ln2_g[0]), row(ln2_b[0]))
    return out.reshape(batch, seq, D_MODEL)
```

```python
import functools

import jax
import jax.numpy as jnp
import numpy as np
from jax import lax
from jax.experimental import pallas as pl
from jax.experimental.pallas import tpu as pltpu

F32 = jnp.float32
BF16 = jnp.bfloat16

D_MODEL = 1024
CHUNK = 64
GDN_HEADS = 4
GDN_DK = 128
GDN_DV = 128
GDN_QK = GDN_HEADS * GDN_DK
GDN_QKV = 3 * GDN_QK
GDN_WIDTH = GDN_HEADS * GDN_DV
FOX_HEADS = 8
FOX_DH = 64
FOX_WIDTH = FOX_HEADS * FOX_DH
CONV_W = 4
D_FF = 4 * D_MODEL
D_PLE = 256
LN_EPS = 1e-5
NORM_EPS = 1e-6
ALPHA = 2.0 ** 0.25

OFF_Z = GDN_QKV
OFF_BETA = OFF_Z + GDN_WIDTH
OFF_FOX = OFF_BETA + 2 * GDN_HEADS
OFF_F = OFF_FOX + 3 * FOX_WIDTH
N_BIG = GDN_QKV + GDN_WIDTH + 3 * FOX_WIDTH
N_SMALL = 128
FOX_Q0 = GDN_QKV + GDN_WIDTH
LOG2E = 1.4426950408889634
FOX_QSCALE = FOX_DH ** -0.5 * LOG2E
LANE_BETA = 0
LANE_GAM = GDN_HEADS
LANE_C = 2 * GDN_HEADS
LANE_GTOT = LANE_C + FOX_HEADS

VMEM_LIMIT = 56 * 1024 * 1024

PROJ_TM = 512
PROJ_TN = 256
PROJ_AHEAD = 1
GATE_TB = 256
GDN_TB = 256
GDN_NB = 4
FOX_TQ = 256
TAIL_TM = 512
TAIL_PART = 256
TAIL_TF = 1024


def _layer_norm(x, g, b):
    mu = jnp.mean(x, -1, keepdims=True)
    xc = x - mu
    var = jnp.mean(xc * xc, -1, keepdims=True)
    return xc * lax.rsqrt(var + LN_EPS) * g + b


def _softplus(x):
    return jnp.maximum(x, 0.0) + jnp.log(1.0 + jnp.exp(-jnp.abs(x)))


def _sigmoid(x):
    return 1.0 / (1.0 + jnp.exp(-x))


def _dot(a, b):
    return jnp.dot(a, b, preferred_element_type=F32)


def _dot_nt(a, b):
    return lax.dot_general(a, b, (((1,), (1,)), ((), ())), preferred_element_type=F32)


def _resident(shape):
    return pl.BlockSpec(shape, lambda *_: (0,) * len(shape), pipeline_mode=pl.Buffered(1))


def _proj_kernel(x_ref, g_ref, b_ref, wb_ref, ws_ref, cw_ref, ob_ref, os_ref, halo_ref, *,
                 tiles_per_seq):
    tm = x_ref.shape[0]
    halo = halo_ref.shape[0]

    @pl.when(pl.program_id(0) % tiles_per_seq == 0)
    def _():
        halo_ref[...] = jnp.zeros_like(halo_ref)

    h = _layer_norm(x_ref[...], g_ref[...], b_ref[...]).astype(BF16)
    nchunk = N_BIG // PROJ_TN
    nconv = GDN_QKV // PROJ_TN
    order = [c for pair in zip(range(nconv), range(nconv, 2 * nconv)) for c in pair]
    order += list(range(2 * nconv, nchunk))
    def chunk_dot(c):
        return _dot(h, wb_ref[:, c * PROJ_TN:(c + 1) * PROJ_TN])

    inflight = [chunk_dot(c) for c in order[:PROJ_AHEAD]]
    for pos, j in enumerate(order):
        cols = slice(j * PROJ_TN, (j + 1) * PROJ_TN)
        acc = inflight.pop(0)
        if pos + PROJ_AHEAD < nchunk:
            inflight.append(chunk_dot(order[pos + PROJ_AHEAD]))
        if (j + 1) * PROJ_TN <= GDN_QKV:
            ext = jnp.concatenate([halo_ref[:, cols], acc], axis=0)
            y = acc * cw_ref[CONV_W - 1:CONV_W, cols]
            for d in range(1, CONV_W):
                y = y + pltpu.roll(ext, d, 0)[halo:] * cw_ref[CONV_W - 1 - d:CONV_W - d, cols]
            halo_ref[:, cols] = acc[tm - halo:]
            acc = y * _sigmoid(y)
        elif FOX_Q0 <= j * PROJ_TN < FOX_Q0 + FOX_WIDTH:
            acc = acc * FOX_QSCALE
        ob_ref[:, cols] = acc.astype(BF16)
    os_ref[...] = _dot(h, ws_ref[...])


def _proj(x2, ln_g, ln_b, w_big, w_small, conv_w, seq):
    n = x2.shape[0]
    assert GDN_QKV % PROJ_TN == 0 and seq % PROJ_TM == 0
    return pl.pallas_call(
        functools.partial(_proj_kernel, tiles_per_seq=seq // PROJ_TM),
        grid=(n // PROJ_TM,),
        in_specs=[
            pl.BlockSpec((PROJ_TM, D_MODEL), lambda i: (i, 0)),
            _resident((1, D_MODEL)),
            _resident((1, D_MODEL)),
            _resident((D_MODEL, N_BIG)),
            _resident((D_MODEL, N_SMALL)),
            _resident((CONV_W, GDN_QKV)),
        ],
        out_specs=[
            pl.BlockSpec((PROJ_TM, N_BIG), lambda i: (i, 0)),
            pl.BlockSpec((PROJ_TM, N_SMALL), lambda i: (i, 0)),
        ],
        out_shape=[
            jax.ShapeDtypeStruct((n, N_BIG), BF16),
            jax.ShapeDtypeStruct((n, N_SMALL), F32),
        ],
        scratch_shapes=[pltpu.VMEM((8, GDN_QKV), F32)],
        compiler_params=pltpu.CompilerParams(
            dimension_semantics=("arbitrary",), vmem_limit_bytes=VMEM_LIMIT),
        name="proj",
    )(x2, ln_g, ln_b, w_big, w_small, conv_w)


def _gates_kernel(s_ref, pv_ref, o_ref, gr_ref, *, seq):
    lane = lax.broadcasted_iota(jnp.int32, (1, N_SMALL), 1)
    bias = pv_ref[0:1, :]
    neg_a = -jnp.exp(pv_ref[1:2, :])
    r = lax.broadcasted_iota(jnp.int32, (GATE_TB, GATE_TB), 0)
    c = lax.broadcasted_iota(jnp.int32, (GATE_TB, GATE_TB), 1)
    l_full = jnp.where(r >= c, 1.0, 0.0).astype(F32)
    same_chunk = jnp.right_shift(r, 6) == jnp.right_shift(c, 6)
    l_chunk = jnp.where(same_chunk, l_full, 0.0)
    l_total = jnp.where(same_chunk, 1.0, 0.0).astype(F32)
    l_decay = jnp.concatenate([l_chunk, l_total], axis=0).astype(BF16)
    l_forget = l_full.astype(BF16)

    def pieces_sum(a):
        return a[:, 0:N_SMALL] + a[:, N_SMALL:2 * N_SMALL] + a[:, 2 * N_SMALL:]

    carry = jnp.zeros((1, N_SMALL), F32)
    for t in range(seq // GATE_TB):
        rows = slice(t * GATE_TB, (t + 1) * GATE_TB)
        x = s_ref[rows, :] + bias
        beta = _sigmoid(x)
        log_g = neg_a * _softplus(x)
        log_f = -_softplus(-x)
        decay = pieces_sum(_dot(l_decay, _split3(log_g)))
        gam, gtot = decay[:GATE_TB], decay[GATE_TB:]
        cum = pieces_sum(_dot(l_forget, _split3(log_f))) + carry
        carry = cum[GATE_TB - 1:GATE_TB, :]
        o_ref[rows, :] = jnp.where(
            lane < LANE_GAM, beta,
            jnp.where(lane < LANE_C, gam, jnp.where(lane < LANE_GTOT, cum, gtot)))
        gr_ref[0, t] = jnp.transpose(gam)[LANE_GAM:LANE_GAM + GDN_HEADS, :]


def _gates(small, pvec, batch, seq):
    assert GATE_TB == GDN_TB
    nt = seq // GATE_TB
    return pl.pallas_call(
        functools.partial(_gates_kernel, seq=seq),
        grid=(batch,),
        in_specs=[
            pl.BlockSpec((seq, N_SMALL), lambda b: (b, 0)),
            _resident((8, N_SMALL)),
        ],
        out_specs=[
            pl.BlockSpec((seq, N_SMALL), lambda b: (b, 0)),
            pl.BlockSpec((1, nt, GDN_HEADS, GATE_TB), lambda b: (b, 0, 0, 0)),
        ],
        out_shape=[
            jax.ShapeDtypeStruct((batch * seq, N_SMALL), F32),
            jax.ShapeDtypeStruct((batch, nt, GDN_HEADS, GATE_TB), F32),
        ],
        compiler_params=pltpu.CompilerParams(
            dimension_semantics=("arbitrary",), vmem_limit_bytes=VMEM_LIMIT),
        name="gates",
    )(small, pvec)


def _gdn_kernel(x_ref, z_ref, gc_ref, gr_ref, ng_ref, o_ref, s_ref):
    nb, tb = x_ref.shape[0], x_ref.shape[1]
    nch = tb // CHUNK
    heads = range(nb * GDN_HEADS)
    nh = GDN_HEADS

    @pl.when(pl.program_id(1) == 0)
    def _():
        s_ref[...] = jnp.zeros_like(s_ref)

    def act(h, c0):
        c0 = c0 + (h % nh) * GDN_DK
        return x_ref[h // nh, :, c0:c0 + GDN_DK].astype(F32)

    def l2norm(v):
        return v * lax.rsqrt(jnp.sum(v * v, -1, keepdims=True) + NORM_EPS)

    ri = lax.broadcasted_iota(jnp.int32, (tb, tb), 0)
    ci = lax.broadcasted_iota(jnp.int32, (tb, tb), 1)
    same = jnp.right_shift(ri, 6) == jnp.right_shift(ci, 6)
    keep = jnp.logical_and(same, ri >= ci)
    diag = ri == ci
    pr = lax.broadcasted_iota(jnp.int32, (CHUNK, tb), 0)
    pc = lax.broadcasted_iota(jnp.int32, (CHUNK, tb), 1)
    eye_packed = jnp.where(jnp.bitwise_and(pc, CHUNK - 1) == pr, 1.0, 0.0).astype(F32)
    lane_chunk = jnp.right_shift(lax.broadcasted_iota(jnp.int32, (1, tb), 1), 6)
    row_chunk = jnp.right_shift(lax.broadcasted_iota(jnp.int32, (tb, 1), 0), 6)

    def to_bd(packed):
        return jnp.where(same, jnp.concatenate([packed] * nch, axis=0), jnp.zeros((), packed.dtype))

    def to_packed(bd):
        out = bd[0:CHUNK]
        for c in range(1, nch):
            out = out + bd[c * CHUNK:(c + 1) * CHUNK]
        return out

    ng = ng_ref[...]
    q = [l2norm(act(h, 0)) * (GDN_DK ** -0.5) for h in heads]
    k = [l2norm(act(h, GDN_QK)) for h in heads]
    v = [act(h, 2 * GDN_QK) for h in heads]

    def gate(h, lane0):
        return gc_ref[h // nh, :, lane0 + h % nh:lane0 + h % nh + 1]

    beta = [gate(h, LANE_BETA) for h in heads]
    gam = [gate(h, LANE_GAM) for h in heads]
    gtot = [gate(h, LANE_GTOT) for h in heads]
    grow = [gr_ref[h // nh, 0, h % nh:h % nh + 1, :] for h in heads]
    egam = [jnp.exp(g) for g in gam]
    kb = [a.astype(BF16) for a in k]
    decay = [jnp.exp(jnp.where(keep, gam[h] - grow[h], -jnp.inf)) for h in heads]
    gram = [_dot_nt(kb[h], kb[h]) for h in heads]
    qk = [(_dot_nt(q[h].astype(BF16), kb[h]) * decay[h]).astype(BF16) for h in heads]

    x_bd = [jnp.where(diag, 0.0, -(gram[h] * beta[h]) * decay[h]) for h in heads]
    x_p = [to_packed(a) for a in x_bd]
    p_p = [eye_packed + a for a in x_p]
    x_p = [_dot(x_p[h].astype(BF16), x_bd[h].astype(BF16)) for h in heads]
    for _ in range(4):
        w_bd = [to_bd(a.astype(BF16)) for a in x_p]
        r = [_dot(jnp.concatenate([p_p[h], x_p[h]], axis=0).astype(BF16), w_bd[h]) for h in heads]
        p_p = [p_p[h] + r[h][:CHUNK] for h in heads]
        x_p = [r[h][CHUNK:] for h in heads]
    p_p = [p_p[h] + _dot(p_p[h].astype(BF16), to_bd(x_p[h].astype(BF16))) for h in heads]

    rhs = [jnp.concatenate([v[h] * beta[h], k[h] * (beta[h] * egam[h])], axis=1).astype(BF16)
           for h in heads]
    sol = [_dot(to_bd(p_p[h].astype(BF16)), rhs[h]) for h in heads]
    u = [a[:, :GDN_DV] for a in sol]
    w = [a[:, GDN_DV:].astype(BF16) for a in sol]
    qd = [(q[h] * egam[h]).astype(BF16) for h in heads]
    kd_t = [jnp.transpose(k[h] * jnp.exp(gtot[h] - gam[h])).astype(BF16) for h in heads]
    z = [z_ref[h // nh, :, (h % nh) * GDN_DV:(h % nh + 1) * GDN_DV].astype(F32) for h in heads]
    zgate = [a * _sigmoid(a) for a in z]

    state = [s_ref[h] for h in heads]
    for c in range(nch):
        rows = slice(c * CHUNK, (c + 1) * CHUNK)
        sb = [a.astype(BF16) for a in state]
        r1 = [_dot(jnp.concatenate([w[h][rows], qd[h][rows]], axis=0), sb[h]) for h in heads]
        v_new = [u[h][rows] - r1[h][:CHUNK] for h in heads]
        v_full = [jnp.where(row_chunk == c, jnp.concatenate([a.astype(BF16)] * nch, axis=0),
                            jnp.zeros((), BF16)) for a in v_new]
        kd_c = [jnp.where(lane_chunk == c, a, jnp.zeros((), BF16)) for a in kd_t]
        r2 = [_dot(jnp.concatenate([qk[h][rows], kd_c[h]], axis=0), v_full[h]) for h in heads]
        for h in heads:
            o = r1[h][CHUNK:] + r2[h][:CHUNK]
            on = o * lax.rsqrt(jnp.mean(o * o, -1, keepdims=True) + NORM_EPS) * ng
            o_ref[h // nh, rows, (h % nh) * GDN_DV:(h % nh + 1) * GDN_DV] = (
                on * zgate[h][rows]).astype(BF16)
        state = [state[h] * jnp.exp(gtot[h][c * CHUNK:c * CHUNK + 1]) + r2[h][CHUNK:] for h in heads]
    for h in heads:
        s_ref[h] = state[h]


def _gdn(proj_big, gcol, grow, norm_g, batch, seq):
    nt = seq // GDN_TB
    nb = GDN_NB
    assert batch % nb == 0
    return pl.pallas_call(
        _gdn_kernel,
        grid=(batch // nb, nt),
        in_specs=[
            pl.BlockSpec((nb, GDN_TB, GDN_QKV), lambda g, t: (g, t, 0)),
            pl.BlockSpec((nb, GDN_TB, GDN_WIDTH), lambda g, t: (g, t, OFF_Z // GDN_WIDTH)),
            pl.BlockSpec((nb, GDN_TB, N_SMALL), lambda g, t: (g, t, 0)),
            pl.BlockSpec((nb, 1, GDN_HEADS, GDN_TB), lambda g, t: (g, t, 0, 0)),
            _resident((1, GDN_DV)),
        ],
        out_specs=pl.BlockSpec((nb, GDN_TB, GDN_WIDTH), lambda g, t: (g, t, 0)),
        out_shape=jax.ShapeDtypeStruct((batch, seq, GDN_WIDTH), BF16),
        scratch_shapes=[
            pltpu.VMEM((nb * GDN_HEADS, GDN_DK, GDN_DV), F32),
        ],
        compiler_params=pltpu.CompilerParams(
            dimension_semantics=("arbitrary", "arbitrary"), vmem_limit_bytes=VMEM_LIMIT),
        name="gdn",
    )(proj_big, proj_big, gcol, grow, norm_g)


FOX_X = 128
FOX_QC = (0, 6)
FOX_KC = (3, 9)


def _split3(c):
    hi = c.astype(BF16)
    r1 = c - hi.astype(F32)
    mid = r1.astype(BF16)
    lo = (r1 - mid.astype(F32)).astype(BF16)
    return jnp.concatenate([hi, mid, lo], axis=1)


def _select3(pair, base):
    e = np.zeros((3 * N_SMALL, FOX_X), np.float32)
    for d in range(3):
        for head in range(2):
            e[d * N_SMALL + LANE_C + 2 * pair + head, base[head] + d] = 1.0
    return e


def _lane_ones(first):
    out = np.zeros((FOX_X,), np.float32)
    for f in first:
        out[f:f + 3] = 1.0
    return out


def _fox_constants():
    npair = FOX_HEADS // 2
    sel = np.stack([np.stack([_select3(p, base) for p in range(npair)])
                    for base in (FOX_KC, FOX_QC)])
    ones = np.stack([_lane_ones(FOX_QC), _lane_ones(FOX_KC)])
    return jnp.asarray(sel, BF16), jnp.asarray(ones, F32)


def _fox_kernel(q_ref, k_ref, v_ref, gc_ref, ng_ref, sel_ref, ones_ref, o_ref, qa_ref, ka_ref, vt_ref,
                m_ref, l_ref, acc_ref, s_ref):
    tq = q_ref.shape[0]
    seq = k_ref.shape[0]
    i = pl.program_id(1)
    pairs = range(FOX_HEADS // 2)
    lane = lax.broadcasted_iota(jnp.int32, (1, 2 * FOX_DH), 1)
    lo_half = lane < FOX_DH
    pcols = [slice(p * 2 * FOX_DH, (p + 1) * 2 * FOX_DH) for p in pairs]

    @pl.when(i == 0)
    def _():
        ones_k = ones_ref[0:1, :]

        def fill(t, carry):
            rows = pl.ds(pl.multiple_of(t * tq, tq), tq)
            pieces = _split3(gc_ref[rows, :] * LOG2E)
            for p in pairs:
                ka_ref[p, rows, 0:2 * FOX_DH] = k_ref[rows, pcols[p]]
                ka_ref[p, rows, 2 * FOX_DH:] = (ones_k - _dot(pieces, sel_ref[0, p])).astype(BF16)
                vt_ref[p, t] = jnp.transpose(v_ref[rows, pcols[p]].astype(F32)).astype(BF16)
            return carry

        lax.fori_loop(0, seq // tq, fill, 0)

    qrows = pl.ds(pl.multiple_of(i * tq, tq), tq)
    pieces_q = _split3(gc_ref[qrows, :] * LOG2E)
    ones_q = ones_ref[1:2, :]
    xlane = lax.broadcasted_iota(jnp.int32, (1, FOX_X), 1)
    for p in pairs:
        qp = q_ref[:, pcols[p]]
        qx = _dot(pieces_q, sel_ref[1, p]) + ones_q
        for half in range(2):
            rows = slice(half * tq, (half + 1) * tq)
            mine = lo_half if half == 0 else jnp.logical_not(lo_half)
            xmine = (xlane < FOX_QC[1]) if half == 0 else (xlane >= FOX_QC[1])
            qa_ref[p, rows, 0:2 * FOX_DH] = jnp.where(mine, qp, jnp.zeros((), BF16))
            qa_ref[p, rows, 2 * FOX_DH:] = jnp.where(xmine, qx, 0.0).astype(BF16)

    ki = lax.broadcasted_iota(jnp.int32, (tq, 2 * tq), 0)
    qi = jnp.bitwise_and(lax.broadcasted_iota(jnp.int32, (tq, 2 * tq), 1), tq - 1)
    causal = ki <= qi

    def scores(j):
        krows = pl.ds(pl.multiple_of(j * tq, tq), tq)
        return [_dot_nt(ka_ref[p, krows, :], qa_ref[p]) for p in pairs]

    def absorb(s, j, first):
        for p in pairs:
            sp = s[p]
            if first:
                sp = jnp.where(causal, sp, -jnp.inf)
            m_new = jnp.max(sp, 0, keepdims=True)
            if not first:
                m_prev = m_ref[p]
                m_new = jnp.maximum(m_prev, m_new)
                scale = jnp.exp2(m_prev - m_new)
            prob = jnp.exp2(sp - m_new)
            psum = jnp.sum(prob, 0, keepdims=True)
            pv = _dot(vt_ref[p, j], prob.astype(BF16))
            m_ref[p] = m_new
            if first:
                l_ref[p] = psum
                acc_ref[p] = pv
            else:
                l_ref[p] = scale * l_ref[p] + psum
                acc_ref[p] = scale * acc_ref[p] + pv

    last = jnp.maximum(i - 1, 0)

    def stage(slot, j_next):
        s_next = scores(jnp.minimum(j_next, last))
        for p in pairs:
            s_ref[slot, p] = s_next[p]

    def consume(slot, j):
        absorb([s_ref[slot, p] for p in pairs], j, False)

    s_diag = scores(i)
    stage(0, 0)
    absorb(s_diag, i, True)

    def body(t, carry):
        stage(1, 2 * t + 1)
        consume(0, 2 * t)
        stage(0, 2 * t + 2)
        consume(1, 2 * t + 1)
        return carry

    lax.fori_loop(0, i // 2, body, 0)

    @pl.when(i % 2 == 1)
    def _():
        consume(0, i - 1)

    ng = ng_ref[...]
    for p in pairs:
        acc = acc_ref[p]
        l = l_ref[p]
        outs = []
        for half in range(2):
            oh = (acc[half * FOX_DH:(half + 1) * FOX_DH, half * tq:(half + 1) * tq]
                  / l[:, half * tq:(half + 1) * tq])
            ms = jnp.mean(oh * oh, 0, keepdims=True)
            outs.append(oh * lax.rsqrt(ms + NORM_EPS))
        o_ref[:, pcols[p]] = (jnp.transpose(jnp.concatenate(outs, axis=0)) * ng).astype(BF16)


def _fox(proj_big, gcol, norm_g2, batch, seq):
    nq = seq // FOX_TQ
    first = (GDN_QKV + GDN_WIDTH) // FOX_WIDTH
    npair = FOX_HEADS // 2
    return pl.pallas_call(
        _fox_kernel,
        grid=(batch, nq),
        in_specs=[
            pl.BlockSpec((FOX_TQ, FOX_WIDTH), lambda b, i: (b * nq + i, first)),
            pl.BlockSpec((seq, FOX_WIDTH), lambda b, i: (b, first + 1)),
            pl.BlockSpec((seq, FOX_WIDTH), lambda b, i: (b, first + 2)),
            pl.BlockSpec((seq, N_SMALL), lambda b, i: (b, 0)),
            _resident((1, 2 * FOX_DH)),
            _resident((2, npair, 3 * N_SMALL, FOX_X)),
            _resident((2, FOX_X)),
        ],
        out_specs=pl.BlockSpec((FOX_TQ, FOX_WIDTH), lambda b, i: (b * nq + i, 0)),
        out_shape=jax.ShapeDtypeStruct((batch * seq, FOX_WIDTH), BF16),
        scratch_shapes=[
            pltpu.VMEM((npair, 2 * FOX_TQ, 2 * FOX_DH + FOX_X), BF16),
            pltpu.VMEM((npair, seq, 2 * FOX_DH + FOX_X), BF16),
            pltpu.VMEM((npair, nq, 2 * FOX_DH, FOX_TQ), BF16),
            pltpu.VMEM((npair, 1, 2 * FOX_TQ), F32),
            pltpu.VMEM((npair, 1, 2 * FOX_TQ), F32),
            pltpu.VMEM((npair, 2 * FOX_DH, 2 * FOX_TQ), F32),
            pltpu.VMEM((2, npair, FOX_TQ, 2 * FOX_TQ), F32),
        ],
        compiler_params=pltpu.CompilerParams(
            dimension_semantics=("arbitrary", "arbitrary"), vmem_limit_bytes=VMEM_LIMIT),
        name="fox",
    )(proj_big, proj_big, proj_big, gcol, norm_g2, *_fox_constants())


def _tail_kernel(x_ref, p_ref, og_ref, of_ref, lin_g, lin_b, wo_ref, l1g, l1b, wu_ref, wd_ref,
                 wp_ref, wg_ref, bg_ref, l2g, l2b, o_ref):
    nparts = x_ref.shape[0] // TAIL_PART
    parts = [slice(r * TAIL_PART, (r + 1) * TAIL_PART) for r in range(nparts)]

    def head(rows):
        h = _layer_norm(x_ref[rows, :], lin_g[...], lin_b[...])
        mix = (_dot(og_ref[rows, :], wo_ref[0:GDN_WIDTH, :])
               + _dot(of_ref[rows, :], wo_ref[GDN_WIDTH:, :]))
        return _layer_norm(ALPHA * h + mix, l1g[...], l1b[...])

    def mlp(rows, h1):
        h1b = h1.astype(BF16)
        gate = _sigmoid(_dot(h1b, wg_ref[...]) + bg_ref[...])
        acc = ALPHA * h1 + _dot(p_ref[rows, :].astype(BF16), wp_ref[...]) * gate
        for j in range(D_FF // TAIL_TF):
            cols = slice(j * TAIL_TF, (j + 1) * TAIL_TF)
            a = jnp.maximum(_dot(h1b, wu_ref[:, cols]), 0.0)
            acc = acc + _dot((a * a).astype(BF16), wd_ref[cols, :])
        return acc

    h1 = head(parts[0])
    for r in range(nparts):
        h1_next = head(parts[r + 1]) if r + 1 < nparts else None
        o_ref[parts[r], :] = _layer_norm(mlp(parts[r], h1), l2g[...], l2b[...])
        h1 = h1_next


def _tail(x2, p2, o_gdn, o_fox, lin_g, lin_b, w_out, l1g, l1b, w_up, w_down, w_ple, w_gate,
          b_gate, l2g, l2b):
    n = x2.shape[0]
    row = lambda width: pl.BlockSpec((TAIL_TM, width), lambda i: (i, 0))
    vec = _resident((1, D_MODEL))
    return pl.pallas_call(
        _tail_kernel,
        grid=(n // TAIL_TM,),
        in_specs=[
            row(D_MODEL), row(D_PLE), row(GDN_WIDTH), row(FOX_WIDTH),
            vec, vec, _resident((D_MODEL, D_MODEL)), vec, vec,
            _resident((D_MODEL, D_FF)), _resident((D_FF, D_MODEL)),
            _resident((D_PLE, D_MODEL)), _resident((D_MODEL, D_MODEL)), vec, vec, vec,
        ],
        out_specs=row(D_MODEL),
        out_shape=jax.ShapeDtypeStruct((n, D_MODEL), F32),
        compiler_params=pltpu.CompilerParams(
            dimension_semantics=("arbitrary",), vmem_limit_bytes=VMEM_LIMIT),
        name="tail",
    )(x2, p2, o_gdn, o_fox, lin_g, lin_b, w_out, l1g, l1b, w_up, w_down, w_ple, w_gate,
      b_gate, l2g, l2b)


def kernel(x, p, ln_in_g, ln_in_b, w_in, conv_w, a_log, dt_bias, gdn_norm_g, b_f, fox_norm_g,
           w_out, ln1_g, ln1_b, w_up, w_down, w_ple, w_ple_gate, b_ple_gate, ln2_g, ln2_b):
    batch, seq, _ = x.shape
    assert x.shape[2] == D_MODEL and w_in.shape[0] == 1
    assert seq % FOX_TQ == 0 and seq % GDN_TB == 0 and (batch * seq) % PROJ_TM == 0
    n = batch * seq
    x2 = x.reshape(n, D_MODEL)
    p2 = p[0].reshape(n, D_PLE)
    row = lambda a: a.reshape(1, -1).astype(F32)

    w0 = w_in[0]
    w_big = jnp.concatenate([w0[:, :OFF_BETA], w0[:, OFF_FOX:OFF_F]], axis=1).astype(BF16)
    w_decay = w0[:, OFF_BETA + GDN_HEADS:OFF_FOX]
    n_gate = LANE_GTOT + GDN_HEADS
    w_small = jnp.concatenate(
        [w0[:, OFF_BETA:OFF_FOX], w0[:, OFF_F:], w_decay, jnp.zeros((D_MODEL, N_SMALL - n_gate), F32)],
        axis=1).astype(BF16)
    zeros4 = jnp.zeros((GDN_HEADS,), F32)
    pad = jnp.zeros((N_SMALL - n_gate,), F32)
    pvec = jnp.zeros((8, N_SMALL), F32)
    pvec = pvec.at[0].set(jnp.concatenate([zeros4, dt_bias[0], b_f[0], dt_bias[0], pad]))
    pvec = pvec.at[1].set(
        jnp.concatenate([zeros4, a_log[0], jnp.zeros((FOX_HEADS,), F32), a_log[0], pad]))

    proj_big, small = _proj(x2, row(ln_in_g), row(ln_in_b), w_big, w_small, conv_w[0], seq)
    gcol, grow = _gates(small, pvec, batch, seq)

    o_gdn = _gdn(proj_big.reshape(batch, seq, N_BIG), gcol.reshape(batch, seq, N_SMALL), grow,
                 row(gdn_norm_g[0]), batch, seq).reshape(n, GDN_WIDTH)
    o_fox = _fox(proj_big, gcol, row(jnp.tile(fox_norm_g[0], 2)), batch, seq)

    out = _tail(x2, p2, o_gdn, o_fox, row(ln_in_g), row(ln_in_b), w_out[0].astype(BF16),
                row(ln1_g[0]), row(ln1_b[0]), w_up[0].astype(BF16), w_down[0].astype(BF16),
                w_ple[0].astype(BF16), w_ple_gate[0].astype(BF16), row(b_ple_gate[0]),
                row(ln2_g[0]), row(ln2_b[0]))
    return out.reshape(batch, seq, D_MODEL)
```

```python
import functools

import jax
import jax.numpy as jnp
import numpy as np
from jax import lax
from jax.experimental import pallas as pl
from jax.experimental.pallas import tpu as pltpu

F32 = jnp.float32
BF16 = jnp.bfloat16

D_MODEL = 1024
CHUNK = 64
GDN_HEADS = 4
GDN_DK = 128
GDN_DV = 128
GDN_QK = GDN_HEADS * GDN_DK
GDN_QKV = 3 * GDN_QK
GDN_WIDTH = GDN_HEADS * GDN_DV
FOX_HEADS = 8
FOX_DH = 64
FOX_WIDTH = FOX_HEADS * FOX_DH
CONV_W = 4
D_FF = 4 * D_MODEL
D_PLE = 256
LN_EPS = 1e-5
NORM_EPS = 1e-6
ALPHA = 2.0 ** 0.25

OFF_Z = GDN_QKV
OFF_BETA = OFF_Z + GDN_WIDTH
OFF_FOX = OFF_BETA + 2 * GDN_HEADS
OFF_F = OFF_FOX + 3 * FOX_WIDTH
N_BIG = GDN_QKV + GDN_WIDTH + 3 * FOX_WIDTH
N_SMALL = 128
FOX_Q0 = GDN_QKV + GDN_WIDTH
LOG2E = 1.4426950408889634
FOX_QSCALE = FOX_DH ** -0.5 * LOG2E
LANE_BETA = 0
LANE_GAM = GDN_HEADS
LANE_C = 2 * GDN_HEADS
LANE_GTOT = LANE_C + FOX_HEADS

VMEM_LIMIT = 56 * 1024 * 1024

PROJ_TM = 512
PROJ_TN = 256
PROJ_AHEAD = 1
GATE_TB = 256
GDN_TB = 256
GDN_NB = 4
FOX_TQ = 256
FOX_NB = 2
TAIL_TM = 512
TAIL_PART = 256
TAIL_TF = 1024


def _layer_norm(x, g, b):
    mu = jnp.mean(x, -1, keepdims=True)
    xc = x - mu
    var = jnp.mean(xc * xc, -1, keepdims=True)
    return xc * lax.rsqrt(var + LN_EPS) * g + b


def _softplus(x):
    return jnp.maximum(x, 0.0) + jnp.log(1.0 + jnp.exp(-jnp.abs(x)))


def _sigmoid(x):
    return 1.0 / (1.0 + jnp.exp(-x))


def _dot(a, b):
    return jnp.dot(a, b, preferred_element_type=F32)


def _dot_nt(a, b):
    return lax.dot_general(a, b, (((1,), (1,)), ((), ())), preferred_element_type=F32)


def _resident(shape):
    return pl.BlockSpec(shape, lambda *_: (0,) * len(shape), pipeline_mode=pl.Buffered(1))


def _proj_kernel(x_ref, g_ref, b_ref, wb_ref, ws_ref, cw_ref, ob_ref, os_ref, halo_ref, *,
                 tiles_per_seq):
    tm = x_ref.shape[0]
    halo = halo_ref.shape[0]

    @pl.when(pl.program_id(0) % tiles_per_seq == 0)
    def _():
        halo_ref[...] = jnp.zeros_like(halo_ref)

    h = _layer_norm(x_ref[...], g_ref[...], b_ref[...]).astype(BF16)
    nchunk = N_BIG // PROJ_TN
    nconv = GDN_QKV // PROJ_TN
    order = [c for pair in zip(range(nconv), range(nconv, 2 * nconv)) for c in pair]
    order += list(range(2 * nconv, nchunk))
    def chunk_dot(c):
        return _dot(h, wb_ref[:, c * PROJ_TN:(c + 1) * PROJ_TN])

    inflight = [chunk_dot(c) for c in order[:PROJ_AHEAD]]
    for pos, j in enumerate(order):
        cols = slice(j * PROJ_TN, (j + 1) * PROJ_TN)
        acc = inflight.pop(0)
        if pos + PROJ_AHEAD < nchunk:
            inflight.append(chunk_dot(order[pos + PROJ_AHEAD]))
        if (j + 1) * PROJ_TN <= GDN_QKV:
            ext = jnp.concatenate([halo_ref[:, cols], acc], axis=0)
            y = acc * cw_ref[CONV_W - 1:CONV_W, cols]
            for d in range(1, CONV_W):
                y = y + pltpu.roll(ext, d, 0)[halo:] * cw_ref[CONV_W - 1 - d:CONV_W - d, cols]
            halo_ref[:, cols] = acc[tm - halo:]
            acc = y * _sigmoid(y)
        elif FOX_Q0 <= j * PROJ_TN < FOX_Q0 + FOX_WIDTH:
            acc = acc * FOX_QSCALE
        ob_ref[:, cols] = acc.astype(BF16)
    os_ref[...] = _dot(h, ws_ref[...])


def _proj(x2, ln_g, ln_b, w_big, w_small, conv_w, seq):
    n = x2.shape[0]
    assert GDN_QKV % PROJ_TN == 0 and seq % PROJ_TM == 0
    return pl.pallas_call(
        functools.partial(_proj_kernel, tiles_per_seq=seq // PROJ_TM),
        grid=(n // PROJ_TM,),
        in_specs=[
            pl.BlockSpec((PROJ_TM, D_MODEL), lambda i: (i, 0)),
            _resident((1, D_MODEL)),
            _resident((1, D_MODEL)),
            _resident((D_MODEL, N_BIG)),
            _resident((D_MODEL, N_SMALL)),
            _resident((CONV_W, GDN_QKV)),
        ],
        out_specs=[
            pl.BlockSpec((PROJ_TM, N_BIG), lambda i: (i, 0)),
            pl.BlockSpec((PROJ_TM, N_SMALL), lambda i: (i, 0)),
        ],
        out_shape=[
            jax.ShapeDtypeStruct((n, N_BIG), BF16),
            jax.ShapeDtypeStruct((n, N_SMALL), F32),
        ],
        scratch_shapes=[pltpu.VMEM((8, GDN_QKV), F32)],
        compiler_params=pltpu.CompilerParams(
            dimension_semantics=("arbitrary",), vmem_limit_bytes=VMEM_LIMIT),
        name="proj",
    )(x2, ln_g, ln_b, w_big, w_small, conv_w)


def _gates_kernel(s_ref, pv_ref, o_ref, gr_ref, *, seq):
    lane = lax.broadcasted_iota(jnp.int32, (1, N_SMALL), 1)
    bias = pv_ref[0:1, :]
    neg_a = -jnp.exp(pv_ref[1:2, :])
    r = lax.broadcasted_iota(jnp.int32, (GATE_TB, GATE_TB), 0)
    c = lax.broadcasted_iota(jnp.int32, (GATE_TB, GATE_TB), 1)
    l_full = jnp.where(r >= c, 1.0, 0.0).astype(F32)
    same_chunk = jnp.right_shift(r, 6) == jnp.right_shift(c, 6)
    l_chunk = jnp.where(same_chunk, l_full, 0.0)
    l_total = jnp.where(same_chunk, 1.0, 0.0).astype(F32)
    l_decay = jnp.concatenate([l_chunk, l_total], axis=0).astype(BF16)
    l_forget = l_full.astype(BF16)

    def pieces_sum(a):
        return a[:, 0:N_SMALL] + a[:, N_SMALL:2 * N_SMALL] + a[:, 2 * N_SMALL:]

    carry = jnp.zeros((1, N_SMALL), F32)
    for t in range(seq // GATE_TB):
        rows = slice(t * GATE_TB, (t + 1) * GATE_TB)
        x = s_ref[rows, :] + bias
        beta = _sigmoid(x)
        log_g = neg_a * _softplus(x)
        log_f = -_softplus(-x)
        decay = pieces_sum(_dot(l_decay, _split3(log_g)))
        gam, gtot = decay[:GATE_TB], decay[GATE_TB:]
        cum = pieces_sum(_dot(l_forget, _split3(log_f))) + carry
        carry = cum[GATE_TB - 1:GATE_TB, :]
        o_ref[rows, :] = jnp.where(
            lane < LANE_GAM, beta,
            jnp.where(lane < LANE_C, gam, jnp.where(lane < LANE_GTOT, cum, gtot)))
        gr_ref[0, t] = jnp.transpose(gam)[LANE_GAM:LANE_GAM + GDN_HEADS, :]


def _gates(small, pvec, batch, seq):
    assert GATE_TB == GDN_TB
    nt = seq // GATE_TB
    return pl.pallas_call(
        functools.partial(_gates_kernel, seq=seq),
        grid=(batch,),
        in_specs=[
            pl.BlockSpec((seq, N_SMALL), lambda b: (b, 0)),
            _resident((8, N_SMALL)),
        ],
        out_specs=[
            pl.BlockSpec((seq, N_SMALL), lambda b: (b, 0)),
            pl.BlockSpec((1, nt, GDN_HEADS, GATE_TB), lambda b: (b, 0, 0, 0)),
        ],
        out_shape=[
            jax.ShapeDtypeStruct((batch * seq, N_SMALL), F32),
            jax.ShapeDtypeStruct((batch, nt, GDN_HEADS, GATE_TB), F32),
        ],
        compiler_params=pltpu.CompilerParams(
            dimension_semantics=("arbitrary",), vmem_limit_bytes=VMEM_LIMIT),
        name="gates",
    )(small, pvec)


def _gdn_kernel(x_ref, z_ref, gc_ref, gr_ref, ng_ref, o_ref, s_ref):
    nb, tb = x_ref.shape[0], x_ref.shape[1]
    nch = tb // CHUNK
    heads = range(nb * GDN_HEADS)
    nh = GDN_HEADS

    @pl.when(pl.program_id(1) == 0)
    def _():
        s_ref[...] = jnp.zeros_like(s_ref)

    def act(h, c0):
        c0 = c0 + (h % nh) * GDN_DK
        return x_ref[h // nh, :, c0:c0 + GDN_DK].astype(F32)

    def l2norm(v):
        return v * lax.rsqrt(jnp.sum(v * v, -1, keepdims=True) + NORM_EPS)

    ri = lax.broadcasted_iota(jnp.int32, (tb, tb), 0)
    ci = lax.broadcasted_iota(jnp.int32, (tb, tb), 1)
    same = jnp.right_shift(ri, 6) == jnp.right_shift(ci, 6)
    keep = jnp.logical_and(same, ri >= ci)
    diag = ri == ci
    pr = lax.broadcasted_iota(jnp.int32, (CHUNK, tb), 0)
    pc = lax.broadcasted_iota(jnp.int32, (CHUNK, tb), 1)
    eye_packed = jnp.where(jnp.bitwise_and(pc, CHUNK - 1) == pr, 1.0, 0.0).astype(F32)
    lane_chunk = jnp.right_shift(lax.broadcasted_iota(jnp.int32, (1, tb), 1), 6)
    row_chunk = jnp.right_shift(lax.broadcasted_iota(jnp.int32, (tb, 1), 0), 6)

    def to_bd(packed):
        return jnp.where(same, jnp.concatenate([packed] * nch, axis=0), jnp.zeros((), packed.dtype))

    def to_packed(bd):
        out = bd[0:CHUNK]
        for c in range(1, nch):
            out = out + bd[c * CHUNK:(c + 1) * CHUNK]
        return out

    ng = ng_ref[...]
    q = [l2norm(act(h, 0)) * (GDN_DK ** -0.5) for h in heads]
    k = [l2norm(act(h, GDN_QK)) for h in heads]
    v = [act(h, 2 * GDN_QK) for h in heads]

    def gate(h, lane0):
        return gc_ref[h // nh, :, lane0 + h % nh:lane0 + h % nh + 1]

    beta = [gate(h, LANE_BETA) for h in heads]
    gam = [gate(h, LANE_GAM) for h in heads]
    gtot = [gate(h, LANE_GTOT) for h in heads]
    grow = [gr_ref[h // nh, 0, h % nh:h % nh + 1, :] for h in heads]
    egam = [jnp.exp(g) for g in gam]
    kb = [a.astype(BF16) for a in k]
    decay = [jnp.exp(jnp.where(keep, gam[h] - grow[h], -jnp.inf)) for h in heads]
    gram = [_dot_nt(kb[h], kb[h]) for h in heads]
    qk = [(_dot_nt(q[h].astype(BF16), kb[h]) * decay[h]).astype(BF16) for h in heads]

    x_bd = [jnp.where(diag, 0.0, -(gram[h] * beta[h]) * decay[h]) for h in heads]
    x_p = [to_packed(a) for a in x_bd]
    p_p = [eye_packed + a for a in x_p]
    x_p = [_dot(x_p[h].astype(BF16), x_bd[h].astype(BF16)) for h in heads]
    for _ in range(4):
        w_bd = [to_bd(a.astype(BF16)) for a in x_p]
        r = [_dot(jnp.concatenate([p_p[h], x_p[h]], axis=0).astype(BF16), w_bd[h]) for h in heads]
        p_p = [p_p[h] + r[h][:CHUNK] for h in heads]
        x_p = [r[h][CHUNK:] for h in heads]
    p_p = [p_p[h] + _dot(p_p[h].astype(BF16), to_bd(x_p[h].astype(BF16))) for h in heads]

    rhs = [jnp.concatenate([v[h] * beta[h], k[h] * (beta[h] * egam[h])], axis=1).astype(BF16)
           for h in heads]
    sol = [_dot(to_bd(p_p[h].astype(BF16)), rhs[h]) for h in heads]
    u = [a[:, :GDN_DV] for a in sol]
    w = [a[:, GDN_DV:].astype(BF16) for a in sol]
    qd = [(q[h] * egam[h]).astype(BF16) for h in heads]
    kd_t = [jnp.transpose(k[h] * jnp.exp(gtot[h] - gam[h])).astype(BF16) for h in heads]
    z = [z_ref[h // nh, :, (h % nh) * GDN_DV:(h % nh + 1) * GDN_DV].astype(F32) for h in heads]
    zgate = [a * _sigmoid(a) for a in z]

    state = [s_ref[h] for h in heads]
    for c in range(nch):
        rows = slice(c * CHUNK, (c + 1) * CHUNK)
        sb = [a.astype(BF16) for a in state]
        r1 = [_dot(jnp.concatenate([w[h][rows], qd[h][rows]], axis=0), sb[h]) for h in heads]
        v_new = [u[h][rows] - r1[h][:CHUNK] for h in heads]
        v_full = [jnp.where(row_chunk == c, jnp.concatenate([a.astype(BF16)] * nch, axis=0),
                            jnp.zeros((), BF16)) for a in v_new]
        kd_c = [jnp.where(lane_chunk == c, a, jnp.zeros((), BF16)) for a in kd_t]
        r2 = [_dot(jnp.concatenate([qk[h][rows], kd_c[h]], axis=0), v_full[h]) for h in heads]
        for h in heads:
            o = r1[h][CHUNK:] + r2[h][:CHUNK]
            on = o * lax.rsqrt(jnp.mean(o * o, -1, keepdims=True) + NORM_EPS) * ng
            o_ref[h // nh, rows, (h % nh) * GDN_DV:(h % nh + 1) * GDN_DV] = (
                on * zgate[h][rows]).astype(BF16)
        state = [state[h] * jnp.exp(gtot[h][c * CHUNK:c * CHUNK + 1]) + r2[h][CHUNK:] for h in heads]
    for h in heads:
        s_ref[h] = state[h]


def _gdn(proj_big, gcol, grow, norm_g, batch, seq):
    nt = seq // GDN_TB
    nb = GDN_NB
    assert batch % nb == 0
    return pl.pallas_call(
        _gdn_kernel,
        grid=(batch // nb, nt),
        in_specs=[
            pl.BlockSpec((nb, GDN_TB, GDN_QKV), lambda g, t: (g, t, 0)),
            pl.BlockSpec((nb, GDN_TB, GDN_WIDTH), lambda g, t: (g, t, OFF_Z // GDN_WIDTH)),
            pl.BlockSpec((nb, GDN_TB, N_SMALL), lambda g, t: (g, t, 0)),
            pl.BlockSpec((nb, 1, GDN_HEADS, GDN_TB), lambda g, t: (g, t, 0, 0)),
            _resident((1, GDN_DV)),
        ],
        out_specs=pl.BlockSpec((nb, GDN_TB, GDN_WIDTH), lambda g, t: (g, t, 0)),
        out_shape=jax.ShapeDtypeStruct((batch, seq, GDN_WIDTH), BF16),
        scratch_shapes=[
            pltpu.VMEM((nb * GDN_HEADS, GDN_DK, GDN_DV), F32),
        ],
        compiler_params=pltpu.CompilerParams(
            dimension_semantics=("arbitrary", "arbitrary"), vmem_limit_bytes=VMEM_LIMIT),
        name="gdn",
    )(proj_big, proj_big, gcol, grow, norm_g)


FOX_X = 128
FOX_QC = (0, 6)
FOX_KC = (3, 9)


def _split3(c):
    hi = c.astype(BF16)
    r1 = c - hi.astype(F32)
    mid = r1.astype(BF16)
    lo = (r1 - mid.astype(F32)).astype(BF16)
    return jnp.concatenate([hi, mid, lo], axis=1)


def _select3(pair, base):
    e = np.zeros((3 * N_SMALL, FOX_X), np.float32)
    for d in range(3):
        for head in range(2):
            e[d * N_SMALL + LANE_C + 2 * pair + head, base[head] + d] = 1.0
    return e


def _lane_ones(first):
    out = np.zeros((FOX_X,), np.float32)
    for f in first:
        out[f:f + 3] = 1.0
    return out


def _fox_constants():
    npair = FOX_HEADS // 2
    sel = np.stack([np.stack([_select3(p, base) for p in range(npair)])
                    for base in (FOX_KC, FOX_QC)])
    ones = np.stack([_lane_ones(FOX_QC), _lane_ones(FOX_KC)])
    return jnp.asarray(sel, BF16), jnp.asarray(ones, F32)


def _fox_kernel(q_ref, k_ref, v_ref, gc_ref, ng_ref, sel_ref, ones_ref, o_ref, qa_ref, ka_ref, vt_ref,
                m_ref, l_ref, acc_ref, s_ref):
    nb, tq = q_ref.shape[0], q_ref.shape[1]
    seq = k_ref.shape[1]
    i = pl.program_id(1)
    npair = FOX_HEADS // 2
    pairs = range(nb * npair)
    lane = lax.broadcasted_iota(jnp.int32, (1, 2 * FOX_DH), 1)
    lo_half = lane < FOX_DH
    pcols = [slice((p % npair) * 2 * FOX_DH, (p % npair + 1) * 2 * FOX_DH) for p in pairs]

    @pl.when(i == 0)
    def _():
        ones_k = ones_ref[0:1, :]

        def fill(t, carry):
            rows = pl.ds(pl.multiple_of(t * tq, tq), tq)
            pieces = [_split3(gc_ref[r, rows, :] * LOG2E) for r in range(nb)]
            for p in pairs:
                r = p // npair
                ka_ref[p, rows, 0:2 * FOX_DH] = k_ref[r, rows, pcols[p]]
                ka_ref[p, rows, 2 * FOX_DH:] = (
                    ones_k - _dot(pieces[r], sel_ref[0, p % npair])).astype(BF16)
                vt_ref[p, t] = jnp.transpose(v_ref[r, rows, pcols[p]].astype(F32)).astype(BF16)
            return carry

        lax.fori_loop(0, seq // tq, fill, 0)

    qrows = pl.ds(pl.multiple_of(i * tq, tq), tq)
    pieces_q = [_split3(gc_ref[r, qrows, :] * LOG2E) for r in range(nb)]
    ones_q = ones_ref[1:2, :]
    xlane = lax.broadcasted_iota(jnp.int32, (1, FOX_X), 1)
    for p in pairs:
        qp = q_ref[p // npair, :, pcols[p]]
        qx = _dot(pieces_q[p // npair], sel_ref[1, p % npair]) + ones_q
        for half in range(2):
            rows = slice(half * tq, (half + 1) * tq)
            mine = lo_half if half == 0 else jnp.logical_not(lo_half)
            xmine = (xlane < FOX_QC[1]) if half == 0 else (xlane >= FOX_QC[1])
            qa_ref[p, rows, 0:2 * FOX_DH] = jnp.where(mine, qp, jnp.zeros((), BF16))
            qa_ref[p, rows, 2 * FOX_DH:] = jnp.where(xmine, qx, 0.0).astype(BF16)

    ki = lax.broadcasted_iota(jnp.int32, (tq, 2 * tq), 0)
    qi = jnp.bitwise_and(lax.broadcasted_iota(jnp.int32, (tq, 2 * tq), 1), tq - 1)
    causal = ki <= qi

    def scores(j):
        krows = pl.ds(pl.multiple_of(j * tq, tq), tq)
        return [_dot_nt(ka_ref[p, krows, :], qa_ref[p]) for p in pairs]

    def absorb(s, j, first):
        for p in pairs:
            sp = s[p]
            if first:
                sp = jnp.where(causal, sp, -jnp.inf)
            m_new = jnp.max(sp, 0, keepdims=True)
            if not first:
                m_prev = m_ref[p]
                m_new = jnp.maximum(m_prev, m_new)
                scale = jnp.exp2(m_prev - m_new)
            prob = jnp.exp2(sp - m_new)
            psum = jnp.sum(prob, 0, keepdims=True)
            pv = _dot(vt_ref[p, j], prob.astype(BF16))
            m_ref[p] = m_new
            if first:
                l_ref[p] = psum
                acc_ref[p] = pv
            else:
                l_ref[p] = scale * l_ref[p] + psum
                acc_ref[p] = scale * acc_ref[p] + pv

    last = jnp.maximum(i - 1, 0)

    def stage(slot, j_next):
        s_next = scores(jnp.minimum(j_next, last))
        for p in pairs:
            s_ref[slot, p] = s_next[p]

    def consume(slot, j):
        absorb([s_ref[slot, p] for p in pairs], j, False)

    s_diag = scores(i)
    stage(0, 0)
    absorb(s_diag, i, True)

    def body(t, carry):
        stage(1, 2 * t + 1)
        consume(0, 2 * t)
        stage(0, 2 * t + 2)
        consume(1, 2 * t + 1)
        return carry

    lax.fori_loop(0, i // 2, body, 0)

    @pl.when(i % 2 == 1)
    def _():
        consume(0, i - 1)

    ng = ng_ref[...]
    for p in pairs:
        acc = acc_ref[p]
        l = l_ref[p]
        outs = []
        for half in range(2):
            oh = (acc[half * FOX_DH:(half + 1) * FOX_DH, half * tq:(half + 1) * tq]
                  / l[:, half * tq:(half + 1) * tq])
            ms = jnp.mean(oh * oh, 0, keepdims=True)
            outs.append(oh * lax.rsqrt(ms + NORM_EPS))
        o_ref[p // npair, :, pcols[p]] = (
            jnp.transpose(jnp.concatenate(outs, axis=0)) * ng).astype(BF16)


def _fox(proj_big, gcol, norm_g2, batch, seq):
    nq = seq // FOX_TQ
    first = (GDN_QKV + GDN_WIDTH) // FOX_WIDTH
    npair = FOX_HEADS // 2
    nb = FOX_NB
    assert batch % nb == 0
    nu = nb * npair
    return pl.pallas_call(
        _fox_kernel,
        grid=(batch // nb, nq),
        in_specs=[
            pl.BlockSpec((nb, FOX_TQ, FOX_WIDTH), lambda g, i: (g, i, first)),
            pl.BlockSpec((nb, seq, FOX_WIDTH), lambda g, i: (g, 0, first + 1)),
            pl.BlockSpec((nb, seq, FOX_WIDTH), lambda g, i: (g, 0, first + 2)),
            pl.BlockSpec((nb, seq, N_SMALL), lambda g, i: (g, 0, 0)),
            _resident((1, 2 * FOX_DH)),
            _resident((2, npair, 3 * N_SMALL, FOX_X)),
            _resident((2, FOX_X)),
        ],
        out_specs=pl.BlockSpec((nb, FOX_TQ, FOX_WIDTH), lambda g, i: (g, i, 0)),
        out_shape=jax.ShapeDtypeStruct((batch, seq, FOX_WIDTH), BF16),
        scratch_shapes=[
            pltpu.VMEM((nu, 2 * FOX_TQ, 2 * FOX_DH + FOX_X), BF16),
            pltpu.VMEM((nu, seq, 2 * FOX_DH + FOX_X), BF16),
            pltpu.VMEM((nu, nq, 2 * FOX_DH, FOX_TQ), BF16),
            pltpu.VMEM((nu, 1, 2 * FOX_TQ), F32),
            pltpu.VMEM((nu, 1, 2 * FOX_TQ), F32),
            pltpu.VMEM((nu, 2 * FOX_DH, 2 * FOX_TQ), F32),
            pltpu.VMEM((2, nu, FOX_TQ, 2 * FOX_TQ), F32),
        ],
        compiler_params=pltpu.CompilerParams(
            dimension_semantics=("arbitrary", "arbitrary"), vmem_limit_bytes=VMEM_LIMIT),
        name="fox",
    )(proj_big, proj_big, proj_big, gcol, norm_g2, *_fox_constants())


def _tail_kernel(x_ref, p_ref, og_ref, of_ref, lin_g, lin_b, wo_ref, l1g, l1b, wu_ref, wd_ref,
                 wp_ref, wg_ref, bg_ref, l2g, l2b, o_ref):
    nparts = x_ref.shape[0] // TAIL_PART
    parts = [slice(r * TAIL_PART, (r + 1) * TAIL_PART) for r in range(nparts)]

    def head(rows):
        h = _layer_norm(x_ref[rows, :], lin_g[...], lin_b[...])
        mix = (_dot(og_ref[rows, :], wo_ref[0:GDN_WIDTH, :])
               + _dot(of_ref[rows, :], wo_ref[GDN_WIDTH:, :]))
        return _layer_norm(ALPHA * h + mix, l1g[...], l1b[...])

    def mlp(rows, h1):
        h1b = h1.astype(BF16)
        gate = _sigmoid(_dot(h1b, wg_ref[...]) + bg_ref[...])
        acc = ALPHA * h1 + _dot(p_ref[rows, :].astype(BF16), wp_ref[...]) * gate
        for j in range(D_FF // TAIL_TF):
            cols = slice(j * TAIL_TF, (j + 1) * TAIL_TF)
            a = jnp.maximum(_dot(h1b, wu_ref[:, cols]), 0.0)
            acc = acc + _dot((a * a).astype(BF16), wd_ref[cols, :])
        return acc

    h1 = head(parts[0])
    for r in range(nparts):
        h1_next = head(parts[r + 1]) if r + 1 < nparts else None
        o_ref[parts[r], :] = _layer_norm(mlp(parts[r], h1), l2g[...], l2b[...])
        h1 = h1_next


def _tail(x2, p2, o_gdn, o_fox, lin_g, lin_b, w_out, l1g, l1b, w_up, w_down, w_ple, w_gate,
          b_gate, l2g, l2b):
    n = x2.shape[0]
    row = lambda width: pl.BlockSpec((TAIL_TM, width), lambda i: (i, 0))
    vec = _resident((1, D_MODEL))
    return pl.pallas_call(
        _tail_kernel,
        grid=(n // TAIL_TM,),
        in_specs=[
            row(D_MODEL), row(D_PLE), row(GDN_WIDTH), row(FOX_WIDTH),
            vec, vec, _resident((D_MODEL, D_MODEL)), vec, vec,
            _resident((D_MODEL, D_FF)), _resident((D_FF, D_MODEL)),
            _resident((D_PLE, D_MODEL)), _resident((D_MODEL, D_MODEL)), vec, vec, vec,
        ],
        out_specs=row(D_MODEL),
        out_shape=jax.ShapeDtypeStruct((n, D_MODEL), F32),
        compiler_params=pltpu.CompilerParams(
            dimension_semantics=("arbitrary",), vmem_limit_bytes=VMEM_LIMIT),
        name="tail",
    )(x2, p2, o_gdn, o_fox, lin_g, lin_b, w_out, l1g, l1b, w_up, w_down, w_ple, w_gate,
      b_gate, l2g, l2b)


def kernel(x, p, ln_in_g, ln_in_b, w_in, conv_w, a_log, dt_bias, gdn_norm_g, b_f, fox_norm_g,
           w_out, ln1_g, ln1_b, w_up, w_down, w_ple, w_ple_gate, b_ple_gate, ln2_g, ln2_b):
    batch, seq, _ = x.shape
    assert x.shape[2] == D_MODEL and w_in.shape[0] == 1
    assert seq % FOX_TQ == 0 and seq % GDN_TB == 0 and (batch * seq) % PROJ_TM == 0
    n = batch * seq
    x2 = x.reshape(n, D_MODEL)
    p2 = p[0].reshape(n, D_PLE)
    row = lambda a: a.reshape(1, -1).astype(F32)

    w0 = w_in[0]
    w_big = jnp.concatenate([w0[:, :OFF_BETA], w0[:, OFF_FOX:OFF_F]], axis=1).astype(BF16)
    w_decay = w0[:, OFF_BETA + GDN_HEADS:OFF_FOX]
    n_gate = LANE_GTOT + GDN_HEADS
    w_small = jnp.concatenate(
        [w0[:, OFF_BETA:OFF_FOX], w0[:, OFF_F:], w_decay, jnp.zeros((D_MODEL, N_SMALL - n_gate), F32)],
        axis=1).astype(BF16)
    zeros4 = jnp.zeros((GDN_HEADS,), F32)
    pad = jnp.zeros((N_SMALL - n_gate,), F32)
    pvec = jnp.zeros((8, N_SMALL), F32)
    pvec = pvec.at[0].set(jnp.concatenate([zeros4, dt_bias[0], b_f[0], dt_bias[0], pad]))
    pvec = pvec.at[1].set(
        jnp.concatenate([zeros4, a_log[0], jnp.zeros((FOX_HEADS,), F32), a_log[0], pad]))

    proj_big, small = _proj(x2, row(ln_in_g), row(ln_in_b), w_big, w_small, conv_w[0], seq)
    gcol, grow = _gates(small, pvec, batch, seq)

    proj3 = proj_big.reshape(batch, seq, N_BIG)
    gcol3 = gcol.reshape(batch, seq, N_SMALL)
    o_gdn = _gdn(proj3, gcol3, grow, row(gdn_norm_g[0]), batch, seq).reshape(n, GDN_WIDTH)
    o_fox = _fox(proj3, gcol3, row(jnp.tile(fox_norm_g[0], 2)), batch, seq).reshape(n, FOX_WIDTH)

    out = _tail(x2, p2, o_gdn, o_fox, row(ln_in_g), row(ln_in_b), w_out[0].astype(BF16),
                row(ln1_g[0]), row(ln1_b[0]), w_up[0].astype(BF16), w_down[0].astype(BF16),
                w_ple[0].astype(BF16), w_ple_gate[0].astype(BF16), row(b_ple_gate[0]),
                row(ln2_g[0]), row(ln2_b[0]))
    return out.reshape(batch, seq, D_MODEL)
```

```python
import functools

import jax
import jax.numpy as jnp
import numpy as np
from jax import lax
from jax.experimental import pallas as pl
from jax.experimental.pallas import tpu as pltpu

F32 = jnp.float32
BF16 = jnp.bfloat16

D_MODEL = 1024
CHUNK = 64
GDN_HEADS = 4
GDN_DK = 128
GDN_DV = 128
GDN_QK = GDN_HEADS * GDN_DK
GDN_QKV = 3 * GDN_QK
GDN_WIDTH = GDN_HEADS * GDN_DV
FOX_HEADS = 8
FOX_DH = 64
FOX_WIDTH = FOX_HEADS * FOX_DH
CONV_W = 4
D_FF = 4 * D_MODEL
D_PLE = 256
LN_EPS = 1e-5
NORM_EPS = 1e-6
ALPHA = 2.0 ** 0.25

OFF_Z = GDN_QKV
OFF_BETA = OFF_Z + GDN_WIDTH
OFF_FOX = OFF_BETA + 2 * GDN_HEADS
OFF_F = OFF_FOX + 3 * FOX_WIDTH
N_BIG = GDN_QKV + GDN_WIDTH + 3 * FOX_WIDTH
N_SMALL = 128
FOX_Q0 = GDN_QKV + GDN_WIDTH
LOG2E = 1.4426950408889634
FOX_QSCALE = FOX_DH ** -0.5 * LOG2E
LANE_BETA = 0
LANE_GAM = GDN_HEADS
LANE_C = 2 * GDN_HEADS
LANE_GTOT = LANE_C + FOX_HEADS

VMEM_LIMIT = 56 * 1024 * 1024

PROJ_TM = 512
PROJ_TN = 256
PROJ_AHEAD = 1
GATE_TB = 256
GDN_TB = 256
GDN_NB = 4
FOX_TQ = 256
FOX_NB = 2
TAIL_TM = 512
TAIL_PART = 256
TAIL_TF = 1024


def _layer_norm(x, g, b):
    mu = jnp.mean(x, -1, keepdims=True)
    xc = x - mu
    var = jnp.mean(xc * xc, -1, keepdims=True)
    return xc * lax.rsqrt(var + LN_EPS) * g + b


def _softplus(x):
    return jnp.maximum(x, 0.0) + jnp.log(1.0 + jnp.exp(-jnp.abs(x)))


def _sigmoid(x):
    return 1.0 / (1.0 + jnp.exp(-x))


def _dot(a, b):
    return jnp.dot(a, b, preferred_element_type=F32)


def _dot_nt(a, b):
    return lax.dot_general(a, b, (((1,), (1,)), ((), ())), preferred_element_type=F32)


def _resident(shape):
    return pl.BlockSpec(shape, lambda *_: (0,) * len(shape), pipeline_mode=pl.Buffered(1))


def _proj_kernel(x_ref, g_ref, b_ref, wb_ref, ws_ref, cw_ref, ob_ref, os_ref, halo_ref, *,
                 tiles_per_seq):
    tm = x_ref.shape[0]
    halo = halo_ref.shape[0]

    @pl.when(pl.program_id(0) % tiles_per_seq == 0)
    def _():
        halo_ref[...] = jnp.zeros_like(halo_ref)

    h = _layer_norm(x_ref[...], g_ref[...], b_ref[...]).astype(BF16)
    nchunk = N_BIG // PROJ_TN
    nconv = GDN_QKV // PROJ_TN
    order = [c for pair in zip(range(nconv), range(nconv, 2 * nconv)) for c in pair]
    order += list(range(2 * nconv, nchunk))
    def chunk_dot(c):
        return _dot(h, wb_ref[:, c * PROJ_TN:(c + 1) * PROJ_TN])

    inflight = [chunk_dot(c) for c in order[:PROJ_AHEAD]]
    for pos, j in enumerate(order):
        cols = slice(j * PROJ_TN, (j + 1) * PROJ_TN)
        acc = inflight.pop(0)
        if pos + PROJ_AHEAD < nchunk:
            inflight.append(chunk_dot(order[pos + PROJ_AHEAD]))
        if (j + 1) * PROJ_TN <= GDN_QKV:
            ext = jnp.concatenate([halo_ref[:, cols], acc], axis=0)
            y = acc * cw_ref[CONV_W - 1:CONV_W, cols]
            for d in range(1, CONV_W):
                y = y + pltpu.roll(ext, d, 0)[halo:] * cw_ref[CONV_W - 1 - d:CONV_W - d, cols]
            halo_ref[:, cols] = acc[tm - halo:]
            acc = y * _sigmoid(y)
        elif FOX_Q0 <= j * PROJ_TN < FOX_Q0 + FOX_WIDTH:
            acc = acc * FOX_QSCALE
        ob_ref[:, cols] = acc.astype(BF16)
    os_ref[...] = _dot(h, ws_ref[...])


def _proj(x2, ln_g, ln_b, w_big, w_small, conv_w, seq):
    n = x2.shape[0]
    assert GDN_QKV % PROJ_TN == 0 and seq % PROJ_TM == 0
    return pl.pallas_call(
        functools.partial(_proj_kernel, tiles_per_seq=seq // PROJ_TM),
        grid=(n // PROJ_TM,),
        in_specs=[
            pl.BlockSpec((PROJ_TM, D_MODEL), lambda i: (i, 0)),
            _resident((1, D_MODEL)),
            _resident((1, D_MODEL)),
            _resident((D_MODEL, N_BIG)),
            _resident((D_MODEL, N_SMALL)),
            _resident((CONV_W, GDN_QKV)),
        ],
        out_specs=[
            pl.BlockSpec((PROJ_TM, N_BIG), lambda i: (i, 0)),
            pl.BlockSpec((PROJ_TM, N_SMALL), lambda i: (i, 0)),
        ],
        out_shape=[
            jax.ShapeDtypeStruct((n, N_BIG), BF16),
            jax.ShapeDtypeStruct((n, N_SMALL), F32),
        ],
        scratch_shapes=[pltpu.VMEM((8, GDN_QKV), F32)],
        compiler_params=pltpu.CompilerParams(
            dimension_semantics=("arbitrary",), vmem_limit_bytes=VMEM_LIMIT),
        name="proj",
    )(x2, ln_g, ln_b, w_big, w_small, conv_w)


def _gates_kernel(s_ref, pv_ref, o_ref, gr_ref, *, seq):
    lane = lax.broadcasted_iota(jnp.int32, (1, N_SMALL), 1)
    bias = pv_ref[0:1, :]
    neg_a = -jnp.exp(pv_ref[1:2, :])
    r = lax.broadcasted_iota(jnp.int32, (GATE_TB, GATE_TB), 0)
    c = lax.broadcasted_iota(jnp.int32, (GATE_TB, GATE_TB), 1)
    l_full = jnp.where(r >= c, 1.0, 0.0).astype(F32)
    same_chunk = jnp.right_shift(r, 6) == jnp.right_shift(c, 6)
    l_chunk = jnp.where(same_chunk, l_full, 0.0)
    l_total = jnp.where(same_chunk, 1.0, 0.0).astype(F32)
    l_decay = jnp.concatenate([l_chunk, l_total], axis=0).astype(BF16)
    l_forget = l_full.astype(BF16)

    def pieces_sum(a):
        return a[:, 0:N_SMALL] + a[:, N_SMALL:2 * N_SMALL] + a[:, 2 * N_SMALL:]

    carry = jnp.zeros((1, N_SMALL), F32)
    for t in range(seq // GATE_TB):
        rows = slice(t * GATE_TB, (t + 1) * GATE_TB)
        x = s_ref[rows, :] + bias
        beta = _sigmoid(x)
        log_g = neg_a * _softplus(x)
        log_f = -_softplus(-x)
        decay = pieces_sum(_dot(l_decay, _split3(log_g)))
        gam, gtot = decay[:GATE_TB], decay[GATE_TB:]
        cum = pieces_sum(_dot(l_forget, _split3(log_f))) + carry
        carry = cum[GATE_TB - 1:GATE_TB, :]
        o_ref[rows, :] = jnp.where(
            lane < LANE_GAM, beta,
            jnp.where(lane < LANE_C, gam, jnp.where(lane < LANE_GTOT, cum, gtot)))
        gr_ref[0, t] = jnp.transpose(gam)[LANE_GAM:LANE_GAM + GDN_HEADS, :]


def _gates(small, pvec, batch, seq):
    assert GATE_TB == GDN_TB
    nt = seq // GATE_TB
    return pl.pallas_call(
        functools.partial(_gates_kernel, seq=seq),
        grid=(batch,),
        in_specs=[
            pl.BlockSpec((seq, N_SMALL), lambda b: (b, 0)),
            _resident((8, N_SMALL)),
        ],
        out_specs=[
            pl.BlockSpec((seq, N_SMALL), lambda b: (b, 0)),
            pl.BlockSpec((1, nt, GDN_HEADS, GATE_TB), lambda b: (b, 0, 0, 0)),
        ],
        out_shape=[
            jax.ShapeDtypeStruct((batch * seq, N_SMALL), F32),
            jax.ShapeDtypeStruct((batch, nt, GDN_HEADS, GATE_TB), F32),
        ],
        compiler_params=pltpu.CompilerParams(
            dimension_semantics=("arbitrary",), vmem_limit_bytes=VMEM_LIMIT),
        name="gates",
    )(small, pvec)


def _gdn_kernel(x_ref, z_ref, gc_ref, gr_ref, ng_ref, *rest, ncast):
    o_ref, s_ref = rest[ncast], rest[2 * ncast + 1]
    for w_f32, w_bf16 in zip(rest[:ncast], rest[ncast + 1:2 * ncast + 1]):
        w_bf16[...] = w_f32[...].astype(BF16)

    nb, tb = x_ref.shape[0], x_ref.shape[1]
    nch = tb // CHUNK
    heads = range(nb * GDN_HEADS)
    nh = GDN_HEADS

    @pl.when(pl.program_id(1) == 0)
    def _():
        s_ref[...] = jnp.zeros_like(s_ref)

    def act(h, c0):
        c0 = c0 + (h % nh) * GDN_DK
        return x_ref[h // nh, :, c0:c0 + GDN_DK].astype(F32)

    def l2norm(v):
        return v * lax.rsqrt(jnp.sum(v * v, -1, keepdims=True) + NORM_EPS)

    ri = lax.broadcasted_iota(jnp.int32, (tb, tb), 0)
    ci = lax.broadcasted_iota(jnp.int32, (tb, tb), 1)
    same = jnp.right_shift(ri, 6) == jnp.right_shift(ci, 6)
    keep = jnp.logical_and(same, ri >= ci)
    diag = ri == ci
    pr = lax.broadcasted_iota(jnp.int32, (CHUNK, tb), 0)
    pc = lax.broadcasted_iota(jnp.int32, (CHUNK, tb), 1)
    eye_packed = jnp.where(jnp.bitwise_and(pc, CHUNK - 1) == pr, 1.0, 0.0).astype(F32)
    lane_chunk = jnp.right_shift(lax.broadcasted_iota(jnp.int32, (1, tb), 1), 6)
    row_chunk = jnp.right_shift(lax.broadcasted_iota(jnp.int32, (tb, 1), 0), 6)

    def to_bd(packed):
        return jnp.where(same, jnp.concatenate([packed] * nch, axis=0), jnp.zeros((), packed.dtype))

    def to_packed(bd):
        out = bd[0:CHUNK]
        for c in range(1, nch):
            out = out + bd[c * CHUNK:(c + 1) * CHUNK]
        return out

    ng = ng_ref[...]
    q = [l2norm(act(h, 0)) * (GDN_DK ** -0.5) for h in heads]
    k = [l2norm(act(h, GDN_QK)) for h in heads]
    v = [act(h, 2 * GDN_QK) for h in heads]

    def gate(h, lane0):
        return gc_ref[h // nh, :, lane0 + h % nh:lane0 + h % nh + 1]

    beta = [gate(h, LANE_BETA) for h in heads]
    gam = [gate(h, LANE_GAM) for h in heads]
    gtot = [gate(h, LANE_GTOT) for h in heads]
    grow = [gr_ref[h // nh, 0, h % nh:h % nh + 1, :] for h in heads]
    egam = [jnp.exp(g) for g in gam]
    kb = [a.astype(BF16) for a in k]
    decay = [jnp.exp(jnp.where(keep, gam[h] - grow[h], -jnp.inf)) for h in heads]
    gram = [_dot_nt(kb[h], kb[h]) for h in heads]
    qk = [(_dot_nt(q[h].astype(BF16), kb[h]) * decay[h]).astype(BF16) for h in heads]

    x_bd = [jnp.where(diag, 0.0, -(gram[h] * beta[h]) * decay[h]) for h in heads]
    x_p = [to_packed(a) for a in x_bd]
    p_p = [eye_packed + a for a in x_p]
    x_p = [_dot(x_p[h].astype(BF16), x_bd[h].astype(BF16)) for h in heads]
    for _ in range(4):
        w_bd = [to_bd(a.astype(BF16)) for a in x_p]
        r = [_dot(jnp.concatenate([p_p[h], x_p[h]], axis=0).astype(BF16), w_bd[h]) for h in heads]
        p_p = [p_p[h] + r[h][:CHUNK] for h in heads]
        x_p = [r[h][CHUNK:] for h in heads]
    p_p = [p_p[h] + _dot(p_p[h].astype(BF16), to_bd(x_p[h].astype(BF16))) for h in heads]

    rhs = [jnp.concatenate([v[h] * beta[h], k[h] * (beta[h] * egam[h])], axis=1).astype(BF16)
           for h in heads]
    sol = [_dot(to_bd(p_p[h].astype(BF16)), rhs[h]) for h in heads]
    u = [a[:, :GDN_DV] for a in sol]
    w = [a[:, GDN_DV:].astype(BF16) for a in sol]
    qd = [(q[h] * egam[h]).astype(BF16) for h in heads]
    kd_t = [jnp.transpose(k[h] * jnp.exp(gtot[h] - gam[h])).astype(BF16) for h in heads]
    z = [z_ref[h // nh, :, (h % nh) * GDN_DV:(h % nh + 1) * GDN_DV].astype(F32) for h in heads]
    zgate = [a * _sigmoid(a) for a in z]

    state = [s_ref[h] for h in heads]
    for c in range(nch):
        rows = slice(c * CHUNK, (c + 1) * CHUNK)
        sb = [a.astype(BF16) for a in state]
        r1 = [_dot(jnp.concatenate([w[h][rows], qd[h][rows]], axis=0), sb[h]) for h in heads]
        v_new = [u[h][rows] - r1[h][:CHUNK] for h in heads]
        v_full = [jnp.where(row_chunk == c, jnp.concatenate([a.astype(BF16)] * nch, axis=0),
                            jnp.zeros((), BF16)) for a in v_new]
        kd_c = [jnp.where(lane_chunk == c, a, jnp.zeros((), BF16)) for a in kd_t]
        r2 = [_dot(jnp.concatenate([qk[h][rows], kd_c[h]], axis=0), v_full[h]) for h in heads]
        for h in heads:
            o = r1[h][CHUNK:] + r2[h][:CHUNK]
            on = o * lax.rsqrt(jnp.mean(o * o, -1, keepdims=True) + NORM_EPS) * ng
            o_ref[h // nh, rows, (h % nh) * GDN_DV:(h % nh + 1) * GDN_DV] = (
                on * zgate[h][rows]).astype(BF16)
        state = [state[h] * jnp.exp(gtot[h][c * CHUNK:c * CHUNK + 1]) + r2[h][CHUNK:] for h in heads]
    for h in heads:
        s_ref[h] = state[h]


def _gdn(proj_big, gcol, grow, norm_g, batch, seq, cast):
    nt = seq // GDN_TB
    nb = GDN_NB
    assert batch % nb == 0
    steps = (batch // nb) * nt
    assert all(w.shape[0] % (16 * steps) == 0 for w in cast)
    cast_specs = [pl.BlockSpec((w.shape[0] // steps, w.shape[1]), lambda g, t: (g * nt + t, 0))
                  for w in cast]
    return pl.pallas_call(
        functools.partial(_gdn_kernel, ncast=len(cast)),
        grid=(batch // nb, nt),
        in_specs=[
            pl.BlockSpec((nb, GDN_TB, GDN_QKV), lambda g, t: (g, t, 0)),
            pl.BlockSpec((nb, GDN_TB, GDN_WIDTH), lambda g, t: (g, t, OFF_Z // GDN_WIDTH)),
            pl.BlockSpec((nb, GDN_TB, N_SMALL), lambda g, t: (g, t, 0)),
            pl.BlockSpec((nb, 1, GDN_HEADS, GDN_TB), lambda g, t: (g, t, 0, 0)),
            _resident((1, GDN_DV)),
        ] + cast_specs,
        out_specs=[pl.BlockSpec((nb, GDN_TB, GDN_WIDTH), lambda g, t: (g, t, 0))] + cast_specs,
        out_shape=[jax.ShapeDtypeStruct((batch, seq, GDN_WIDTH), BF16)]
        + [jax.ShapeDtypeStruct(w.shape, BF16) for w in cast],
        scratch_shapes=[
            pltpu.VMEM((nb * GDN_HEADS, GDN_DK, GDN_DV), F32),
        ],
        compiler_params=pltpu.CompilerParams(
            dimension_semantics=("arbitrary", "arbitrary"), vmem_limit_bytes=VMEM_LIMIT),
        name="gdn",
    )(proj_big, proj_big, gcol, grow, norm_g, *cast)


FOX_X = 128
FOX_QC = (0, 6)
FOX_KC = (3, 9)


def _split3(c):
    hi = c.astype(BF16)
    r1 = c - hi.astype(F32)
    mid = r1.astype(BF16)
    lo = (r1 - mid.astype(F32)).astype(BF16)
    return jnp.concatenate([hi, mid, lo], axis=1)


def _select3(pair, base):
    e = np.zeros((3 * N_SMALL, FOX_X), np.float32)
    for d in range(3):
        for head in range(2):
            e[d * N_SMALL + LANE_C + 2 * pair + head, base[head] + d] = 1.0
    return e


def _lane_ones(first):
    out = np.zeros((FOX_X,), np.float32)
    for f in first:
        out[f:f + 3] = 1.0
    return out


def _fox_constants():
    npair = FOX_HEADS // 2
    sel = np.stack([np.stack([_select3(p, base) for p in range(npair)])
                    for base in (FOX_KC, FOX_QC)])
    ones = np.stack([_lane_ones(FOX_QC), _lane_ones(FOX_KC)])
    return jnp.asarray(sel, BF16), jnp.asarray(ones, F32)


def _fox_kernel(q_ref, k_ref, v_ref, gc_ref, ng_ref, sel_ref, ones_ref, o_ref, qa_ref, ka_ref, vt_ref,
                m_ref, l_ref, acc_ref, s_ref):
    nb, tq = q_ref.shape[0], q_ref.shape[1]
    seq = k_ref.shape[1]
    i = pl.program_id(1)
    npair = FOX_HEADS // 2
    pairs = range(nb * npair)
    lane = lax.broadcasted_iota(jnp.int32, (1, 2 * FOX_DH), 1)
    lo_half = lane < FOX_DH
    pcols = [slice((p % npair) * 2 * FOX_DH, (p % npair + 1) * 2 * FOX_DH) for p in pairs]

    @pl.when(i == 0)
    def _():
        ones_k = ones_ref[0:1, :]

        def fill(t, carry):
            rows = pl.ds(pl.multiple_of(t * tq, tq), tq)
            pieces = [_split3(gc_ref[r, rows, :] * LOG2E) for r in range(nb)]
            for p in pairs:
                r = p // npair
                ka_ref[p, rows, 0:2 * FOX_DH] = k_ref[r, rows, pcols[p]]
                ka_ref[p, rows, 2 * FOX_DH:] = (
                    ones_k - _dot(pieces[r], sel_ref[0, p % npair])).astype(BF16)
                vt_ref[p, t] = jnp.transpose(v_ref[r, rows, pcols[p]].astype(F32)).astype(BF16)
            return carry

        lax.fori_loop(0, seq // tq, fill, 0)

    qrows = pl.ds(pl.multiple_of(i * tq, tq), tq)
    pieces_q = [_split3(gc_ref[r, qrows, :] * LOG2E) for r in range(nb)]
    ones_q = ones_ref[1:2, :]
    xlane = lax.broadcasted_iota(jnp.int32, (1, FOX_X), 1)
    for p in pairs:
        qp = q_ref[p // npair, :, pcols[p]]
        qx = _dot(pieces_q[p // npair], sel_ref[1, p % npair]) + ones_q
        for half in range(2):
            rows = slice(half * tq, (half + 1) * tq)
            mine = lo_half if half == 0 else jnp.logical_not(lo_half)
            xmine = (xlane < FOX_QC[1]) if half == 0 else (xlane >= FOX_QC[1])
            qa_ref[p, rows, 0:2 * FOX_DH] = jnp.where(mine, qp, jnp.zeros((), BF16))
            qa_ref[p, rows, 2 * FOX_DH:] = jnp.where(xmine, qx, 0.0).astype(BF16)

    ki = lax.broadcasted_iota(jnp.int32, (tq, 2 * tq), 0)
    qi = jnp.bitwise_and(lax.broadcasted_iota(jnp.int32, (tq, 2 * tq), 1), tq - 1)
    causal = ki <= qi

    def scores(j):
        krows = pl.ds(pl.multiple_of(j * tq, tq), tq)
        return [_dot_nt(ka_ref[p, krows, :], qa_ref[p]) for p in pairs]

    def absorb(s, j, first):
        for p in pairs:
            sp = s[p]
            if first:
                sp = jnp.where(causal, sp, -jnp.inf)
            m_new = jnp.max(sp, 0, keepdims=True)
            if not first:
                m_prev = m_ref[p]
                m_new = jnp.maximum(m_prev, m_new)
                scale = jnp.exp2(m_prev - m_new)
            prob = jnp.exp2(sp - m_new)
            psum = jnp.sum(prob, 0, keepdims=True)
            pv = _dot(vt_ref[p, j], prob.astype(BF16))
            m_ref[p] = m_new
            if first:
                l_ref[p] = psum
                acc_ref[p] = pv
            else:
                l_ref[p] = scale * l_ref[p] + psum
                acc_ref[p] = scale * acc_ref[p] + pv

    last = jnp.maximum(i - 1, 0)

    def stage(slot, j_next):
        s_next = scores(jnp.minimum(j_next, last))
        for p in pairs:
            s_ref[slot, p] = s_next[p]

    def consume(slot, j):
        absorb([s_ref[slot, p] for p in pairs], j, False)

    s_diag = scores(i)
    stage(0, 0)
    absorb(s_diag, i, True)

    def body(t, carry):
        stage(1, 2 * t + 1)
        consume(0, 2 * t)
        stage(0, 2 * t + 2)
        consume(1, 2 * t + 1)
        return carry

    lax.fori_loop(0, i // 2, body, 0)

    @pl.when(i % 2 == 1)
    def _():
        consume(0, i - 1)

    ng = ng_ref[...]
    for p in pairs:
        acc = acc_ref[p]
        l = l_ref[p]
        outs = []
        for half in range(2):
            oh = (acc[half * FOX_DH:(half + 1) * FOX_DH, half * tq:(half + 1) * tq]
                  / l[:, half * tq:(half + 1) * tq])
            ms = jnp.mean(oh * oh, 0, keepdims=True)
            outs.append(oh * lax.rsqrt(ms + NORM_EPS))
        o_ref[p // npair, :, pcols[p]] = (
            jnp.transpose(jnp.concatenate(outs, axis=0)) * ng).astype(BF16)


def _fox(proj_big, gcol, norm_g2, batch, seq):
    nq = seq // FOX_TQ
    first = (GDN_QKV + GDN_WIDTH) // FOX_WIDTH
    npair = FOX_HEADS // 2
    nb = FOX_NB
    assert batch % nb == 0
    nu = nb * npair
    return pl.pallas_call(
        _fox_kernel,
        grid=(batch // nb, nq),
        in_specs=[
            pl.BlockSpec((nb, FOX_TQ, FOX_WIDTH), lambda g, i: (g, i, first)),
            pl.BlockSpec((nb, seq, FOX_WIDTH), lambda g, i: (g, 0, first + 1)),
            pl.BlockSpec((nb, seq, FOX_WIDTH), lambda g, i: (g, 0, first + 2)),
            pl.BlockSpec((nb, seq, N_SMALL), lambda g, i: (g, 0, 0)),
            _resident((1, 2 * FOX_DH)),
            _resident((2, npair, 3 * N_SMALL, FOX_X)),
            _resident((2, FOX_X)),
        ],
        out_specs=pl.BlockSpec((nb, FOX_TQ, FOX_WIDTH), lambda g, i: (g, i, 0)),
        out_shape=jax.ShapeDtypeStruct((batch, seq, FOX_WIDTH), BF16),
        scratch_shapes=[
            pltpu.VMEM((nu, 2 * FOX_TQ, 2 * FOX_DH + FOX_X), BF16),
            pltpu.VMEM((nu, seq, 2 * FOX_DH + FOX_X), BF16),
            pltpu.VMEM((nu, nq, 2 * FOX_DH, FOX_TQ), BF16),
            pltpu.VMEM((nu, 1, 2 * FOX_TQ), F32),
            pltpu.VMEM((nu, 1, 2 * FOX_TQ), F32),
            pltpu.VMEM((nu, 2 * FOX_DH, 2 * FOX_TQ), F32),
            pltpu.VMEM((2, nu, FOX_TQ, 2 * FOX_TQ), F32),
        ],
        compiler_params=pltpu.CompilerParams(
            dimension_semantics=("arbitrary", "arbitrary"), vmem_limit_bytes=VMEM_LIMIT),
        name="fox",
    )(proj_big, proj_big, proj_big, gcol, norm_g2, *_fox_constants())


def _tail_kernel(x_ref, p_ref, og_ref, of_ref, lin_g, lin_b, wo_ref, l1g, l1b, wu_ref, wd_ref,
                 wp_ref, wg_ref, bg_ref, l2g, l2b, o_ref):
    nparts = x_ref.shape[0] // TAIL_PART
    parts = [slice(r * TAIL_PART, (r + 1) * TAIL_PART) for r in range(nparts)]

    def head(rows):
        h = _layer_norm(x_ref[rows, :], lin_g[...], lin_b[...])
        mix = (_dot(og_ref[rows, :], wo_ref[0:GDN_WIDTH, :])
               + _dot(of_ref[rows, :], wo_ref[GDN_WIDTH:, :]))
        return _layer_norm(ALPHA * h + mix, l1g[...], l1b[...])

    def mlp(rows, h1):
        h1b = h1.astype(BF16)
        gate = _sigmoid(_dot(h1b, wg_ref[...]) + bg_ref[...])
        acc = ALPHA * h1 + _dot(p_ref[rows, :].astype(BF16), wp_ref[...]) * gate
        for j in range(D_FF // TAIL_TF):
            cols = slice(j * TAIL_TF, (j + 1) * TAIL_TF)
            a = jnp.maximum(_dot(h1b, wu_ref[:, cols]), 0.0)
            acc = acc + _dot((a * a).astype(BF16), wd_ref[cols, :])
        return acc

    h1 = head(parts[0])
    for r in range(nparts):
        h1_next = head(parts[r + 1]) if r + 1 < nparts else None
        o_ref[parts[r], :] = _layer_norm(mlp(parts[r], h1), l2g[...], l2b[...])
        h1 = h1_next


def _tail(x2, p2, o_gdn, o_fox, lin_g, lin_b, w_out, l1g, l1b, w_up, w_down, w_ple, w_gate,
          b_gate, l2g, l2b):
    n = x2.shape[0]
    row = lambda width: pl.BlockSpec((TAIL_TM, width), lambda i: (i, 0))
    vec = _resident((1, D_MODEL))
    return pl.pallas_call(
        _tail_kernel,
        grid=(n // TAIL_TM,),
        in_specs=[
            row(D_MODEL), row(D_PLE), row(GDN_WIDTH), row(FOX_WIDTH),
            vec, vec, _resident((D_MODEL, D_MODEL)), vec, vec,
            _resident((D_MODEL, D_FF)), _resident((D_FF, D_MODEL)),
            _resident((D_PLE, D_MODEL)), _resident((D_MODEL, D_MODEL)), vec, vec, vec,
        ],
        out_specs=row(D_MODEL),
        out_shape=jax.ShapeDtypeStruct((n, D_MODEL), F32),
        compiler_params=pltpu.CompilerParams(
            dimension_semantics=("arbitrary",), vmem_limit_bytes=VMEM_LIMIT),
        name="tail",
    )(x2, p2, o_gdn, o_fox, lin_g, lin_b, w_out, l1g, l1b, w_up, w_down, w_ple, w_gate,
      b_gate, l2g, l2b)


def kernel(x, p, ln_in_g, ln_in_b, w_in, conv_w, a_log, dt_bias, gdn_norm_g, b_f, fox_norm_g,
           w_out, ln1_g, ln1_b, w_up, w_down, w_ple, w_ple_gate, b_ple_gate, ln2_g, ln2_b):
    batch, seq, _ = x.shape
    assert x.shape[2] == D_MODEL and w_in.shape[0] == 1
    assert seq % FOX_TQ == 0 and seq % GDN_TB == 0 and (batch * seq) % PROJ_TM == 0
    n = batch * seq
    x2 = x.reshape(n, D_MODEL)
    p2 = p[0].reshape(n, D_PLE)
    row = lambda a: a.reshape(1, -1).astype(F32)

    w0 = w_in[0]
    w_big = jnp.concatenate([w0[:, :OFF_BETA], w0[:, OFF_FOX:OFF_F]], axis=1).astype(BF16)
    w_decay = w0[:, OFF_BETA + GDN_HEADS:OFF_FOX]
    n_gate = LANE_GTOT + GDN_HEADS
    w_small = jnp.concatenate(
        [w0[:, OFF_BETA:OFF_FOX], w0[:, OFF_F:], w_decay, jnp.zeros((D_MODEL, N_SMALL - n_gate), F32)],
        axis=1).astype(BF16)
    zeros4 = jnp.zeros((GDN_HEADS,), F32)
    pad = jnp.zeros((N_SMALL - n_gate,), F32)
    pvec = jnp.zeros((8, N_SMALL), F32)
    pvec = pvec.at[0].set(jnp.concatenate([zeros4, dt_bias[0], b_f[0], dt_bias[0], pad]))
    pvec = pvec.at[1].set(
        jnp.concatenate([zeros4, a_log[0], jnp.zeros((FOX_HEADS,), F32), a_log[0], pad]))

    proj_big, small = _proj(x2, row(ln_in_g), row(ln_in_b), w_big, w_small, conv_w[0], seq)
    gcol, grow = _gates(small, pvec, batch, seq)

    proj3 = proj_big.reshape(batch, seq, N_BIG)
    gcol3 = gcol.reshape(batch, seq, N_SMALL)
    o_gdn, w_out_b, w_up_b, w_down_b, w_ple_b, w_gate_b = _gdn(
        proj3, gcol3, grow, row(gdn_norm_g[0]), batch, seq,
        (w_out[0], w_up[0], w_down[0], w_ple[0], w_ple_gate[0]))
    o_gdn = o_gdn.reshape(n, GDN_WIDTH)
    o_fox = _fox(proj3, gcol3, row(jnp.tile(fox_norm_g[0], 2)), batch, seq).reshape(n, FOX_WIDTH)

    out = _tail(x2, p2, o_gdn, o_fox, row(ln_in_g), row(ln_in_b), w_out_b,
                row(ln1_g[0]), row(ln1_b[0]), w_up_b, w_down_b, w_ple_b, w_gate_b,
                row(b_ple_gate[0]), row(ln2_g[0]), row(ln2_b[0]))
    return out.reshape(batch, seq, D_MODEL)
```

```python
import functools

import jax
import jax.numpy as jnp
import numpy as np
from jax import lax
from jax.experimental import pallas as pl
from jax.experimental.pallas import tpu as pltpu

F32 = jnp.float32
BF16 = jnp.bfloat16

D_MODEL = 1024
CHUNK = 64
GDN_HEADS = 4
GDN_DK = 128
GDN_DV = 128
GDN_QK = GDN_HEADS * GDN_DK
GDN_QKV = 3 * GDN_QK
GDN_WIDTH = GDN_HEADS * GDN_DV
FOX_HEADS = 8
FOX_DH = 64
FOX_WIDTH = FOX_HEADS * FOX_DH
CONV_W = 4
D_FF = 4 * D_MODEL
D_PLE = 256
LN_EPS = 1e-5
NORM_EPS = 1e-6
ALPHA = 2.0 ** 0.25

OFF_Z = GDN_QKV
OFF_BETA = OFF_Z + GDN_WIDTH
OFF_FOX = OFF_BETA + 2 * GDN_HEADS
OFF_F = OFF_FOX + 3 * FOX_WIDTH
D_IN = OFF_F + FOX_HEADS
N_BIG = GDN_QKV + GDN_WIDTH + 3 * FOX_WIDTH
N_SMALL = 128
FOX_Q0 = GDN_QKV + GDN_WIDTH
LOG2E = 1.4426950408889634
FOX_QSCALE = FOX_DH ** -0.5 * LOG2E
LANE_BETA = 0
LANE_GAM = GDN_HEADS
LANE_C = 2 * GDN_HEADS
LANE_GTOT = LANE_C + FOX_HEADS

VMEM_LIMIT = 56 * 1024 * 1024

PROJ_TM = 512
PROJ_TN = 256
PROJ_AHEAD = 1
GATE_TB = 256
GDN_TB = 256
GDN_NB = 4
FOX_TQ = 256
FOX_NB = 2
TAIL_TM = 512
TAIL_PART = 256
TAIL_TF = 1024


def _layer_norm(x, g, b):
    mu = jnp.mean(x, -1, keepdims=True)
    xc = x - mu
    var = jnp.mean(xc * xc, -1, keepdims=True)
    return xc * lax.rsqrt(var + LN_EPS) * g + b


def _softplus(x):
    return jnp.maximum(x, 0.0) + jnp.log(1.0 + jnp.exp(-jnp.abs(x)))


def _sigmoid(x):
    return 1.0 / (1.0 + jnp.exp(-x))


def _dot(a, b):
    return jnp.dot(a, b, preferred_element_type=F32)


def _dot_nt(a, b):
    return lax.dot_general(a, b, (((1,), (1,)), ((), ())), preferred_element_type=F32)


def _resident(shape):
    return pl.BlockSpec(shape, lambda *_: (0,) * len(shape), pipeline_mode=pl.Buffered(1))


def _proj_kernel(x_ref, g_ref, b_ref, w_ref, cw_ref, ob_ref, os_ref, halo_ref, wb_ref, ws_ref, *,
                 tiles_per_seq):
    tm = x_ref.shape[0]
    halo = halo_ref.shape[0]

    @pl.when(pl.program_id(0) == 0)
    def _():
        r = lax.broadcasted_iota(jnp.int32, (N_SMALL, N_SMALL), 0)
        c = lax.broadcasted_iota(jnp.int32, (N_SMALL, N_SMALL), 1)
        same = jnp.where(r == c, 1.0, 0.0)
        again = jnp.where(r + (LANE_GTOT - LANE_GAM) == c, 1.0, 0.0)
        pick_decay = jnp.where(c < LANE_C, same,
                               jnp.where(c < LANE_GTOT, 0.0,
                                         jnp.where(c < LANE_GTOT + GDN_HEADS, again, 0.0))).astype(BF16)
        tail0 = N_SMALL - FOX_HEADS - LANE_C
        pick_forget = jnp.where(c < LANE_C, 0.0,
                                jnp.where(c < LANE_GTOT, jnp.where(r == c + tail0, 1.0, 0.0),
                                          0.0)).astype(BF16)
        step = 128
        for r0 in range(0, D_MODEL, step):
            rows = slice(r0, r0 + step)
            wb_ref[rows, 0:OFF_BETA] = w_ref[rows, 0:OFF_BETA].astype(BF16)
            wb_ref[rows, OFF_BETA:N_BIG] = w_ref[rows, OFF_FOX:OFF_F].astype(BF16)
            g1 = w_ref[rows, OFF_BETA:OFF_BETA + N_SMALL].astype(BF16)
            g2 = w_ref[rows, D_IN - N_SMALL:D_IN].astype(BF16)
            ws_ref[rows, :] = (_dot(g1, pick_decay) + _dot(g2, pick_forget)).astype(BF16)

    @pl.when(pl.program_id(0) % tiles_per_seq == 0)
    def _():
        halo_ref[...] = jnp.zeros_like(halo_ref)

    h = _layer_norm(x_ref[...], g_ref[...], b_ref[...]).astype(BF16)
    nchunk = N_BIG // PROJ_TN
    nconv = GDN_QKV // PROJ_TN
    order = [c for pair in zip(range(nconv), range(nconv, 2 * nconv)) for c in pair]
    order += list(range(2 * nconv, nchunk))
    def chunk_dot(c):
        return _dot(h, wb_ref[:, c * PROJ_TN:(c + 1) * PROJ_TN])

    inflight = [chunk_dot(c) for c in order[:PROJ_AHEAD]]
    for pos, j in enumerate(order):
        cols = slice(j * PROJ_TN, (j + 1) * PROJ_TN)
        acc = inflight.pop(0)
        if pos + PROJ_AHEAD < nchunk:
            inflight.append(chunk_dot(order[pos + PROJ_AHEAD]))
        if (j + 1) * PROJ_TN <= GDN_QKV:
            ext = jnp.concatenate([halo_ref[:, cols], acc], axis=0)
            y = acc * cw_ref[CONV_W - 1:CONV_W, cols]
            for d in range(1, CONV_W):
                y = y + pltpu.roll(ext, d, 0)[halo:] * cw_ref[CONV_W - 1 - d:CONV_W - d, cols]
            halo_ref[:, cols] = acc[tm - halo:]
            acc = y * _sigmoid(y)
        elif FOX_Q0 <= j * PROJ_TN < FOX_Q0 + FOX_WIDTH:
            acc = acc * FOX_QSCALE
        ob_ref[:, cols] = acc.astype(BF16)
    os_ref[...] = _dot(h, ws_ref[...])


def _proj(x2, ln_g, ln_b, w_in, conv_w, seq):
    n = x2.shape[0]
    assert GDN_QKV % PROJ_TN == 0 and seq % PROJ_TM == 0
    assert w_in.shape == (D_MODEL, D_IN) and OFF_BETA % N_SMALL == 0 and (OFF_F - 8) % N_SMALL == 0
    return pl.pallas_call(
        functools.partial(_proj_kernel, tiles_per_seq=seq // PROJ_TM),
        grid=(n // PROJ_TM,),
        in_specs=[
            pl.BlockSpec((PROJ_TM, D_MODEL), lambda i: (i, 0)),
            _resident((1, D_MODEL)),
            _resident((1, D_MODEL)),
            _resident((D_MODEL, D_IN)),
            _resident((CONV_W, GDN_QKV)),
        ],
        out_specs=[
            pl.BlockSpec((PROJ_TM, N_BIG), lambda i: (i, 0)),
            pl.BlockSpec((PROJ_TM, N_SMALL), lambda i: (i, 0)),
        ],
        out_shape=[
            jax.ShapeDtypeStruct((n, N_BIG), BF16),
            jax.ShapeDtypeStruct((n, N_SMALL), F32),
        ],
        scratch_shapes=[
            pltpu.VMEM((8, GDN_QKV), F32),
            pltpu.VMEM((D_MODEL, N_BIG), BF16),
            pltpu.VMEM((D_MODEL, N_SMALL), BF16),
        ],
        compiler_params=pltpu.CompilerParams(
            dimension_semantics=("arbitrary",), vmem_limit_bytes=VMEM_LIMIT),
        name="proj",
    )(x2, ln_g, ln_b, w_in, conv_w)


def _gates_kernel(s_ref, pv_ref, o_ref, gr_ref, *, seq):
    lane = lax.broadcasted_iota(jnp.int32, (1, N_SMALL), 1)
    bias = pv_ref[0:1, :]
    neg_a = -jnp.exp(pv_ref[1:2, :])
    r = lax.broadcasted_iota(jnp.int32, (GATE_TB, GATE_TB), 0)
    c = lax.broadcasted_iota(jnp.int32, (GATE_TB, GATE_TB), 1)
    l_full = jnp.where(r >= c, 1.0, 0.0).astype(F32)
    same_chunk = jnp.right_shift(r, 6) == jnp.right_shift(c, 6)
    l_chunk = jnp.where(same_chunk, l_full, 0.0)
    l_total = jnp.where(same_chunk, 1.0, 0.0).astype(F32)
    l_decay = jnp.concatenate([l_chunk, l_total], axis=0).astype(BF16)
    l_forget = l_full.astype(BF16)

    def pieces_sum(a):
        return a[:, 0:N_SMALL] + a[:, N_SMALL:2 * N_SMALL] + a[:, 2 * N_SMALL:]

    carry = jnp.zeros((1, N_SMALL), F32)
    for t in range(seq // GATE_TB):
        rows = slice(t * GATE_TB, (t + 1) * GATE_TB)
        x = s_ref[rows, :] + bias
        beta = _sigmoid(x)
        log_g = neg_a * _softplus(x)
        log_f = -_softplus(-x)
        decay = pieces_sum(_dot(l_decay, _split3(log_g)))
        gam, gtot = decay[:GATE_TB], decay[GATE_TB:]
        cum = pieces_sum(_dot(l_forget, _split3(log_f))) + carry
        carry = cum[GATE_TB - 1:GATE_TB, :]
        o_ref[rows, :] = jnp.where(
            lane < LANE_GAM, beta,
            jnp.where(lane < LANE_C, gam, jnp.where(lane < LANE_GTOT, cum, gtot)))
        gr_ref[0, t] = jnp.transpose(gam)[LANE_GAM:LANE_GAM + GDN_HEADS, :]


def _gates(small, pvec, batch, seq):
    assert GATE_TB == GDN_TB
    nt = seq // GATE_TB
    return pl.pallas_call(
        functools.partial(_gates_kernel, seq=seq),
        grid=(batch,),
        in_specs=[
            pl.BlockSpec((seq, N_SMALL), lambda b: (b, 0)),
            _resident((8, N_SMALL)),
        ],
        out_specs=[
            pl.BlockSpec((seq, N_SMALL), lambda b: (b, 0)),
            pl.BlockSpec((1, nt, GDN_HEADS, GATE_TB), lambda b: (b, 0, 0, 0)),
        ],
        out_shape=[
            jax.ShapeDtypeStruct((batch * seq, N_SMALL), F32),
            jax.ShapeDtypeStruct((batch, nt, GDN_HEADS, GATE_TB), F32),
        ],
        compiler_params=pltpu.CompilerParams(
            dimension_semantics=("arbitrary",), vmem_limit_bytes=VMEM_LIMIT),
        name="gates",
    )(small, pvec)


def _gdn_kernel(x_ref, z_ref, gc_ref, gr_ref, ng_ref, *rest, ncast):
    o_ref, s_ref = rest[ncast], rest[2 * ncast + 1]
    for w_f32, w_bf16 in zip(rest[:ncast], rest[ncast + 1:2 * ncast + 1]):
        w_bf16[...] = w_f32[...].astype(BF16)

    nb, tb = x_ref.shape[0], x_ref.shape[1]
    nch = tb // CHUNK
    heads = range(nb * GDN_HEADS)
    nh = GDN_HEADS

    @pl.when(pl.program_id(1) == 0)
    def _():
        s_ref[...] = jnp.zeros_like(s_ref)

    def act(h, c0):
        c0 = c0 + (h % nh) * GDN_DK
        return x_ref[h // nh, :, c0:c0 + GDN_DK].astype(F32)

    def l2norm(v):
        return v * lax.rsqrt(jnp.sum(v * v, -1, keepdims=True) + NORM_EPS)

    ri = lax.broadcasted_iota(jnp.int32, (tb, tb), 0)
    ci = lax.broadcasted_iota(jnp.int32, (tb, tb), 1)
    same = jnp.right_shift(ri, 6) == jnp.right_shift(ci, 6)
    keep = jnp.logical_and(same, ri >= ci)
    diag = ri == ci
    pr = lax.broadcasted_iota(jnp.int32, (CHUNK, tb), 0)
    pc = lax.broadcasted_iota(jnp.int32, (CHUNK, tb), 1)
    eye_packed = jnp.where(jnp.bitwise_and(pc, CHUNK - 1) == pr, 1.0, 0.0).astype(F32)
    lane_chunk = jnp.right_shift(lax.broadcasted_iota(jnp.int32, (1, tb), 1), 6)
    row_chunk = jnp.right_shift(lax.broadcasted_iota(jnp.int32, (tb, 1), 0), 6)

    def to_bd(packed):
        return jnp.where(same, jnp.concatenate([packed] * nch, axis=0), jnp.zeros((), packed.dtype))

    def to_packed(bd):
        out = bd[0:CHUNK]
        for c in range(1, nch):
            out = out + bd[c * CHUNK:(c + 1) * CHUNK]
        return out

    ng = ng_ref[...]
    q = [l2norm(act(h, 0)) * (GDN_DK ** -0.5) for h in heads]
    k = [l2norm(act(h, GDN_QK)) for h in heads]
    v = [act(h, 2 * GDN_QK) for h in heads]

    def gate(h, lane0):
        return gc_ref[h // nh, :, lane0 + h % nh:lane0 + h % nh + 1]

    beta = [gate(h, LANE_BETA) for h in heads]
    gam = [gate(h, LANE_GAM) for h in heads]
    gtot = [gate(h, LANE_GTOT) for h in heads]
    grow = [gr_ref[h // nh, 0, h % nh:h % nh + 1, :] for h in heads]
    egam = [jnp.exp(g) for g in gam]
    kb = [a.astype(BF16) for a in k]
    decay = [jnp.exp(jnp.where(keep, gam[h] - grow[h], -jnp.inf)) for h in heads]
    gram = [_dot_nt(kb[h], kb[h]) for h in heads]
    qk = [(_dot_nt(q[h].astype(BF16), kb[h]) * decay[h]).astype(BF16) for h in heads]

    x_bd = [jnp.where(diag, 0.0, -(gram[h] * beta[h]) * decay[h]) for h in heads]
    x_p = [to_packed(a) for a in x_bd]
    p_p = [eye_packed + a for a in x_p]
    x_p = [_dot(x_p[h].astype(BF16), x_bd[h].astype(BF16)) for h in heads]
    for _ in range(4):
        w_bd = [to_bd(a.astype(BF16)) for a in x_p]
        r = [_dot(jnp.concatenate([p_p[h], x_p[h]], axis=0).astype(BF16), w_bd[h]) for h in heads]
        p_p = [p_p[h] + r[h][:CHUNK] for h in heads]
        x_p = [r[h][CHUNK:] for h in heads]
    p_p = [p_p[h] + _dot(p_p[h].astype(BF16), to_bd(x_p[h].astype(BF16))) for h in heads]

    rhs = [jnp.concatenate([v[h] * beta[h], k[h] * (beta[h] * egam[h])], axis=1).astype(BF16)
           for h in heads]
    sol = [_dot(to_bd(p_p[h].astype(BF16)), rhs[h]) for h in heads]
    u = [a[:, :GDN_DV] for a in sol]
    w = [a[:, GDN_DV:].astype(BF16) for a in sol]
    qd = [(q[h] * egam[h]).astype(BF16) for h in heads]
    kd_t = [jnp.transpose(k[h] * jnp.exp(gtot[h] - gam[h])).astype(BF16) for h in heads]
    z = [z_ref[h // nh, :, (h % nh) * GDN_DV:(h % nh + 1) * GDN_DV].astype(F32) for h in heads]
    zgate = [a * _sigmoid(a) for a in z]

    state = [s_ref[h] for h in heads]
    for c in range(nch):
        rows = slice(c * CHUNK, (c + 1) * CHUNK)
        sb = [a.astype(BF16) for a in state]
        r1 = [_dot(jnp.concatenate([w[h][rows], qd[h][rows]], axis=0), sb[h]) for h in heads]
        v_new = [u[h][rows] - r1[h][:CHUNK] for h in heads]
        v_full = [jnp.where(row_chunk == c, jnp.concatenate([a.astype(BF16)] * nch, axis=0),
                            jnp.zeros((), BF16)) for a in v_new]
        kd_c = [jnp.where(lane_chunk == c, a, jnp.zeros((), BF16)) for a in kd_t]
        r2 = [_dot(jnp.concatenate([qk[h][rows], kd_c[h]], axis=0), v_full[h]) for h in heads]
        for h in heads:
            o = r1[h][CHUNK:] + r2[h][:CHUNK]
            on = o * lax.rsqrt(jnp.mean(o * o, -1, keepdims=True) + NORM_EPS) * ng
            o_ref[h // nh, rows, (h % nh) * GDN_DV:(h % nh + 1) * GDN_DV] = (
                on * zgate[h][rows]).astype(BF16)
        state = [state[h] * jnp.exp(gtot[h][c * CHUNK:c * CHUNK + 1]) + r2[h][CHUNK:] for h in heads]
    for h in heads:
        s_ref[h] = state[h]


def _gdn(proj_big, gcol, grow, norm_g, batch, seq, cast):
    nt = seq // GDN_TB
    nb = GDN_NB
    assert batch % nb == 0
    steps = (batch // nb) * nt
    assert all(w.shape[0] % (16 * steps) == 0 for w in cast)
    cast_specs = [pl.BlockSpec((w.shape[0] // steps, w.shape[1]), lambda g, t: (g * nt + t, 0))
                  for w in cast]
    return pl.pallas_call(
        functools.partial(_gdn_kernel, ncast=len(cast)),
        grid=(batch // nb, nt),
        in_specs=[
            pl.BlockSpec((nb, GDN_TB, GDN_QKV), lambda g, t: (g, t, 0)),
            pl.BlockSpec((nb, GDN_TB, GDN_WIDTH), lambda g, t: (g, t, OFF_Z // GDN_WIDTH)),
            pl.BlockSpec((nb, GDN_TB, N_SMALL), lambda g, t: (g, t, 0)),
            pl.BlockSpec((nb, 1, GDN_HEADS, GDN_TB), lambda g, t: (g, t, 0, 0)),
            _resident((1, GDN_DV)),
        ] + cast_specs,
        out_specs=[pl.BlockSpec((nb, GDN_TB, GDN_WIDTH), lambda g, t: (g, t, 0))] + cast_specs,
        out_shape=[jax.ShapeDtypeStruct((batch, seq, GDN_WIDTH), BF16)]
        + [jax.ShapeDtypeStruct(w.shape, BF16) for w in cast],
        scratch_shapes=[
            pltpu.VMEM((nb * GDN_HEADS, GDN_DK, GDN_DV), F32),
        ],
        compiler_params=pltpu.CompilerParams(
            dimension_semantics=("arbitrary", "arbitrary"), vmem_limit_bytes=VMEM_LIMIT),
        name="gdn",
    )(proj_big, proj_big, gcol, grow, norm_g, *cast)


FOX_X = 128
FOX_QC = (0, 6)
FOX_KC = (3, 9)


def _split3(c):
    hi = c.astype(BF16)
    r1 = c - hi.astype(F32)
    mid = r1.astype(BF16)
    lo = (r1 - mid.astype(F32)).astype(BF16)
    return jnp.concatenate([hi, mid, lo], axis=1)


def _select3(pair, base):
    e = np.zeros((3 * N_SMALL, FOX_X), np.float32)
    for d in range(3):
        for head in range(2):
            e[d * N_SMALL + LANE_C + 2 * pair + head, base[head] + d] = 1.0
    return e


def _lane_ones(first):
    out = np.zeros((FOX_X,), np.float32)
    for f in first:
        out[f:f + 3] = 1.0
    return out


def _fox_constants():
    npair = FOX_HEADS // 2
    sel = np.stack([np.stack([_select3(p, base) for p in range(npair)])
                    for base in (FOX_KC, FOX_QC)])
    ones = np.stack([_lane_ones(FOX_QC), _lane_ones(FOX_KC)])
    return jnp.asarray(sel, BF16), jnp.asarray(ones, F32)


def _fox_kernel(q_ref, k_ref, v_ref, gc_ref, ng_ref, sel_ref, ones_ref, o_ref, qa_ref, ka_ref, vt_ref,
                m_ref, l_ref, acc_ref, s_ref):
    nb, tq = q_ref.shape[0], q_ref.shape[1]
    seq = k_ref.shape[1]
    i = pl.program_id(1)
    npair = FOX_HEADS // 2
    pairs = range(nb * npair)
    lane = lax.broadcasted_iota(jnp.int32, (1, 2 * FOX_DH), 1)
    lo_half = lane < FOX_DH
    pcols = [slice((p % npair) * 2 * FOX_DH, (p % npair + 1) * 2 * FOX_DH) for p in pairs]

    @pl.when(i == 0)
    def _():
        ones_k = ones_ref[0:1, :]

        def fill(t, carry):
            rows = pl.ds(pl.multiple_of(t * tq, tq), tq)
            pieces = [_split3(gc_ref[r, rows, :] * LOG2E) for r in range(nb)]
            for p in pairs:
                r = p // npair
                ka_ref[p, rows, 0:2 * FOX_DH] = k_ref[r, rows, pcols[p]]
                ka_ref[p, rows, 2 * FOX_DH:] = (
                    ones_k - _dot(pieces[r], sel_ref[0, p % npair])).astype(BF16)
                vt_ref[p, t] = jnp.transpose(v_ref[r, rows, pcols[p]].astype(F32)).astype(BF16)
            return carry

        lax.fori_loop(0, seq // tq, fill, 0)

    qrows = pl.ds(pl.multiple_of(i * tq, tq), tq)
    pieces_q = [_split3(gc_ref[r, qrows, :] * LOG2E) for r in range(nb)]
    ones_q = ones_ref[1:2, :]
    xlane = lax.broadcasted_iota(jnp.int32, (1, FOX_X), 1)
    for p in pairs:
        qp = q_ref[p // npair, :, pcols[p]]
        qx = _dot(pieces_q[p // npair], sel_ref[1, p % npair]) + ones_q
        for half in range(2):
            rows = slice(half * tq, (half + 1) * tq)
            mine = lo_half if half == 0 else jnp.logical_not(lo_half)
            xmine = (xlane < FOX_QC[1]) if half == 0 else (xlane >= FOX_QC[1])
            qa_ref[p, rows, 0:2 * FOX_DH] = jnp.where(mine, qp, jnp.zeros((), BF16))
            qa_ref[p, rows, 2 * FOX_DH:] = jnp.where(xmine, qx, 0.0).astype(BF16)

    ki = lax.broadcasted_iota(jnp.int32, (tq, 2 * tq), 0)
    qi = jnp.bitwise_and(lax.broadcasted_iota(jnp.int32, (tq, 2 * tq), 1), tq - 1)
    causal = ki <= qi

    def scores(j):
        krows = pl.ds(pl.multiple_of(j * tq, tq), tq)
        return [_dot_nt(ka_ref[p, krows, :], qa_ref[p]) for p in pairs]

    def absorb(s, j, first):
        for p in pairs:
            sp = s[p]
            if first:
                sp = jnp.where(causal, sp, -jnp.inf)
            m_new = jnp.max(sp, 0, keepdims=True)
            if not first:
                m_prev = m_ref[p]
                m_new = jnp.maximum(m_prev, m_new)
                scale = jnp.exp2(m_prev - m_new)
            prob = jnp.exp2(sp - m_new)
            psum = jnp.sum(prob, 0, keepdims=True)
            pv = _dot(vt_ref[p, j], prob.astype(BF16))
            m_ref[p] = m_new
            if first:
                l_ref[p] = psum
                acc_ref[p] = pv
            else:
                l_ref[p] = scale * l_ref[p] + psum
                acc_ref[p] = scale * acc_ref[p] + pv

    last = jnp.maximum(i - 1, 0)

    def stage(slot, j_next):
        s_next = scores(jnp.minimum(j_next, last))
        for p in pairs:
            s_ref[slot, p] = s_next[p]

    def consume(slot, j):
        absorb([s_ref[slot, p] for p in pairs], j, False)

    s_diag = scores(i)
    stage(0, 0)
    absorb(s_diag, i, True)

    def body(t, carry):
        stage(1, 2 * t + 1)
        consume(0, 2 * t)
        stage(0, 2 * t + 2)
        consume(1, 2 * t + 1)
        return carry

    lax.fori_loop(0, i // 2, body, 0)

    @pl.when(i % 2 == 1)
    def _():
        consume(0, i - 1)

    ng = ng_ref[...]
    for p in pairs:
        acc = acc_ref[p]
        l = l_ref[p]
        outs = []
        for half in range(2):
            oh = (acc[half * FOX_DH:(half + 1) * FOX_DH, half * tq:(half + 1) * tq]
                  / l[:, half * tq:(half + 1) * tq])
            ms = jnp.mean(oh * oh, 0, keepdims=True)
            outs.append(oh * lax.rsqrt(ms + NORM_EPS))
        o_ref[p // npair, :, pcols[p]] = (
            jnp.transpose(jnp.concatenate(outs, axis=0)) * ng).astype(BF16)


def _fox(proj_big, gcol, norm_g2, batch, seq):
    nq = seq // FOX_TQ
    first = (GDN_QKV + GDN_WIDTH) // FOX_WIDTH
    npair = FOX_HEADS // 2
    nb = FOX_NB
    assert batch % nb == 0
    nu = nb * npair
    return pl.pallas_call(
        _fox_kernel,
        grid=(batch // nb, nq),
        in_specs=[
            pl.BlockSpec((nb, FOX_TQ, FOX_WIDTH), lambda g, i: (g, i, first)),
            pl.BlockSpec((nb, seq, FOX_WIDTH), lambda g, i: (g, 0, first + 1)),
            pl.BlockSpec((nb, seq, FOX_WIDTH), lambda g, i: (g, 0, first + 2)),
            pl.BlockSpec((nb, seq, N_SMALL), lambda g, i: (g, 0, 0)),
            _resident((1, 2 * FOX_DH)),
            _resident((2, npair, 3 * N_SMALL, FOX_X)),
            _resident((2, FOX_X)),
        ],
        out_specs=pl.BlockSpec((nb, FOX_TQ, FOX_WIDTH), lambda g, i: (g, i, 0)),
        out_shape=jax.ShapeDtypeStruct((batch, seq, FOX_WIDTH), BF16),
        scratch_shapes=[
            pltpu.VMEM((nu, 2 * FOX_TQ, 2 * FOX_DH + FOX_X), BF16),
            pltpu.VMEM((nu, seq, 2 * FOX_DH + FOX_X), BF16),
            pltpu.VMEM((nu, nq, 2 * FOX_DH, FOX_TQ), BF16),
            pltpu.VMEM((nu, 1, 2 * FOX_TQ), F32),
            pltpu.VMEM((nu, 1, 2 * FOX_TQ), F32),
            pltpu.VMEM((nu, 2 * FOX_DH, 2 * FOX_TQ), F32),
            pltpu.VMEM((2, nu, FOX_TQ, 2 * FOX_TQ), F32),
        ],
        compiler_params=pltpu.CompilerParams(
            dimension_semantics=("arbitrary", "arbitrary"), vmem_limit_bytes=VMEM_LIMIT),
        name="fox",
    )(proj_big, proj_big, proj_big, gcol, norm_g2, *_fox_constants())


def _tail_kernel(x_ref, p_ref, og_ref, of_ref, lin_g, lin_b, wo_ref, l1g, l1b, wu_ref, wd_ref,
                 wp_ref, wg_ref, bg_ref, l2g, l2b, o_ref):
    nparts = x_ref.shape[0] // TAIL_PART
    parts = [slice(r * TAIL_PART, (r + 1) * TAIL_PART) for r in range(nparts)]

    def head(rows):
        h = _layer_norm(x_ref[rows, :], lin_g[...], lin_b[...])
        mix = (_dot(og_ref[rows, :], wo_ref[0:GDN_WIDTH, :])
               + _dot(of_ref[rows, :], wo_ref[GDN_WIDTH:, :]))
        return _layer_norm(ALPHA * h + mix, l1g[...], l1b[...])

    def mlp(rows, h1):
        h1b = h1.astype(BF16)
        gate = _sigmoid(_dot(h1b, wg_ref[...]) + bg_ref[...])
        acc = ALPHA * h1 + _dot(p_ref[rows, :].astype(BF16), wp_ref[...]) * gate
        for j in range(D_FF // TAIL_TF):
            cols = slice(j * TAIL_TF, (j + 1) * TAIL_TF)
            a = jnp.maximum(_dot(h1b, wu_ref[:, cols]), 0.0)
            acc = acc + _dot((a * a).astype(BF16), wd_ref[cols, :])
        return acc

    h1 = head(parts[0])
    for r in range(nparts):
        h1_next = head(parts[r + 1]) if r + 1 < nparts else None
        o_ref[parts[r], :] = _layer_norm(mlp(parts[r], h1), l2g[...], l2b[...])
        h1 = h1_next


def _tail(x2, p2, o_gdn, o_fox, lin_g, lin_b, w_out, l1g, l1b, w_up, w_down, w_ple, w_gate,
          b_gate, l2g, l2b):
    n = x2.shape[0]
    row = lambda width: pl.BlockSpec((TAIL_TM, width), lambda i: (i, 0))
    vec = _resident((1, D_MODEL))
    return pl.pallas_call(
        _tail_kernel,
        grid=(n // TAIL_TM,),
        in_specs=[
            row(D_MODEL), row(D_PLE), row(GDN_WIDTH), row(FOX_WIDTH),
            vec, vec, _resident((D_MODEL, D_MODEL)), vec, vec,
            _resident((D_MODEL, D_FF)), _resident((D_FF, D_MODEL)),
            _resident((D_PLE, D_MODEL)), _resident((D_MODEL, D_MODEL)), vec, vec, vec,
        ],
        out_specs=row(D_MODEL),
        out_shape=jax.ShapeDtypeStruct((n, D_MODEL), F32),
        compiler_params=pltpu.CompilerParams(
            dimension_semantics=("arbitrary",), vmem_limit_bytes=VMEM_LIMIT),
        name="tail",
    )(x2, p2, o_gdn, o_fox, lin_g, lin_b, w_out, l1g, l1b, w_up, w_down, w_ple, w_gate,
      b_gate, l2g, l2b)


def kernel(x, p, ln_in_g, ln_in_b, w_in, conv_w, a_log, dt_bias, gdn_norm_g, b_f, fox_norm_g,
           w_out, ln1_g, ln1_b, w_up, w_down, w_ple, w_ple_gate, b_ple_gate, ln2_g, ln2_b):
    batch, seq, _ = x.shape
    assert x.shape[2] == D_MODEL and w_in.shape[0] == 1
    assert seq % FOX_TQ == 0 and seq % GDN_TB == 0 and (batch * seq) % PROJ_TM == 0
    n = batch * seq
    x2 = x.reshape(n, D_MODEL)
    p2 = p[0].reshape(n, D_PLE)
    row = lambda a: a.reshape(1, -1).astype(F32)

    n_gate = LANE_GTOT + GDN_HEADS
    zeros4 = jnp.zeros((GDN_HEADS,), F32)
    pad = jnp.zeros((N_SMALL - n_gate,), F32)
    pvec = jnp.zeros((8, N_SMALL), F32)
    pvec = pvec.at[0].set(jnp.concatenate([zeros4, dt_bias[0], b_f[0], dt_bias[0], pad]))
    pvec = pvec.at[1].set(
        jnp.concatenate([zeros4, a_log[0], jnp.zeros((FOX_HEADS,), F32), a_log[0], pad]))

    proj_big, small = _proj(x2, row(ln_in_g), row(ln_in_b), w_in[0], conv_w[0], seq)
    gcol, grow = _gates(small, pvec, batch, seq)

    proj3 = proj_big.reshape(batch, seq, N_BIG)
    gcol3 = gcol.reshape(batch, seq, N_SMALL)
    o_gdn, w_out_b, w_up_b, w_down_b, w_ple_b, w_gate_b = _gdn(
        proj3, gcol3, grow, row(gdn_norm_g[0]), batch, seq,
        (w_out[0], w_up[0], w_down[0], w_ple[0], w_ple_gate[0]))
    o_gdn = o_gdn.reshape(n, GDN_WIDTH)
    o_fox = _fox(proj3, gcol3, row(jnp.tile(fox_norm_g[0], 2)), batch, seq).reshape(n, FOX_WIDTH)

    out = _tail(x2, p2, o_gdn, o_fox, row(ln_in_g), row(ln_in_b), w_out_b,
                row(ln1_g[0]), row(ln1_b[0]), w_up_b, w_down_b, w_ple_b, w_gate_b,
                row(b_ple_gate[0]), row(ln2_g[0]), row(ln2_b[0]))
    return out.reshape(batch, seq, D_MODEL)
```

```python
import functools

import jax
import jax.numpy as jnp
import numpy as np
from jax import lax
from jax.experimental import pallas as pl
from jax.experimental.pallas import tpu as pltpu

F32 = jnp.float32
BF16 = jnp.bfloat16

D_MODEL = 1024
CHUNK = 64
GDN_HEADS = 4
GDN_DK = 128
GDN_DV = 128
GDN_QK = GDN_HEADS * GDN_DK
GDN_QKV = 3 * GDN_QK
GDN_WIDTH = GDN_HEADS * GDN_DV
FOX_HEADS = 8
FOX_DH = 64
FOX_WIDTH = FOX_HEADS * FOX_DH
CONV_W = 4
D_FF = 4 * D_MODEL
D_PLE = 256
LN_EPS = 1e-5
NORM_EPS = 1e-6
ALPHA = 2.0 ** 0.25

OFF_Z = GDN_QKV
OFF_BETA = OFF_Z + GDN_WIDTH
OFF_FOX = OFF_BETA + 2 * GDN_HEADS
OFF_F = OFF_FOX + 3 * FOX_WIDTH
D_IN = OFF_F + FOX_HEADS
N_BIG = GDN_QKV + GDN_WIDTH + 3 * FOX_WIDTH
N_SMALL = 128
FOX_Q0 = GDN_QKV + GDN_WIDTH
LOG2E = 1.4426950408889634
FOX_QSCALE = FOX_DH ** -0.5 * LOG2E
LANE_BETA = 0
LANE_GAM = GDN_HEADS
LANE_C = 2 * GDN_HEADS
LANE_GTOT = LANE_C + FOX_HEADS + GDN_HEADS

VMEM_LIMIT = 56 * 1024 * 1024

PROJ_TM = 512
PROJ_TN = 256
PROJ_AHEAD = 1
GATE_TB = 256
GDN_TB = 256
GDN_NB = 4
FOX_TQ = 256
FOX_NB = 2
TAIL_TM = 512
TAIL_PART = 256
TAIL_TF = 1024


def _layer_norm(x, g, b):
    mu = jnp.mean(x, -1, keepdims=True)
    xc = x - mu
    var = jnp.mean(xc * xc, -1, keepdims=True)
    return xc * lax.rsqrt(var + LN_EPS) * g + b


def _softplus(x):
    return jnp.maximum(x, 0.0) + jnp.log(1.0 + jnp.exp(-jnp.abs(x)))


def _sigmoid(x):
    return 1.0 / (1.0 + jnp.exp(-x))


def _dot(a, b):
    return jnp.dot(a, b, preferred_element_type=F32)


def _dot_nt(a, b):
    return lax.dot_general(a, b, (((1,), (1,)), ((), ())), preferred_element_type=F32)


def _resident(shape):
    return pl.BlockSpec(shape, lambda *_: (0,) * len(shape), pipeline_mode=pl.Buffered(1))


def _proj_kernel(x_ref, g_ref, b_ref, w_ref, cw_ref, ob_ref, os_ref, halo_ref, wb_ref, ws_ref, *,
                 tiles_per_seq):
    tm = x_ref.shape[0]
    halo = halo_ref.shape[0]

    @pl.when(pl.program_id(0) == 0)
    def _():
        step = 256
        for r0 in range(0, OFF_BETA, step):
            wb_ref[r0:r0 + step, :] = w_ref[r0:r0 + step, :].astype(BF16)
        for r0 in range(0, N_BIG - OFF_BETA, step):
            wb_ref[OFF_BETA + r0:OFF_BETA + r0 + step, :] = (
                w_ref[OFF_FOX + r0:OFF_FOX + r0 + step, :].astype(BF16))
        decay = w_ref[OFF_BETA:OFF_FOX, :]
        gates = [decay, w_ref[OFF_F:D_IN, :], decay,
                 jnp.zeros((N_SMALL - LANE_GTOT - GDN_HEADS, D_MODEL), F32)]
        ws_ref[...] = jnp.concatenate(gates, axis=0).astype(BF16)

    @pl.when(pl.program_id(0) % tiles_per_seq == 0)
    def _():
        halo_ref[...] = jnp.zeros_like(halo_ref)

    h = _layer_norm(x_ref[...], g_ref[...], b_ref[...]).astype(BF16)
    nchunk = N_BIG // PROJ_TN
    nconv = GDN_QKV // PROJ_TN
    order = [c for pair in zip(range(nconv), range(nconv, 2 * nconv)) for c in pair]
    order += list(range(2 * nconv, nchunk))
    def chunk_dot(c):
        return _dot_nt(h, wb_ref[c * PROJ_TN:(c + 1) * PROJ_TN, :])

    inflight = [chunk_dot(c) for c in order[:PROJ_AHEAD]]
    for pos, j in enumerate(order):
        cols = slice(j * PROJ_TN, (j + 1) * PROJ_TN)
        acc = inflight.pop(0)
        if pos + PROJ_AHEAD < nchunk:
            inflight.append(chunk_dot(order[pos + PROJ_AHEAD]))
        if (j + 1) * PROJ_TN <= GDN_QKV:
            ext = jnp.concatenate([halo_ref[:, cols], acc], axis=0)
            y = acc * cw_ref[CONV_W - 1:CONV_W, cols]
            for d in range(1, CONV_W):
                y = y + pltpu.roll(ext, d, 0)[halo:] * cw_ref[CONV_W - 1 - d:CONV_W - d, cols]
            halo_ref[:, cols] = acc[tm - halo:]
            acc = y * _sigmoid(y)
        elif FOX_Q0 <= j * PROJ_TN < FOX_Q0 + FOX_WIDTH:
            acc = acc * FOX_QSCALE
        ob_ref[:, cols] = acc.astype(BF16)
    os_ref[...] = _dot_nt(h, ws_ref[...])


def _proj(x2, ln_g, ln_b, w_in, conv_w, seq):
    n = x2.shape[0]
    assert GDN_QKV % PROJ_TN == 0 and seq % PROJ_TM == 0
    assert w_in.shape == (D_IN, D_MODEL) and OFF_FOX % 8 == 0 and OFF_F % 8 == 0
    return pl.pallas_call(
        functools.partial(_proj_kernel, tiles_per_seq=seq // PROJ_TM),
        grid=(n // PROJ_TM,),
        in_specs=[
            pl.BlockSpec((PROJ_TM, D_MODEL), lambda i: (i, 0)),
            _resident((1, D_MODEL)),
            _resident((1, D_MODEL)),
            _resident((D_IN, D_MODEL)),
            _resident((CONV_W, GDN_QKV)),
        ],
        out_specs=[
            pl.BlockSpec((PROJ_TM, N_BIG), lambda i: (i, 0)),
            pl.BlockSpec((PROJ_TM, N_SMALL), lambda i: (i, 0)),
        ],
        out_shape=[
            jax.ShapeDtypeStruct((n, N_BIG), BF16),
            jax.ShapeDtypeStruct((n, N_SMALL), F32),
        ],
        scratch_shapes=[
            pltpu.VMEM((8, GDN_QKV), F32),
            pltpu.VMEM((N_BIG, D_MODEL), BF16),
            pltpu.VMEM((N_SMALL, D_MODEL), BF16),
        ],
        compiler_params=pltpu.CompilerParams(
            dimension_semantics=("arbitrary",), vmem_limit_bytes=VMEM_LIMIT),
        name="proj",
    )(x2, ln_g, ln_b, w_in, conv_w)


def _gates_kernel(s_ref, pv_ref, o_ref, gr_ref, *, seq):
    lane = lax.broadcasted_iota(jnp.int32, (1, N_SMALL), 1)
    bias = pv_ref[0:1, :]
    neg_a = -jnp.exp(pv_ref[1:2, :])
    r = lax.broadcasted_iota(jnp.int32, (GATE_TB, GATE_TB), 0)
    c = lax.broadcasted_iota(jnp.int32, (GATE_TB, GATE_TB), 1)
    l_full = jnp.where(r >= c, 1.0, 0.0).astype(F32)
    same_chunk = jnp.right_shift(r, 6) == jnp.right_shift(c, 6)
    l_chunk = jnp.where(same_chunk, l_full, 0.0)
    l_total = jnp.where(same_chunk, 1.0, 0.0).astype(F32)
    l_decay = jnp.concatenate([l_chunk, l_total], axis=0).astype(BF16)
    l_forget = l_full.astype(BF16)

    def pieces_sum(a):
        return a[:, 0:N_SMALL] + a[:, N_SMALL:2 * N_SMALL] + a[:, 2 * N_SMALL:]

    carry = jnp.zeros((1, N_SMALL), F32)
    for t in range(seq // GATE_TB):
        rows = slice(t * GATE_TB, (t + 1) * GATE_TB)
        x = s_ref[rows, :] + bias
        beta = _sigmoid(x)
        log_g = neg_a * _softplus(x)
        log_f = -_softplus(-x)
        decay = pieces_sum(_dot(l_decay, _split3(log_g)))
        gam, gtot = decay[:GATE_TB], decay[GATE_TB:]
        cum = pieces_sum(_dot(l_forget, _split3(log_f))) + carry
        carry = cum[GATE_TB - 1:GATE_TB, :]
        o_ref[rows, :] = jnp.where(
            lane < LANE_GAM, beta,
            jnp.where(lane < LANE_C, gam, jnp.where(lane < LANE_GTOT, cum, gtot)))
        gr_ref[0, t] = jnp.transpose(gam)[LANE_GAM:LANE_GAM + GDN_HEADS, :]


def _gates(small, pvec, batch, seq):
    assert GATE_TB == GDN_TB
    nt = seq // GATE_TB
    return pl.pallas_call(
        functools.partial(_gates_kernel, seq=seq),
        grid=(batch,),
        in_specs=[
            pl.BlockSpec((seq, N_SMALL), lambda b: (b, 0)),
            _resident((8, N_SMALL)),
        ],
        out_specs=[
            pl.BlockSpec((seq, N_SMALL), lambda b: (b, 0)),
            pl.BlockSpec((1, nt, GDN_HEADS, GATE_TB), lambda b: (b, 0, 0, 0)),
        ],
        out_shape=[
            jax.ShapeDtypeStruct((batch * seq, N_SMALL), F32),
            jax.ShapeDtypeStruct((batch, nt, GDN_HEADS, GATE_TB), F32),
        ],
        compiler_params=pltpu.CompilerParams(
            dimension_semantics=("arbitrary",), vmem_limit_bytes=VMEM_LIMIT),
        name="gates",
    )(small, pvec)


def _gdn_kernel(x_ref, z_ref, gc_ref, gr_ref, ng_ref, *rest, ncast):
    o_ref, s_ref = rest[ncast], rest[2 * ncast + 1]
    for w_f32, w_bf16 in zip(rest[:ncast], rest[ncast + 1:2 * ncast + 1]):
        w_bf16[...] = w_f32[...].astype(BF16)

    nb, tb = x_ref.shape[0], x_ref.shape[1]
    nch = tb // CHUNK
    heads = range(nb * GDN_HEADS)
    nh = GDN_HEADS

    @pl.when(pl.program_id(1) == 0)
    def _():
        s_ref[...] = jnp.zeros_like(s_ref)

    def act(h, c0):
        c0 = c0 + (h % nh) * GDN_DK
        return x_ref[h // nh, :, c0:c0 + GDN_DK].astype(F32)

    def l2norm(v):
        return v * lax.rsqrt(jnp.sum(v * v, -1, keepdims=True) + NORM_EPS)

    ri = lax.broadcasted_iota(jnp.int32, (tb, tb), 0)
    ci = lax.broadcasted_iota(jnp.int32, (tb, tb), 1)
    same = jnp.right_shift(ri, 6) == jnp.right_shift(ci, 6)
    keep = jnp.logical_and(same, ri >= ci)
    diag = ri == ci
    pr = lax.broadcasted_iota(jnp.int32, (CHUNK, tb), 0)
    pc = lax.broadcasted_iota(jnp.int32, (CHUNK, tb), 1)
    eye_packed = jnp.where(jnp.bitwise_and(pc, CHUNK - 1) == pr, 1.0, 0.0).astype(F32)
    lane_chunk = jnp.right_shift(lax.broadcasted_iota(jnp.int32, (1, tb), 1), 6)
    row_chunk = jnp.right_shift(lax.broadcasted_iota(jnp.int32, (tb, 1), 0), 6)

    def to_bd(packed):
        return jnp.where(same, jnp.concatenate([packed] * nch, axis=0), jnp.zeros((), packed.dtype))

    def to_packed(bd):
        out = bd[0:CHUNK]
        for c in range(1, nch):
            out = out + bd[c * CHUNK:(c + 1) * CHUNK]
        return out

    ng = ng_ref[...]
    q = [l2norm(act(h, 0)) * (GDN_DK ** -0.5) for h in heads]
    k = [l2norm(act(h, GDN_QK)) for h in heads]
    v = [act(h, 2 * GDN_QK) for h in heads]

    def gate(h, lane0):
        return gc_ref[h // nh, :, lane0 + h % nh:lane0 + h % nh + 1]

    beta = [gate(h, LANE_BETA) for h in heads]
    gam = [gate(h, LANE_GAM) for h in heads]
    gtot = [gate(h, LANE_GTOT) for h in heads]
    grow = [gr_ref[h // nh, 0, h % nh:h % nh + 1, :] for h in heads]
    egam = [jnp.exp(g) for g in gam]
    kb = [a.astype(BF16) for a in k]
    decay = [jnp.exp(jnp.where(keep, gam[h] - grow[h], -jnp.inf)) for h in heads]
    gram = [_dot_nt(kb[h], kb[h]) for h in heads]
    qk = [(_dot_nt(q[h].astype(BF16), kb[h]) * decay[h]).astype(BF16) for h in heads]

    x_bd = [jnp.where(diag, 0.0, -(gram[h] * beta[h]) * decay[h]) for h in heads]
    x_p = [to_packed(a) for a in x_bd]
    p_p = [eye_packed + a for a in x_p]
    x_p = [_dot(x_p[h].astype(BF16), x_bd[h].astype(BF16)) for h in heads]
    for _ in range(4):
        w_bd = [to_bd(a.astype(BF16)) for a in x_p]
        r = [_dot(jnp.concatenate([p_p[h], x_p[h]], axis=0).astype(BF16), w_bd[h]) for h in heads]
        p_p = [p_p[h] + r[h][:CHUNK] for h in heads]
        x_p = [r[h][CHUNK:] for h in heads]
    p_p = [p_p[h] + _dot(p_p[h].astype(BF16), to_bd(x_p[h].astype(BF16))) for h in heads]

    rhs = [jnp.concatenate([v[h] * beta[h], k[h] * (beta[h] * egam[h])], axis=1).astype(BF16)
           for h in heads]
    sol = [_dot(to_bd(p_p[h].astype(BF16)), rhs[h]) for h in heads]
    u = [a[:, :GDN_DV] for a in sol]
    w = [a[:, GDN_DV:].astype(BF16) for a in sol]
    qd = [(q[h] * egam[h]).astype(BF16) for h in heads]
    kd_t = [jnp.transpose(k[h] * jnp.exp(gtot[h] - gam[h])).astype(BF16) for h in heads]
    z = [z_ref[h // nh, :, (h % nh) * GDN_DV:(h % nh + 1) * GDN_DV].astype(F32) for h in heads]
    zgate = [a * _sigmoid(a) for a in z]

    state = [s_ref[h] for h in heads]
    for c in range(nch):
        rows = slice(c * CHUNK, (c + 1) * CHUNK)
        sb = [a.astype(BF16) for a in state]
        r1 = [_dot(jnp.concatenate([w[h][rows], qd[h][rows]], axis=0), sb[h]) for h in heads]
        v_new = [u[h][rows] - r1[h][:CHUNK] for h in heads]
        v_full = [jnp.where(row_chunk == c, jnp.concatenate([a.astype(BF16)] * nch, axis=0),
                            jnp.zeros((), BF16)) for a in v_new]
        kd_c = [jnp.where(lane_chunk == c, a, jnp.zeros((), BF16)) for a in kd_t]
        r2 = [_dot(jnp.concatenate([qk[h][rows], kd_c[h]], axis=0), v_full[h]) for h in heads]
        for h in heads:
            o = r1[h][CHUNK:] + r2[h][:CHUNK]
            on = o * lax.rsqrt(jnp.mean(o * o, -1, keepdims=True) + NORM_EPS) * ng
            o_ref[h // nh, rows, (h % nh) * GDN_DV:(h % nh + 1) * GDN_DV] = (
                on * zgate[h][rows]).astype(BF16)
        state = [state[h] * jnp.exp(gtot[h][c * CHUNK:c * CHUNK + 1]) + r2[h][CHUNK:] for h in heads]
    for h in heads:
        s_ref[h] = state[h]


def _gdn(proj_big, gcol, grow, norm_g, batch, seq, cast):
    nt = seq // GDN_TB
    nb = GDN_NB
    assert batch % nb == 0
    steps = (batch // nb) * nt
    assert all(w.shape[0] % (16 * steps) == 0 for w in cast)
    cast_specs = [pl.BlockSpec((w.shape[0] // steps, w.shape[1]), lambda g, t: (g * nt + t, 0))
                  for w in cast]
    return pl.pallas_call(
        functools.partial(_gdn_kernel, ncast=len(cast)),
        grid=(batch // nb, nt),
        in_specs=[
            pl.BlockSpec((nb, GDN_TB, GDN_QKV), lambda g, t: (g, t, 0)),
            pl.BlockSpec((nb, GDN_TB, GDN_WIDTH), lambda g, t: (g, t, OFF_Z // GDN_WIDTH)),
            pl.BlockSpec((nb, GDN_TB, N_SMALL), lambda g, t: (g, t, 0)),
            pl.BlockSpec((nb, 1, GDN_HEADS, GDN_TB), lambda g, t: (g, t, 0, 0)),
            _resident((1, GDN_DV)),
        ] + cast_specs,
        out_specs=[pl.BlockSpec((nb, GDN_TB, GDN_WIDTH), lambda g, t: (g, t, 0))] + cast_specs,
        out_shape=[jax.ShapeDtypeStruct((batch, seq, GDN_WIDTH), BF16)]
        + [jax.ShapeDtypeStruct(w.shape, BF16) for w in cast],
        scratch_shapes=[
            pltpu.VMEM((nb * GDN_HEADS, GDN_DK, GDN_DV), F32),
        ],
        compiler_params=pltpu.CompilerParams(
            dimension_semantics=("arbitrary", "arbitrary"), vmem_limit_bytes=VMEM_LIMIT),
        name="gdn",
    )(proj_big, proj_big, gcol, grow, norm_g, *cast)


FOX_X = 128
FOX_QC = (0, 6)
FOX_KC = (3, 9)


def _split3(c):
    hi = c.astype(BF16)
    r1 = c - hi.astype(F32)
    mid = r1.astype(BF16)
    lo = (r1 - mid.astype(F32)).astype(BF16)
    return jnp.concatenate([hi, mid, lo], axis=1)


def _select3(pair, base):
    e = np.zeros((3 * N_SMALL, FOX_X), np.float32)
    for d in range(3):
        for head in range(2):
            e[d * N_SMALL + LANE_C + 2 * pair + head, base[head] + d] = 1.0
    return e


def _lane_ones(first):
    out = np.zeros((FOX_X,), np.float32)
    for f in first:
        out[f:f + 3] = 1.0
    return out


def _fox_constants():
    npair = FOX_HEADS // 2
    sel = np.stack([np.stack([_select3(p, base) for p in range(npair)])
                    for base in (FOX_KC, FOX_QC)])
    ones = np.stack([_lane_ones(FOX_QC), _lane_ones(FOX_KC)])
    return jnp.asarray(sel, BF16), jnp.asarray(ones, F32)


def _fox_kernel(q_ref, k_ref, v_ref, gc_ref, ng_ref, sel_ref, ones_ref, o_ref, qa_ref, ka_ref, vt_ref,
                m_ref, l_ref, acc_ref, s_ref):
    nb, tq = q_ref.shape[0], q_ref.shape[1]
    seq = k_ref.shape[1]
    i = pl.program_id(1)
    npair = FOX_HEADS // 2
    pairs = range(nb * npair)
    lane = lax.broadcasted_iota(jnp.int32, (1, 2 * FOX_DH), 1)
    lo_half = lane < FOX_DH
    pcols = [slice((p % npair) * 2 * FOX_DH, (p % npair + 1) * 2 * FOX_DH) for p in pairs]

    @pl.when(i == 0)
    def _():
        ones_k = ones_ref[0:1, :]

        def fill(t, carry):
            rows = pl.ds(pl.multiple_of(t * tq, tq), tq)
            pieces = [_split3(gc_ref[r, rows, :] * LOG2E) for r in range(nb)]
            for p in pairs:
                r = p // npair
                ka_ref[p, rows, 0:2 * FOX_DH] = k_ref[r, rows, pcols[p]]
                ka_ref[p, rows, 2 * FOX_DH:] = (
                    ones_k - _dot(pieces[r], sel_ref[0, p % npair])).astype(BF16)
                vt_ref[p, t] = jnp.transpose(v_ref[r, rows, pcols[p]].astype(F32)).astype(BF16)
            return carry

        lax.fori_loop(0, seq // tq, fill, 0)

    qrows = pl.ds(pl.multiple_of(i * tq, tq), tq)
    pieces_q = [_split3(gc_ref[r, qrows, :] * LOG2E) for r in range(nb)]
    ones_q = ones_ref[1:2, :]
    xlane = lax.broadcasted_iota(jnp.int32, (1, FOX_X), 1)
    for p in pairs:
        qp = q_ref[p // npair, :, pcols[p]]
        qx = _dot(pieces_q[p // npair], sel_ref[1, p % npair]) + ones_q
        for half in range(2):
            rows = slice(half * tq, (half + 1) * tq)
            mine = lo_half if half == 0 else jnp.logical_not(lo_half)
            xmine = (xlane < FOX_QC[1]) if half == 0 else (xlane >= FOX_QC[1])
            qa_ref[p, rows, 0:2 * FOX_DH] = jnp.where(mine, qp, jnp.zeros((), BF16))
            qa_ref[p, rows, 2 * FOX_DH:] = jnp.where(xmine, qx, 0.0).astype(BF16)

    ki = lax.broadcasted_iota(jnp.int32, (tq, 2 * tq), 0)
    qi = jnp.bitwise_and(lax.broadcasted_iota(jnp.int32, (tq, 2 * tq), 1), tq - 1)
    causal = ki <= qi

    def scores(j):
        krows = pl.ds(pl.multiple_of(j * tq, tq), tq)
        return [_dot_nt(ka_ref[p, krows, :], qa_ref[p]) for p in pairs]

    def absorb(s, j, first):
        for p in pairs:
            sp = s[p]
            if first:
                sp = jnp.where(causal, sp, -jnp.inf)
            m_new = jnp.max(sp, 0, keepdims=True)
            if not first:
                m_prev = m_ref[p]
                m_new = jnp.maximum(m_prev, m_new)
                scale = jnp.exp2(m_prev - m_new)
            prob = jnp.exp2(sp - m_new)
            psum = jnp.sum(prob, 0, keepdims=True)
            pv = _dot(vt_ref[p, j], prob.astype(BF16))
            m_ref[p] = m_new
            if first:
                l_ref[p] = psum
                acc_ref[p] = pv
            else:
                l_ref[p] = scale * l_ref[p] + psum
                acc_ref[p] = scale * acc_ref[p] + pv

    last = jnp.maximum(i - 1, 0)

    def stage(slot, j_next):
        s_next = scores(jnp.minimum(j_next, last))
        for p in pairs:
            s_ref[slot, p] = s_next[p]

    def consume(slot, j):
        absorb([s_ref[slot, p] for p in pairs], j, False)

    s_diag = scores(i)
    stage(0, 0)
    absorb(s_diag, i, True)

    def body(t, carry):
        stage(1, 2 * t + 1)
        consume(0, 2 * t)
        stage(0, 2 * t + 2)
        consume(1, 2 * t + 1)
        return carry

    lax.fori_loop(0, i // 2, body, 0)

    @pl.when(i % 2 == 1)
    def _():
        consume(0, i - 1)

    ng = ng_ref[...]
    for p in pairs:
        acc = acc_ref[p]
        l = l_ref[p]
        outs = []
        for half in range(2):
            oh = (acc[half * FOX_DH:(half + 1) * FOX_DH, half * tq:(half + 1) * tq]
                  / l[:, half * tq:(half + 1) * tq])
            ms = jnp.mean(oh * oh, 0, keepdims=True)
            outs.append(oh * lax.rsqrt(ms + NORM_EPS))
        o_ref[p // npair, :, pcols[p]] = (
            jnp.transpose(jnp.concatenate(outs, axis=0)) * ng).astype(BF16)


def _fox(proj_big, gcol, norm_g2, batch, seq):
    nq = seq // FOX_TQ
    first = (GDN_QKV + GDN_WIDTH) // FOX_WIDTH
    npair = FOX_HEADS // 2
    nb = FOX_NB
    assert batch % nb == 0
    nu = nb * npair
    return pl.pallas_call(
        _fox_kernel,
        grid=(batch // nb, nq),
        in_specs=[
            pl.BlockSpec((nb, FOX_TQ, FOX_WIDTH), lambda g, i: (g, i, first)),
            pl.BlockSpec((nb, seq, FOX_WIDTH), lambda g, i: (g, 0, first + 1)),
            pl.BlockSpec((nb, seq, FOX_WIDTH), lambda g, i: (g, 0, first + 2)),
            pl.BlockSpec((nb, seq, N_SMALL), lambda g, i: (g, 0, 0)),
            _resident((1, 2 * FOX_DH)),
            _resident((2, npair, 3 * N_SMALL, FOX_X)),
            _resident((2, FOX_X)),
        ],
        out_specs=pl.BlockSpec((nb, FOX_TQ, FOX_WIDTH), lambda g, i: (g, i, 0)),
        out_shape=jax.ShapeDtypeStruct((batch, seq, FOX_WIDTH), BF16),
        scratch_shapes=[
            pltpu.VMEM((nu, 2 * FOX_TQ, 2 * FOX_DH + FOX_X), BF16),
            pltpu.VMEM((nu, seq, 2 * FOX_DH + FOX_X), BF16),
            pltpu.VMEM((nu, nq, 2 * FOX_DH, FOX_TQ), BF16),
            pltpu.VMEM((nu, 1, 2 * FOX_TQ), F32),
            pltpu.VMEM((nu, 1, 2 * FOX_TQ), F32),
            pltpu.VMEM((nu, 2 * FOX_DH, 2 * FOX_TQ), F32),
            pltpu.VMEM((2, nu, FOX_TQ, 2 * FOX_TQ), F32),
        ],
        compiler_params=pltpu.CompilerParams(
            dimension_semantics=("arbitrary", "arbitrary"), vmem_limit_bytes=VMEM_LIMIT),
        name="fox",
    )(proj_big, proj_big, proj_big, gcol, norm_g2, *_fox_constants())


def _tail_kernel(x_ref, p_ref, og_ref, of_ref, lin_g, lin_b, wo_ref, l1g, l1b, wu_ref, wd_ref,
                 wp_ref, wg_ref, bg_ref, l2g, l2b, o_ref):
    nparts = x_ref.shape[0] // TAIL_PART
    parts = [slice(r * TAIL_PART, (r + 1) * TAIL_PART) for r in range(nparts)]

    def head(rows):
        h = _layer_norm(x_ref[rows, :], lin_g[...], lin_b[...])
        mix = (_dot(og_ref[rows, :], wo_ref[0:GDN_WIDTH, :])
               + _dot(of_ref[rows, :], wo_ref[GDN_WIDTH:, :]))
        return _layer_norm(ALPHA * h + mix, l1g[...], l1b[...])

    def mlp(rows, h1):
        h1b = h1.astype(BF16)
        gate = _sigmoid(_dot(h1b, wg_ref[...]) + bg_ref[...])
        acc = ALPHA * h1 + _dot(p_ref[rows, :].astype(BF16), wp_ref[...]) * gate
        for j in range(D_FF // TAIL_TF):
            cols = slice(j * TAIL_TF, (j + 1) * TAIL_TF)
            a = jnp.maximum(_dot(h1b, wu_ref[:, cols]), 0.0)
            acc = acc + _dot((a * a).astype(BF16), wd_ref[cols, :])
        return acc

    h1 = head(parts[0])
    for r in range(nparts):
        h1_next = head(parts[r + 1]) if r + 1 < nparts else None
        o_ref[parts[r], :] = _layer_norm(mlp(parts[r], h1), l2g[...], l2b[...])
        h1 = h1_next


def _tail(x2, p2, o_gdn, o_fox, lin_g, lin_b, w_out, l1g, l1b, w_up, w_down, w_ple, w_gate,
          b_gate, l2g, l2b):
    n = x2.shape[0]
    row = lambda width: pl.BlockSpec((TAIL_TM, width), lambda i: (i, 0))
    vec = _resident((1, D_MODEL))
    return pl.pallas_call(
        _tail_kernel,
        grid=(n // TAIL_TM,),
        in_specs=[
            row(D_MODEL), row(D_PLE), row(GDN_WIDTH), row(FOX_WIDTH),
            vec, vec, _resident((D_MODEL, D_MODEL)), vec, vec,
            _resident((D_MODEL, D_FF)), _resident((D_FF, D_MODEL)),
            _resident((D_PLE, D_MODEL)), _resident((D_MODEL, D_MODEL)), vec, vec, vec,
        ],
        out_specs=row(D_MODEL),
        out_shape=jax.ShapeDtypeStruct((n, D_MODEL), F32),
        compiler_params=pltpu.CompilerParams(
            dimension_semantics=("arbitrary",), vmem_limit_bytes=VMEM_LIMIT),
        name="tail",
    )(x2, p2, o_gdn, o_fox, lin_g, lin_b, w_out, l1g, l1b, w_up, w_down, w_ple, w_gate,
      b_gate, l2g, l2b)


def kernel(x, p, ln_in_g, ln_in_b, w_in, conv_w, a_log, dt_bias, gdn_norm_g, b_f, fox_norm_g,
           w_out, ln1_g, ln1_b, w_up, w_down, w_ple, w_ple_gate, b_ple_gate, ln2_g, ln2_b):
    batch, seq, _ = x.shape
    assert x.shape[2] == D_MODEL and w_in.shape[0] == 1
    assert seq % FOX_TQ == 0 and seq % GDN_TB == 0 and (batch * seq) % PROJ_TM == 0
    n = batch * seq
    x2 = x.reshape(n, D_MODEL)
    p2 = p[0].reshape(n, D_PLE)
    row = lambda a: a.reshape(1, -1).astype(F32)

    n_gate = LANE_GTOT + GDN_HEADS
    zeros4 = jnp.zeros((GDN_HEADS,), F32)
    pad = jnp.zeros((N_SMALL - n_gate,), F32)
    pvec = jnp.zeros((8, N_SMALL), F32)
    pvec = pvec.at[0].set(jnp.concatenate([zeros4, dt_bias[0], b_f[0], zeros4, dt_bias[0], pad]))
    pvec = pvec.at[1].set(
        jnp.concatenate([zeros4, a_log[0], jnp.zeros((FOX_HEADS,), F32), zeros4, a_log[0], pad]))

    proj_big, small = _proj(x2, row(ln_in_g), row(ln_in_b), jnp.swapaxes(w_in[0], 0, 1), conv_w[0],
                            seq)
    gcol, grow = _gates(small, pvec, batch, seq)

    proj3 = proj_big.reshape(batch, seq, N_BIG)
    gcol3 = gcol.reshape(batch, seq, N_SMALL)
    o_gdn, w_out_b, w_up_b, w_down_b, w_ple_b, w_gate_b = _gdn(
        proj3, gcol3, grow, row(gdn_norm_g[0]), batch, seq,
        (w_out[0], w_up[0], w_down[0], w_ple[0], w_ple_gate[0]))
    o_gdn = o_gdn.reshape(n, GDN_WIDTH)
    o_fox = _fox(proj3, gcol3, row(jnp.tile(fox_norm_g[0], 2)), batch, seq).reshape(n, FOX_WIDTH)

    out = _tail(x2, p2, o_gdn, o_fox, row(ln_in_g), row(ln_in_b), w_out_b,
                row(ln1_g[0]), row(ln1_b[0]), w_up_b, w_down_b, w_ple_b, w_gate_b,
                row(b_ple_gate[0]), row(ln2_g[0]), row(ln2_b[0]))
    return out.reshape(batch, seq, D_MODEL)
```

```python
import functools

import jax
import jax.numpy as jnp
import numpy as np
from jax import lax
from jax.experimental import pallas as pl
from jax.experimental.pallas import tpu as pltpu

F32 = jnp.float32
BF16 = jnp.bfloat16

D_MODEL = 1024
CHUNK = 64
GDN_HEADS = 4
GDN_DK = 128
GDN_DV = 128
GDN_QK = GDN_HEADS * GDN_DK
GDN_QKV = 3 * GDN_QK
GDN_WIDTH = GDN_HEADS * GDN_DV
FOX_HEADS = 8
FOX_DH = 64
FOX_WIDTH = FOX_HEADS * FOX_DH
CONV_W = 4
D_FF = 4 * D_MODEL
D_PLE = 256
LN_EPS = 1e-5
NORM_EPS = 1e-6
ALPHA = 2.0 ** 0.25

OFF_Z = GDN_QKV
OFF_BETA = OFF_Z + GDN_WIDTH
OFF_FOX = OFF_BETA + 2 * GDN_HEADS
OFF_F = OFF_FOX + 3 * FOX_WIDTH
D_IN = OFF_F + FOX_HEADS
N_BIG = GDN_QKV + GDN_WIDTH + 3 * FOX_WIDTH
N_SMALL = 128
FOX_Q0 = GDN_QKV + GDN_WIDTH
LOG2E = 1.4426950408889634
FOX_QSCALE = FOX_DH ** -0.5 * LOG2E
LANE_BETA = 0
LANE_GAM = GDN_HEADS
LANE_C = 2 * GDN_HEADS
LANE_GTOT = LANE_C + FOX_HEADS

VMEM_LIMIT = 56 * 1024 * 1024

PROJ_TM = 1024
PROJ_TN = 256
PROJ_AHEAD = 1
GATE_TB = 256
GDN_TB = 256
GDN_NB = 4
FOX_TQ = 256
FOX_NB = 2
TAIL_TM = 1024
TAIL_PART = 256
TAIL_TF = 1024


def _layer_norm(x, g, b):
    mu = jnp.mean(x, -1, keepdims=True)
    xc = x - mu
    var = jnp.mean(xc * xc, -1, keepdims=True)
    return xc * lax.rsqrt(var + LN_EPS) * g + b


def _softplus(x):
    return jnp.maximum(x, 0.0) + jnp.log(1.0 + jnp.exp(-jnp.abs(x)))


def _sigmoid(x):
    return 1.0 / (1.0 + jnp.exp(-x))


def _dot(a, b):
    return jnp.dot(a, b, preferred_element_type=F32)


def _dot_nt(a, b):
    return lax.dot_general(a, b, (((1,), (1,)), ((), ())), preferred_element_type=F32)


def _resident(shape):
    return pl.BlockSpec(shape, lambda *_: (0,) * len(shape), pipeline_mode=pl.Buffered(1))


def _proj_kernel(x_ref, g_ref, b_ref, w_ref, cw_ref, ob_ref, os_ref, halo_ref, wb_ref, ws_ref, *,
                 tiles_per_seq):
    tm = x_ref.shape[0]
    halo = halo_ref.shape[0]

    @pl.when(pl.program_id(0) == 0)
    def _():
        r = lax.broadcasted_iota(jnp.int32, (N_SMALL, N_SMALL), 0)
        c = lax.broadcasted_iota(jnp.int32, (N_SMALL, N_SMALL), 1)
        same = jnp.where(r == c, 1.0, 0.0)
        again = jnp.where(r + (LANE_GTOT - LANE_GAM) == c, 1.0, 0.0)
        pick_decay = jnp.where(c < LANE_C, same,
                               jnp.where(c < LANE_GTOT, 0.0,
                                         jnp.where(c < LANE_GTOT + GDN_HEADS, again, 0.0))).astype(BF16)
        tail0 = N_SMALL - FOX_HEADS - LANE_C
        pick_forget = jnp.where(c < LANE_C, 0.0,
                                jnp.where(c < LANE_GTOT, jnp.where(r == c + tail0, 1.0, 0.0),
                                          0.0)).astype(BF16)
        step = 128
        for r0 in range(0, D_MODEL, step):
            rows = slice(r0, r0 + step)
            wb_ref[rows, 0:OFF_BETA] = w_ref[rows, 0:OFF_BETA].astype(BF16)
            wb_ref[rows, OFF_BETA:N_BIG] = w_ref[rows, OFF_FOX:OFF_F].astype(BF16)
            g1 = w_ref[rows, OFF_BETA:OFF_BETA + N_SMALL].astype(BF16)
            g2 = w_ref[rows, D_IN - N_SMALL:D_IN].astype(BF16)
            ws_ref[rows, :] = (_dot(g1, pick_decay) + _dot(g2, pick_forget)).astype(BF16)

    @pl.when(pl.program_id(0) % tiles_per_seq == 0)
    def _():
        halo_ref[...] = jnp.zeros_like(halo_ref)

    h = _layer_norm(x_ref[...], g_ref[...], b_ref[...]).astype(BF16)
    nchunk = N_BIG // PROJ_TN
    nconv = GDN_QKV // PROJ_TN
    order = [c for pair in zip(range(nconv), range(nconv, 2 * nconv)) for c in pair]
    order += list(range(2 * nconv, nchunk))
    def chunk_dot(c):
        return _dot(h, wb_ref[:, c * PROJ_TN:(c + 1) * PROJ_TN])

    inflight = [chunk_dot(c) for c in order[:PROJ_AHEAD]]
    for pos, j in enumerate(order):
        cols = slice(j * PROJ_TN, (j + 1) * PROJ_TN)
        acc = inflight.pop(0)
        if pos + PROJ_AHEAD < nchunk:
            inflight.append(chunk_dot(order[pos + PROJ_AHEAD]))
        if (j + 1) * PROJ_TN <= GDN_QKV:
            ext = jnp.concatenate([halo_ref[:, cols], acc], axis=0)
            y = acc * cw_ref[CONV_W - 1:CONV_W, cols]
            for d in range(1, CONV_W):
                y = y + pltpu.roll(ext, d, 0)[halo:] * cw_ref[CONV_W - 1 - d:CONV_W - d, cols]
            halo_ref[:, cols] = acc[tm - halo:]
            acc = y * _sigmoid(y)
        elif FOX_Q0 <= j * PROJ_TN < FOX_Q0 + FOX_WIDTH:
            acc = acc * FOX_QSCALE
        ob_ref[:, cols] = acc.astype(BF16)
    os_ref[...] = _dot(h, ws_ref[...])


def _proj(x2, ln_g, ln_b, w_in, conv_w, seq):
    n = x2.shape[0]
    assert GDN_QKV % PROJ_TN == 0 and seq % PROJ_TM == 0
    assert w_in.shape == (D_MODEL, D_IN) and OFF_BETA % N_SMALL == 0 and (OFF_F - 8) % N_SMALL == 0
    return pl.pallas_call(
        functools.partial(_proj_kernel, tiles_per_seq=seq // PROJ_TM),
        grid=(n // PROJ_TM,),
        in_specs=[
            pl.BlockSpec((PROJ_TM, D_MODEL), lambda i: (i, 0)),
            _resident((1, D_MODEL)),
            _resident((1, D_MODEL)),
            _resident((D_MODEL, D_IN)),
            _resident((CONV_W, GDN_QKV)),
        ],
        out_specs=[
            pl.BlockSpec((PROJ_TM, N_BIG), lambda i: (i, 0)),
            pl.BlockSpec((PROJ_TM, N_SMALL), lambda i: (i, 0)),
        ],
        out_shape=[
            jax.ShapeDtypeStruct((n, N_BIG), BF16),
            jax.ShapeDtypeStruct((n, N_SMALL), F32),
        ],
        scratch_shapes=[
            pltpu.VMEM((8, GDN_QKV), F32),
            pltpu.VMEM((D_MODEL, N_BIG), BF16),
            pltpu.VMEM((D_MODEL, N_SMALL), BF16),
        ],
        compiler_params=pltpu.CompilerParams(
            dimension_semantics=("arbitrary",), vmem_limit_bytes=VMEM_LIMIT),
        name="proj",
    )(x2, ln_g, ln_b, w_in, conv_w)


def _gates_kernel(s_ref, pv_ref, o_ref, gr_ref, *, seq):
    lane = lax.broadcasted_iota(jnp.int32, (1, N_SMALL), 1)
    bias = pv_ref[0:1, :]
    neg_a = -jnp.exp(pv_ref[1:2, :])
    r = lax.broadcasted_iota(jnp.int32, (GATE_TB, GATE_TB), 0)
    c = lax.broadcasted_iota(jnp.int32, (GATE_TB, GATE_TB), 1)
    l_full = jnp.where(r >= c, 1.0, 0.0).astype(F32)
    same_chunk = jnp.right_shift(r, 6) == jnp.right_shift(c, 6)
    l_chunk = jnp.where(same_chunk, l_full, 0.0)
    l_total = jnp.where(same_chunk, 1.0, 0.0).astype(F32)
    l_decay = jnp.concatenate([l_chunk, l_total], axis=0).astype(BF16)
    l_forget = l_full.astype(BF16)

    def pieces_sum(a):
        return a[:, 0:N_SMALL] + a[:, N_SMALL:2 * N_SMALL] + a[:, 2 * N_SMALL:]

    carry = jnp.zeros((1, N_SMALL), F32)
    for t in range(seq // GATE_TB):
        rows = slice(t * GATE_TB, (t + 1) * GATE_TB)
        x = s_ref[rows, :] + bias
        beta = _sigmoid(x)
        log_g = neg_a * _softplus(x)
        log_f = -_softplus(-x)
        decay = pieces_sum(_dot(l_decay, _split3(log_g)))
        gam, gtot = decay[:GATE_TB], decay[GATE_TB:]
        cum = pieces_sum(_dot(l_forget, _split3(log_f))) + carry
        carry = cum[GATE_TB - 1:GATE_TB, :]
        o_ref[rows, :] = jnp.where(
            lane < LANE_GAM, beta,
            jnp.where(lane < LANE_C, gam, jnp.where(lane < LANE_GTOT, cum, gtot)))
        gr_ref[0, t] = jnp.transpose(gam)[LANE_GAM:LANE_GAM + GDN_HEADS, :]


def _gates(small, pvec, batch, seq):
    assert GATE_TB == GDN_TB
    nt = seq // GATE_TB
    return pl.pallas_call(
        functools.partial(_gates_kernel, seq=seq),
        grid=(batch,),
        in_specs=[
            pl.BlockSpec((seq, N_SMALL), lambda b: (b, 0)),
            _resident((8, N_SMALL)),
        ],
        out_specs=[
            pl.BlockSpec((seq, N_SMALL), lambda b: (b, 0)),
            pl.BlockSpec((1, nt, GDN_HEADS, GATE_TB), lambda b: (b, 0, 0, 0)),
        ],
        out_shape=[
            jax.ShapeDtypeStruct((batch * seq, N_SMALL), F32),
            jax.ShapeDtypeStruct((batch, nt, GDN_HEADS, GATE_TB), F32),
        ],
        compiler_params=pltpu.CompilerParams(
            dimension_semantics=("arbitrary",), vmem_limit_bytes=VMEM_LIMIT),
        name="gates",
    )(small, pvec)


def _gdn_kernel(x_ref, z_ref, gc_ref, gr_ref, ng_ref, *rest, ncast):
    o_ref, s_ref = rest[ncast], rest[2 * ncast + 1]
    for w_f32, w_bf16 in zip(rest[:ncast], rest[ncast + 1:2 * ncast + 1]):
        w_bf16[...] = w_f32[...].astype(BF16)

    nb, tb = x_ref.shape[0], x_ref.shape[1]
    nch = tb // CHUNK
    heads = range(nb * GDN_HEADS)
    nh = GDN_HEADS

    @pl.when(pl.program_id(1) == 0)
    def _():
        s_ref[...] = jnp.zeros_like(s_ref)

    def act(h, c0):
        c0 = c0 + (h % nh) * GDN_DK
        return x_ref[h // nh, :, c0:c0 + GDN_DK].astype(F32)

    def l2norm(v):
        return v * lax.rsqrt(jnp.sum(v * v, -1, keepdims=True) + NORM_EPS)

    ri = lax.broadcasted_iota(jnp.int32, (tb, tb), 0)
    ci = lax.broadcasted_iota(jnp.int32, (tb, tb), 1)
    same = jnp.right_shift(ri, 6) == jnp.right_shift(ci, 6)
    keep = jnp.logical_and(same, ri >= ci)
    diag = ri == ci
    pr = lax.broadcasted_iota(jnp.int32, (CHUNK, tb), 0)
    pc = lax.broadcasted_iota(jnp.int32, (CHUNK, tb), 1)
    eye_packed = jnp.where(jnp.bitwise_and(pc, CHUNK - 1) == pr, 1.0, 0.0).astype(F32)
    lane_chunk = jnp.right_shift(lax.broadcasted_iota(jnp.int32, (1, tb), 1), 6)
    row_chunk = jnp.right_shift(lax.broadcasted_iota(jnp.int32, (tb, 1), 0), 6)

    def to_bd(packed):
        return jnp.where(same, jnp.concatenate([packed] * nch, axis=0), jnp.zeros((), packed.dtype))

    def to_packed(bd):
        out = bd[0:CHUNK]
        for c in range(1, nch):
            out = out + bd[c * CHUNK:(c + 1) * CHUNK]
        return out

    ng = ng_ref[...]
    q = [l2norm(act(h, 0)) * (GDN_DK ** -0.5) for h in heads]
    k = [l2norm(act(h, GDN_QK)) for h in heads]
    v = [act(h, 2 * GDN_QK) for h in heads]

    def gate(h, lane0):
        return gc_ref[h // nh, :, lane0 + h % nh:lane0 + h % nh + 1]

    beta = [gate(h, LANE_BETA) for h in heads]
    gam = [gate(h, LANE_GAM) for h in heads]
    gtot = [gate(h, LANE_GTOT) for h in heads]
    grow = [gr_ref[h // nh, 0, h % nh:h % nh + 1, :] for h in heads]
    egam = [jnp.exp(g) for g in gam]
    kb = [a.astype(BF16) for a in k]
    decay = [jnp.exp(jnp.where(keep, gam[h] - grow[h], -jnp.inf)) for h in heads]
    gram = [_dot_nt(kb[h], kb[h]) for h in heads]
    qk = [(_dot_nt(q[h].astype(BF16), kb[h]) * decay[h]).astype(BF16) for h in heads]

    x_bd = [jnp.where(diag, 0.0, -(gram[h] * beta[h]) * decay[h]) for h in heads]
    x_p = [to_packed(a) for a in x_bd]
    p_p = [eye_packed + a for a in x_p]
    x_p = [_dot(x_p[h].astype(BF16), x_bd[h].astype(BF16)) for h in heads]
    for _ in range(4):
        w_bd = [to_bd(a.astype(BF16)) for a in x_p]
        r = [_dot(jnp.concatenate([p_p[h], x_p[h]], axis=0).astype(BF16), w_bd[h]) for h in heads]
        p_p = [p_p[h] + r[h][:CHUNK] for h in heads]
        x_p = [r[h][CHUNK:] for h in heads]
    p_p = [p_p[h] + _dot(p_p[h].astype(BF16), to_bd(x_p[h].astype(BF16))) for h in heads]

    rhs = [jnp.concatenate([v[h] * beta[h], k[h] * (beta[h] * egam[h])], axis=1).astype(BF16)
           for h in heads]
    sol = [_dot(to_bd(p_p[h].astype(BF16)), rhs[h]) for h in heads]
    u = [a[:, :GDN_DV] for a in sol]
    w = [a[:, GDN_DV:].astype(BF16) for a in sol]
    qd = [(q[h] * egam[h]).astype(BF16) for h in heads]
    kd_t = [jnp.transpose(k[h] * jnp.exp(gtot[h] - gam[h])).astype(BF16) for h in heads]
    z = [z_ref[h // nh, :, (h % nh) * GDN_DV:(h % nh + 1) * GDN_DV].astype(F32) for h in heads]
    zgate = [a * _sigmoid(a) for a in z]

    state = [s_ref[h] for h in heads]
    for c in range(nch):
        rows = slice(c * CHUNK, (c + 1) * CHUNK)
        sb = [a.astype(BF16) for a in state]
        r1 = [_dot(jnp.concatenate([w[h][rows], qd[h][rows]], axis=0), sb[h]) for h in heads]
        v_new = [u[h][rows] - r1[h][:CHUNK] for h in heads]
        v_full = [jnp.where(row_chunk == c, jnp.concatenate([a.astype(BF16)] * nch, axis=0),
                            jnp.zeros((), BF16)) for a in v_new]
        kd_c = [jnp.where(lane_chunk == c, a, jnp.zeros((), BF16)) for a in kd_t]
        r2 = [_dot(jnp.concatenate([qk[h][rows], kd_c[h]], axis=0), v_full[h]) for h in heads]
        for h in heads:
            o = r1[h][CHUNK:] + r2[h][:CHUNK]
            on = o * lax.rsqrt(jnp.mean(o * o, -1, keepdims=True) + NORM_EPS) * ng
            o_ref[h // nh, rows, (h % nh) * GDN_DV:(h % nh + 1) * GDN_DV] = (
                on * zgate[h][rows]).astype(BF16)
        state = [state[h] * jnp.exp(gtot[h][c * CHUNK:c * CHUNK + 1]) + r2[h][CHUNK:] for h in heads]
    for h in heads:
        s_ref[h] = state[h]


def _gdn(proj_big, gcol, grow, norm_g, batch, seq, cast):
    nt = seq // GDN_TB
    nb = GDN_NB
    assert batch % nb == 0
    steps = (batch // nb) * nt
    assert all(w.shape[0] % (16 * steps) == 0 for w in cast)
    cast_specs = [pl.BlockSpec((w.shape[0] // steps, w.shape[1]), lambda g, t: (g * nt + t, 0))
                  for w in cast]
    return pl.pallas_call(
        functools.partial(_gdn_kernel, ncast=len(cast)),
        grid=(batch // nb, nt),
        in_specs=[
            pl.BlockSpec((nb, GDN_TB, GDN_QKV), lambda g, t: (g, t, 0)),
            pl.BlockSpec((nb, GDN_TB, GDN_WIDTH), lambda g, t: (g, t, OFF_Z // GDN_WIDTH)),
            pl.BlockSpec((nb, GDN_TB, N_SMALL), lambda g, t: (g, t, 0)),
            pl.BlockSpec((nb, 1, GDN_HEADS, GDN_TB), lambda g, t: (g, t, 0, 0)),
            _resident((1, GDN_DV)),
        ] + cast_specs,
        out_specs=[pl.BlockSpec((nb, GDN_TB, GDN_WIDTH), lambda g, t: (g, t, 0))] + cast_specs,
        out_shape=[jax.ShapeDtypeStruct((batch, seq, GDN_WIDTH), BF16)]
        + [jax.ShapeDtypeStruct(w.shape, BF16) for w in cast],
        scratch_shapes=[
            pltpu.VMEM((nb * GDN_HEADS, GDN_DK, GDN_DV), F32),
        ],
        compiler_params=pltpu.CompilerParams(
            dimension_semantics=("arbitrary", "arbitrary"), vmem_limit_bytes=VMEM_LIMIT),
        name="gdn",
    )(proj_big, proj_big, gcol, grow, norm_g, *cast)


FOX_X = 128
FOX_QC = (0, 6)
FOX_KC = (3, 9)


def _split3(c):
    hi = c.astype(BF16)
    r1 = c - hi.astype(F32)
    mid = r1.astype(BF16)
    lo = (r1 - mid.astype(F32)).astype(BF16)
    return jnp.concatenate([hi, mid, lo], axis=1)


def _select3(pair, base):
    e = np.zeros((3 * N_SMALL, FOX_X), np.float32)
    for d in range(3):
        for head in range(2):
            e[d * N_SMALL + LANE_C + 2 * pair + head, base[head] + d] = 1.0
    return e


def _lane_ones(first):
    out = np.zeros((FOX_X,), np.float32)
    for f in first:
        out[f:f + 3] = 1.0
    return out


def _fox_constants():
    npair = FOX_HEADS // 2
    sel = np.stack([np.stack([_select3(p, base) for p in range(npair)])
                    for base in (FOX_KC, FOX_QC)])
    ones = np.stack([_lane_ones(FOX_QC), _lane_ones(FOX_KC)])
    return jnp.asarray(sel, BF16), jnp.asarray(ones, F32)


def _fox_kernel(q_ref, k_ref, v_ref, gc_ref, ng_ref, sel_ref, ones_ref, o_ref, qa_ref, ka_ref, vt_ref,
                m_ref, l_ref, acc_ref, s_ref):
    nb, tq = q_ref.shape[0], q_ref.shape[1]
    seq = k_ref.shape[1]
    i = pl.program_id(1)
    npair = FOX_HEADS // 2
    pairs = range(nb * npair)
    lane = lax.broadcasted_iota(jnp.int32, (1, 2 * FOX_DH), 1)
    lo_half = lane < FOX_DH
    pcols = [slice((p % npair) * 2 * FOX_DH, (p % npair + 1) * 2 * FOX_DH) for p in pairs]

    @pl.when(i == 0)
    def _():
        ones_k = ones_ref[0:1, :]

        def fill(t, carry):
            rows = pl.ds(pl.multiple_of(t * tq, tq), tq)
            pieces = [_split3(gc_ref[r, rows, :] * LOG2E) for r in range(nb)]
            for p in pairs:
                r = p // npair
                ka_ref[p, rows, 0:2 * FOX_DH] = k_ref[r, rows, pcols[p]]
                ka_ref[p, rows, 2 * FOX_DH:] = (
                    ones_k - _dot(pieces[r], sel_ref[0, p % npair])).astype(BF16)
                vt_ref[p, t] = jnp.transpose(v_ref[r, rows, pcols[p]].astype(F32)).astype(BF16)
            return carry

        lax.fori_loop(0, seq // tq, fill, 0)

    qrows = pl.ds(pl.multiple_of(i * tq, tq), tq)
    pieces_q = [_split3(gc_ref[r, qrows, :] * LOG2E) for r in range(nb)]
    ones_q = ones_ref[1:2, :]
    xlane = lax.broadcasted_iota(jnp.int32, (1, FOX_X), 1)
    for p in pairs:
        qp = q_ref[p // npair, :, pcols[p]]
        qx = _dot(pieces_q[p // npair], sel_ref[1, p % npair]) + ones_q
        for half in range(2):
            rows = slice(half * tq, (half + 1) * tq)
            mine = lo_half if half == 0 else jnp.logical_not(lo_half)
            xmine = (xlane < FOX_QC[1]) if half == 0 else (xlane >= FOX_QC[1])
            qa_ref[p, rows, 0:2 * FOX_DH] = jnp.where(mine, qp, jnp.zeros((), BF16))
            qa_ref[p, rows, 2 * FOX_DH:] = jnp.where(xmine, qx, 0.0).astype(BF16)

    ki = lax.broadcasted_iota(jnp.int32, (tq, 2 * tq), 0)
    qi = jnp.bitwise_and(lax.broadcasted_iota(jnp.int32, (tq, 2 * tq), 1), tq - 1)
    causal = ki <= qi

    def scores(j):
        krows = pl.ds(pl.multiple_of(j * tq, tq), tq)
        return [_dot_nt(ka_ref[p, krows, :], qa_ref[p]) for p in pairs]

    def absorb(s, j, first):
        for p in pairs:
            sp = s[p]
            if first:
                sp = jnp.where(causal, sp, -jnp.inf)
            m_new = jnp.max(sp, 0, keepdims=True)
            if not first:
                m_prev = m_ref[p]
                m_new = jnp.maximum(m_prev, m_new)
                scale = jnp.exp2(m_prev - m_new)
            prob = jnp.exp2(sp - m_new)
            psum = jnp.sum(prob, 0, keepdims=True)
            pv = _dot(vt_ref[p, j], prob.astype(BF16))
            m_ref[p] = m_new
            if first:
                l_ref[p] = psum
                acc_ref[p] = pv
            else:
                l_ref[p] = scale * l_ref[p] + psum
                acc_ref[p] = scale * acc_ref[p] + pv

    last = jnp.maximum(i - 1, 0)

    def stage(slot, j_next):
        s_next = scores(jnp.minimum(j_next, last))
        for p in pairs:
            s_ref[slot, p] = s_next[p]

    def consume(slot, j):
        absorb([s_ref[slot, p] for p in pairs], j, False)

    s_diag = scores(i)
    stage(0, 0)
    absorb(s_diag, i, True)

    def body(t, carry):
        stage(1, 2 * t + 1)
        consume(0, 2 * t)
        stage(0, 2 * t + 2)
        consume(1, 2 * t + 1)
        return carry

    lax.fori_loop(0, i // 2, body, 0)

    @pl.when(i % 2 == 1)
    def _():
        consume(0, i - 1)

    ng = ng_ref[...]
    for p in pairs:
        acc = acc_ref[p]
        l = l_ref[p]
        outs = []
        for half in range(2):
            oh = (acc[half * FOX_DH:(half + 1) * FOX_DH, half * tq:(half + 1) * tq]
                  / l[:, half * tq:(half + 1) * tq])
            ms = jnp.mean(oh * oh, 0, keepdims=True)
            outs.append(oh * lax.rsqrt(ms + NORM_EPS))
        o_ref[p // npair, :, pcols[p]] = (
            jnp.transpose(jnp.concatenate(outs, axis=0)) * ng).astype(BF16)


def _fox(proj_big, gcol, norm_g2, batch, seq):
    nq = seq // FOX_TQ
    first = (GDN_QKV + GDN_WIDTH) // FOX_WIDTH
    npair = FOX_HEADS // 2
    nb = FOX_NB
    assert batch % nb == 0
    nu = nb * npair
    return pl.pallas_call(
        _fox_kernel,
        grid=(batch // nb, nq),
        in_specs=[
            pl.BlockSpec((nb, FOX_TQ, FOX_WIDTH), lambda g, i: (g, i, first)),
            pl.BlockSpec((nb, seq, FOX_WIDTH), lambda g, i: (g, 0, first + 1)),
            pl.BlockSpec((nb, seq, FOX_WIDTH), lambda g, i: (g, 0, first + 2)),
            pl.BlockSpec((nb, seq, N_SMALL), lambda g, i: (g, 0, 0)),
            _resident((1, 2 * FOX_DH)),
            _resident((2, npair, 3 * N_SMALL, FOX_X)),
            _resident((2, FOX_X)),
        ],
        out_specs=pl.BlockSpec((nb, FOX_TQ, FOX_WIDTH), lambda g, i: (g, i, 0)),
        out_shape=jax.ShapeDtypeStruct((batch, seq, FOX_WIDTH), BF16),
        scratch_shapes=[
            pltpu.VMEM((nu, 2 * FOX_TQ, 2 * FOX_DH + FOX_X), BF16),
            pltpu.VMEM((nu, seq, 2 * FOX_DH + FOX_X), BF16),
            pltpu.VMEM((nu, nq, 2 * FOX_DH, FOX_TQ), BF16),
            pltpu.VMEM((nu, 1, 2 * FOX_TQ), F32),
            pltpu.VMEM((nu, 1, 2 * FOX_TQ), F32),
            pltpu.VMEM((nu, 2 * FOX_DH, 2 * FOX_TQ), F32),
            pltpu.VMEM((2, nu, FOX_TQ, 2 * FOX_TQ), F32),
        ],
        compiler_params=pltpu.CompilerParams(
            dimension_semantics=("arbitrary", "arbitrary"), vmem_limit_bytes=VMEM_LIMIT),
        name="fox",
    )(proj_big, proj_big, proj_big, gcol, norm_g2, *_fox_constants())


def _tail_kernel(x_ref, p_ref, og_ref, of_ref, lin_g, lin_b, wo_ref, l1g, l1b, wu_ref, wd_ref,
                 wp_ref, wg_ref, bg_ref, l2g, l2b, o_ref):
    nparts = x_ref.shape[0] // TAIL_PART
    parts = [slice(r * TAIL_PART, (r + 1) * TAIL_PART) for r in range(nparts)]

    def head(rows):
        h = _layer_norm(x_ref[rows, :], lin_g[...], lin_b[...])
        mix = (_dot(og_ref[rows, :], wo_ref[0:GDN_WIDTH, :])
               + _dot(of_ref[rows, :], wo_ref[GDN_WIDTH:, :]))
        return _layer_norm(ALPHA * h + mix, l1g[...], l1b[...])

    def mlp(rows, h1):
        h1b = h1.astype(BF16)
        gate = _sigmoid(_dot(h1b, wg_ref[...]) + bg_ref[...])
        acc = ALPHA * h1 + _dot(p_ref[rows, :].astype(BF16), wp_ref[...]) * gate
        for j in range(D_FF // TAIL_TF):
            cols = slice(j * TAIL_TF, (j + 1) * TAIL_TF)
            a = jnp.maximum(_dot(h1b, wu_ref[:, cols]), 0.0)
            acc = acc + _dot((a * a).astype(BF16), wd_ref[cols, :])
        return acc

    h1 = head(parts[0])
    for r in range(nparts):
        h1_next = head(parts[r + 1]) if r + 1 < nparts else None
        o_ref[parts[r], :] = _layer_norm(mlp(parts[r], h1), l2g[...], l2b[...])
        h1 = h1_next


def _tail(x2, p2, o_gdn, o_fox, lin_g, lin_b, w_out, l1g, l1b, w_up, w_down, w_ple, w_gate,
          b_gate, l2g, l2b):
    n = x2.shape[0]
    row = lambda width: pl.BlockSpec((TAIL_TM, width), lambda i: (i, 0))
    vec = _resident((1, D_MODEL))
    return pl.pallas_call(
        _tail_kernel,
        grid=(n // TAIL_TM,),
        in_specs=[
            row(D_MODEL), row(D_PLE), row(GDN_WIDTH), row(FOX_WIDTH),
            vec, vec, _resident((D_MODEL, D_MODEL)), vec, vec,
            _resident((D_MODEL, D_FF)), _resident((D_FF, D_MODEL)),
            _resident((D_PLE, D_MODEL)), _resident((D_MODEL, D_MODEL)), vec, vec, vec,
        ],
        out_specs=row(D_MODEL),
        out_shape=jax.ShapeDtypeStruct((n, D_MODEL), F32),
        compiler_params=pltpu.CompilerParams(
            dimension_semantics=("arbitrary",), vmem_limit_bytes=VMEM_LIMIT),
        name="tail",
    )(x2, p2, o_gdn, o_fox, lin_g, lin_b, w_out, l1g, l1b, w_up, w_down, w_ple, w_gate,
      b_gate, l2g, l2b)


def kernel(x, p, ln_in_g, ln_in_b, w_in, conv_w, a_log, dt_bias, gdn_norm_g, b_f, fox_norm_g,
           w_out, ln1_g, ln1_b, w_up, w_down, w_ple, w_ple_gate, b_ple_gate, ln2_g, ln2_b):
    batch, seq, _ = x.shape
    assert x.shape[2] == D_MODEL and w_in.shape[0] == 1
    assert seq % FOX_TQ == 0 and seq % GDN_TB == 0 and (batch * seq) % PROJ_TM == 0
    n = batch * seq
    x2 = x.reshape(n, D_MODEL)
    p2 = p[0].reshape(n, D_PLE)
    row = lambda a: a.reshape(1, -1).astype(F32)

    n_gate = LANE_GTOT + GDN_HEADS
    zeros4 = jnp.zeros((GDN_HEADS,), F32)
    pad = jnp.zeros((N_SMALL - n_gate,), F32)
    pvec = jnp.zeros((8, N_SMALL), F32)
    pvec = pvec.at[0].set(jnp.concatenate([zeros4, dt_bias[0], b_f[0], dt_bias[0], pad]))
    pvec = pvec.at[1].set(
        jnp.concatenate([zeros4, a_log[0], jnp.zeros((FOX_HEADS,), F32), a_log[0], pad]))

    proj_big, small = _proj(x2, row(ln_in_g), row(ln_in_b), w_in[0], conv_w[0], seq)
    gcol, grow = _gates(small, pvec, batch, seq)

    proj3 = proj_big.reshape(batch, seq, N_BIG)
    gcol3 = gcol.reshape(batch, seq, N_SMALL)
    o_gdn, w_out_b, w_up_b, w_down_b, w_ple_b, w_gate_b = _gdn(
        proj3, gcol3, grow, row(gdn_norm_g[0]), batch, seq,
        (w_out[0], w_up[0], w_down[0], w_ple[0], w_ple_gate[0]))
    o_gdn = o_gdn.reshape(n, GDN_WIDTH)
    o_fox = _fox(proj3, gcol3, row(jnp.tile(fox_norm_g[0], 2)), batch, seq).reshape(n, FOX_WIDTH)

    out = _tail(x2, p2, o_gdn, o_fox, row(ln_in_g), row(ln_in_b), w_out_b,
                row(ln1_g[0]), row(ln1_b[0]), w_up_b, w_down_b, w_ple_b, w_gate_b,
                row(b_ple_gate[0]), row(ln2_g[0]), row(ln2_b[0]))
    return out.reshape(batch, seq, D_MODEL)
```

```python
import functools

import jax
import jax.numpy as jnp
import numpy as np
from jax import lax
from jax.experimental import pallas as pl
from jax.experimental.pallas import tpu as pltpu

F32 = jnp.float32
BF16 = jnp.bfloat16

D_MODEL = 1024
CHUNK = 64
CHUNK_SHIFT = CHUNK.bit_length() - 1
SUBLANES = 8
GDN_HEADS = 4
GDN_DK = 128
GDN_DV = 128
GDN_QK = GDN_HEADS * GDN_DK
GDN_QKV = 3 * GDN_QK
GDN_WIDTH = GDN_HEADS * GDN_DV
FOX_HEADS = 8
FOX_DH = 64
FOX_WIDTH = FOX_HEADS * FOX_DH
CONV_W = 4
D_FF = 4 * D_MODEL
D_PLE = 256
LN_EPS = 1e-5
NORM_EPS = 1e-6
ALPHA = 2.0 ** 0.25

OFF_Z = GDN_QKV
OFF_BETA = OFF_Z + GDN_WIDTH
OFF_FOX = OFF_BETA + 2 * GDN_HEADS
OFF_F = OFF_FOX + 3 * FOX_WIDTH
D_IN = OFF_F + FOX_HEADS
N_BIG = GDN_QKV + GDN_WIDTH + 3 * FOX_WIDTH
N_SMALL = 128
FOX_Q0 = GDN_QKV + GDN_WIDTH
LOG2E = 1.4426950408889634
FOX_QSCALE = FOX_DH ** -0.5 * LOG2E
LANE_BETA = 0
LANE_GAM = GDN_HEADS
LANE_C = 2 * GDN_HEADS
LANE_GTOT = LANE_C + FOX_HEADS

VMEM_LIMIT = 56 * 1024 * 1024

PROJ_TM = 1024
PROJ_TN = 256
PROJ_AHEAD = 1
GATE_TB = 256
GDN_TB = 256
GDN_NB = 4
FOX_TQ = 256
FOX_NB = 2
TAIL_TM = 512
TAIL_PART = 256
TAIL_TF = 1024


def _layer_norm(x, g, b):
    mu = jnp.mean(x, -1, keepdims=True)
    xc = x - mu
    var = jnp.mean(xc * xc, -1, keepdims=True)
    return xc * lax.rsqrt(var + LN_EPS) * g + b


def _softplus(x):
    return jnp.maximum(x, 0.0) + jnp.log(1.0 + jnp.exp(-jnp.abs(x)))


def _sigmoid(x):
    return 1.0 / (1.0 + jnp.exp2(x * -LOG2E))


def _dot(a, b):
    return jnp.dot(a, b, preferred_element_type=F32)


def _dot_nt(a, b):
    return lax.dot_general(a, b, (((1,), (1,)), ((), ())), preferred_element_type=F32)


def _resident(shape):
    return pl.BlockSpec(shape, lambda *_: (0,) * len(shape), pipeline_mode=pl.Buffered(1))


def _proj_kernel(x_ref, g_ref, b_ref, w_ref, cw_ref, ob_ref, os_ref, halo_ref, wb_ref, ws_ref, *,
                 tiles_per_seq):
    tm = x_ref.shape[0]
    halo = halo_ref.shape[0]

    @pl.when(pl.program_id(0) == 0)
    def _():
        r = lax.broadcasted_iota(jnp.int32, (N_SMALL, N_SMALL), 0)
        c = lax.broadcasted_iota(jnp.int32, (N_SMALL, N_SMALL), 1)
        same = jnp.where(r == c, 1.0, 0.0)
        again = jnp.where(r + (LANE_GTOT - LANE_GAM) == c, 1.0, 0.0)
        pick_decay = jnp.where(c < LANE_C, same,
                               jnp.where(c < LANE_GTOT, 0.0,
                                         jnp.where(c < LANE_GTOT + GDN_HEADS, again, 0.0))).astype(BF16)
        tail0 = N_SMALL - FOX_HEADS - LANE_C
        pick_forget = jnp.where(c < LANE_C, 0.0,
                                jnp.where(c < LANE_GTOT, jnp.where(r == c + tail0, 1.0, 0.0),
                                          0.0)).astype(BF16)
        step = 128
        for r0 in range(0, D_MODEL, step):
            rows = slice(r0, r0 + step)
            wb_ref[rows, 0:OFF_BETA] = w_ref[rows, 0:OFF_BETA].astype(BF16)
            wb_ref[rows, OFF_BETA:N_BIG] = w_ref[rows, OFF_FOX:OFF_F].astype(BF16)
            g1 = w_ref[rows, OFF_BETA:OFF_BETA + N_SMALL].astype(BF16)
            g2 = w_ref[rows, D_IN - N_SMALL:D_IN].astype(BF16)
            ws_ref[rows, :] = (_dot(g1, pick_decay) + _dot(g2, pick_forget)).astype(BF16)

    @pl.when(pl.program_id(0) % tiles_per_seq == 0)
    def _():
        halo_ref[...] = jnp.zeros_like(halo_ref)

    h = _layer_norm(x_ref[...], g_ref[...], b_ref[...]).astype(BF16)
    nchunk = N_BIG // PROJ_TN
    nconv = GDN_QKV // PROJ_TN
    order = [c for pair in zip(range(nconv), range(nconv, 2 * nconv)) for c in pair]
    order += list(range(2 * nconv, nchunk))
    def chunk_dot(c):
        return _dot(h, wb_ref[:, c * PROJ_TN:(c + 1) * PROJ_TN])

    inflight = [chunk_dot(c) for c in order[:PROJ_AHEAD]]
    for pos, j in enumerate(order):
        cols = slice(j * PROJ_TN, (j + 1) * PROJ_TN)
        acc = inflight.pop(0)
        if pos + PROJ_AHEAD < nchunk:
            inflight.append(chunk_dot(order[pos + PROJ_AHEAD]))
        if (j + 1) * PROJ_TN <= GDN_QKV:
            ext = jnp.concatenate([halo_ref[:, cols], acc], axis=0)
            y = acc * cw_ref[CONV_W - 1:CONV_W, cols]
            for d in range(1, CONV_W):
                y = y + pltpu.roll(ext, d, 0)[halo:] * cw_ref[CONV_W - 1 - d:CONV_W - d, cols]
            halo_ref[:, cols] = acc[tm - halo:]
            acc = y * _sigmoid(y)
        elif FOX_Q0 <= j * PROJ_TN < FOX_Q0 + FOX_WIDTH:
            acc = acc * FOX_QSCALE
        ob_ref[:, cols] = acc.astype(BF16)
    os_ref[...] = _dot(h, ws_ref[...])


def _proj(x2, ln_g, ln_b, w_in, conv_w, seq):
    n = x2.shape[0]
    assert GDN_QKV % PROJ_TN == 0 and seq % PROJ_TM == 0
    assert w_in.shape == (D_MODEL, D_IN) and OFF_BETA % N_SMALL == 0 and (OFF_F - 8) % N_SMALL == 0
    return pl.pallas_call(
        functools.partial(_proj_kernel, tiles_per_seq=seq // PROJ_TM),
        grid=(n // PROJ_TM,),
        in_specs=[
            pl.BlockSpec((PROJ_TM, D_MODEL), lambda i: (i, 0)),
            _resident((1, D_MODEL)),
            _resident((1, D_MODEL)),
            _resident((D_MODEL, D_IN)),
            _resident((CONV_W, GDN_QKV)),
        ],
        out_specs=[
            pl.BlockSpec((PROJ_TM, N_BIG), lambda i: (i, 0)),
            pl.BlockSpec((PROJ_TM, N_SMALL), lambda i: (i, 0)),
        ],
        out_shape=[
            jax.ShapeDtypeStruct((n, N_BIG), BF16),
            jax.ShapeDtypeStruct((n, N_SMALL), F32),
        ],
        scratch_shapes=[
            pltpu.VMEM((SUBLANES, GDN_QKV), F32),
            pltpu.VMEM((D_MODEL, N_BIG), BF16),
            pltpu.VMEM((D_MODEL, N_SMALL), BF16),
        ],
        compiler_params=pltpu.CompilerParams(
            dimension_semantics=("arbitrary",), vmem_limit_bytes=VMEM_LIMIT),
        name="proj",
    )(x2, ln_g, ln_b, w_in, conv_w)


def _gates_kernel(s_ref, pv_ref, o_ref, gr_ref, *, seq):
    lane = lax.broadcasted_iota(jnp.int32, (1, N_SMALL), 1)
    bias = pv_ref[0:1, :]
    neg_a = -jnp.exp(pv_ref[1:2, :])
    r = lax.broadcasted_iota(jnp.int32, (GATE_TB, GATE_TB), 0)
    c = lax.broadcasted_iota(jnp.int32, (GATE_TB, GATE_TB), 1)
    l_full = jnp.where(r >= c, 1.0, 0.0).astype(F32)
    same_chunk = jnp.right_shift(r, CHUNK_SHIFT) == jnp.right_shift(c, CHUNK_SHIFT)
    l_chunk = jnp.where(same_chunk, l_full, 0.0)
    l_total = jnp.where(same_chunk, 1.0, 0.0).astype(F32)
    l_decay = jnp.concatenate([l_chunk, l_total], axis=0).astype(BF16)
    l_forget = l_full.astype(BF16)

    def pieces_sum(a):
        return a[:, 0:N_SMALL] + a[:, N_SMALL:2 * N_SMALL] + a[:, 2 * N_SMALL:]

    carry = jnp.zeros((1, N_SMALL), F32)
    for t in range(seq // GATE_TB):
        rows = slice(t * GATE_TB, (t + 1) * GATE_TB)
        x = s_ref[rows, :] + bias
        beta = _sigmoid(x)
        log_g = neg_a * _softplus(x)
        log_f = -_softplus(-x)
        decay = pieces_sum(_dot(l_decay, _split3(log_g)))
        gam, gtot = decay[:GATE_TB], decay[GATE_TB:]
        cum = pieces_sum(_dot(l_forget, _split3(log_f))) + carry
        carry = cum[GATE_TB - 1:GATE_TB, :]
        o_ref[rows, :] = jnp.where(
            lane < LANE_GAM, beta,
            jnp.where(lane < LANE_C, gam, jnp.where(lane < LANE_GTOT, cum, gtot)))
        gr_ref[0, t] = jnp.transpose(gam)[LANE_GAM:LANE_GAM + GDN_HEADS, :]


def _gates(small, pvec, batch, seq):
    assert GATE_TB == GDN_TB
    nt = seq // GATE_TB
    return pl.pallas_call(
        functools.partial(_gates_kernel, seq=seq),
        grid=(batch,),
        in_specs=[
            pl.BlockSpec((seq, N_SMALL), lambda b: (b, 0)),
            _resident((8, N_SMALL)),
        ],
        out_specs=[
            pl.BlockSpec((seq, N_SMALL), lambda b: (b, 0)),
            pl.BlockSpec((1, nt, GDN_HEADS, GATE_TB), lambda b: (b, 0, 0, 0)),
        ],
        out_shape=[
            jax.ShapeDtypeStruct((batch * seq, N_SMALL), F32),
            jax.ShapeDtypeStruct((batch, nt, GDN_HEADS, GATE_TB), F32),
        ],
        compiler_params=pltpu.CompilerParams(
            dimension_semantics=("arbitrary",), vmem_limit_bytes=VMEM_LIMIT),
        name="gates",
    )(small, pvec)


def _gdn_kernel(x_ref, z_ref, gc_ref, gr_ref, ng_ref, *rest, ncast):
    o_ref, s_ref = rest[ncast], rest[2 * ncast + 1]
    for w_f32, w_bf16 in zip(rest[:ncast], rest[ncast + 1:2 * ncast + 1]):
        w_bf16[...] = w_f32[...].astype(BF16)

    nb, tb = x_ref.shape[0], x_ref.shape[1]
    nch = tb // CHUNK
    heads = range(nb * GDN_HEADS)
    nh = GDN_HEADS

    @pl.when(pl.program_id(1) == 0)
    def _():
        s_ref[...] = jnp.zeros_like(s_ref)

    def act(h, c0):
        c0 = c0 + (h % nh) * GDN_DK
        return x_ref[h // nh, :, c0:c0 + GDN_DK].astype(F32)

    def l2norm(v, scale):
        return v * (lax.rsqrt(jnp.sum(v * v, -1, keepdims=True) + NORM_EPS) * scale)

    ri = lax.broadcasted_iota(jnp.int32, (tb, tb), 0)
    ci = lax.broadcasted_iota(jnp.int32, (tb, tb), 1)
    same = jnp.right_shift(ri, CHUNK_SHIFT) == jnp.right_shift(ci, CHUNK_SHIFT)
    keep = jnp.logical_and(same, ri >= ci)
    diag = ri == ci
    pr = lax.broadcasted_iota(jnp.int32, (CHUNK, tb), 0)
    pc = lax.broadcasted_iota(jnp.int32, (CHUNK, tb), 1)
    eye_packed = jnp.where(jnp.bitwise_and(pc, CHUNK - 1) == pr, 1.0, 0.0).astype(F32)
    lane_chunk = jnp.right_shift(lax.broadcasted_iota(jnp.int32, (1, tb), 1), CHUNK_SHIFT)
    row_chunk = jnp.right_shift(lax.broadcasted_iota(jnp.int32, (tb, 1), 0), CHUNK_SHIFT)

    def to_bd(packed):
        return jnp.where(same, jnp.concatenate([packed] * nch, axis=0), jnp.zeros((), packed.dtype))

    def to_packed(bd):
        out = bd[0:CHUNK]
        for c in range(1, nch):
            out = out + bd[c * CHUNK:(c + 1) * CHUNK]
        return out

    ng = ng_ref[...]
    q = [l2norm(act(h, 0), GDN_DK ** -0.5) for h in heads]
    k = [l2norm(act(h, GDN_QK), 1.0) for h in heads]
    v = [act(h, 2 * GDN_QK) for h in heads]

    def gate(h, lane0):
        return gc_ref[h // nh, :, lane0 + h % nh:lane0 + h % nh + 1]

    beta = [gate(h, LANE_BETA) for h in heads]
    gam = [gate(h, LANE_GAM) for h in heads]
    gtot = [gate(h, LANE_GTOT) for h in heads]
    grow = [gr_ref[h // nh, 0, h % nh:h % nh + 1, :] for h in heads]
    egam = [jnp.exp(g) for g in gam]
    kb = [a.astype(BF16) for a in k]
    decay = [jnp.exp(jnp.where(keep, gam[h] - grow[h], -jnp.inf)) for h in heads]
    gram = [_dot_nt(kb[h], kb[h]) for h in heads]
    qk = [(_dot_nt(q[h].astype(BF16), kb[h]) * decay[h]).astype(BF16) for h in heads]

    x_bd = [jnp.where(diag, 0.0, -(gram[h] * beta[h]) * decay[h]) for h in heads]
    x_p = [to_packed(a) for a in x_bd]
    p_p = [eye_packed + a for a in x_p]
    x_p = [_dot(x_p[h].astype(BF16), x_bd[h].astype(BF16)) for h in heads]
    for _ in range(4):
        w_bd = [to_bd(a.astype(BF16)) for a in x_p]
        r = [_dot(jnp.concatenate([p_p[h], x_p[h]], axis=0).astype(BF16), w_bd[h]) for h in heads]
        p_p = [p_p[h] + r[h][:CHUNK] for h in heads]
        x_p = [r[h][CHUNK:] for h in heads]
    p_p = [p_p[h] + _dot(p_p[h].astype(BF16), to_bd(x_p[h].astype(BF16))) for h in heads]

    rhs = [jnp.concatenate([v[h] * beta[h], k[h] * (beta[h] * egam[h])], axis=1).astype(BF16)
           for h in heads]
    sol = [_dot(to_bd(p_p[h].astype(BF16)), rhs[h]) for h in heads]
    u = [a[:, :GDN_DV] for a in sol]
    w = [a[:, GDN_DV:].astype(BF16) for a in sol]
    qd = [(q[h] * egam[h]).astype(BF16) for h in heads]
    kd_t = [jnp.transpose(k[h] * jnp.exp(gtot[h] - gam[h])).astype(BF16) for h in heads]
    z = [z_ref[h // nh, :, (h % nh) * GDN_DV:(h % nh + 1) * GDN_DV].astype(F32) for h in heads]
    zgate = [a * _sigmoid(a) for a in z]

    state = [s_ref[h] for h in heads]
    for c in range(nch):
        rows = slice(c * CHUNK, (c + 1) * CHUNK)
        sb = [a.astype(BF16) for a in state]
        r1 = [_dot(jnp.concatenate([w[h][rows], qd[h][rows]], axis=0), sb[h]) for h in heads]
        v_new = [u[h][rows] - r1[h][:CHUNK] for h in heads]
        v_full = [jnp.where(row_chunk == c, jnp.concatenate([a.astype(BF16)] * nch, axis=0),
                            jnp.zeros((), BF16)) for a in v_new]
        kd_c = [jnp.where(lane_chunk == c, a, jnp.zeros((), BF16)) for a in kd_t]
        r2 = [_dot(jnp.concatenate([qk[h][rows], kd_c[h]], axis=0), v_full[h]) for h in heads]
        for h in heads:
            o = r1[h][CHUNK:] + r2[h][:CHUNK]
            on = o * lax.rsqrt(jnp.mean(o * o, -1, keepdims=True) + NORM_EPS) * ng
            o_ref[h // nh, rows, (h % nh) * GDN_DV:(h % nh + 1) * GDN_DV] = (
                on * zgate[h][rows]).astype(BF16)
        state = [state[h] * jnp.exp(gtot[h][c * CHUNK:c * CHUNK + 1]) + r2[h][CHUNK:] for h in heads]
    for h in heads:
        s_ref[h] = state[h]


def _gdn(proj_big, gcol, grow, norm_g, batch, seq, cast):
    nt = seq // GDN_TB
    nb = GDN_NB
    assert batch % nb == 0
    steps = (batch // nb) * nt
    assert all(w.shape[0] % (16 * steps) == 0 for w in cast)
    cast_specs = [pl.BlockSpec((w.shape[0] // steps, w.shape[1]), lambda g, t: (g * nt + t, 0))
                  for w in cast]
    return pl.pallas_call(
        functools.partial(_gdn_kernel, ncast=len(cast)),
        grid=(batch // nb, nt),
        in_specs=[
            pl.BlockSpec((nb, GDN_TB, GDN_QKV), lambda g, t: (g, t, 0)),
            pl.BlockSpec((nb, GDN_TB, GDN_WIDTH), lambda g, t: (g, t, OFF_Z // GDN_WIDTH)),
            pl.BlockSpec((nb, GDN_TB, N_SMALL), lambda g, t: (g, t, 0)),
            pl.BlockSpec((nb, 1, GDN_HEADS, GDN_TB), lambda g, t: (g, t, 0, 0)),
            _resident((1, GDN_DV)),
        ] + cast_specs,
        out_specs=[pl.BlockSpec((nb, GDN_TB, GDN_WIDTH), lambda g, t: (g, t, 0))] + cast_specs,
        out_shape=[jax.ShapeDtypeStruct((batch, seq, GDN_WIDTH), BF16)]
        + [jax.ShapeDtypeStruct(w.shape, BF16) for w in cast],
        scratch_shapes=[
            pltpu.VMEM((nb * GDN_HEADS, GDN_DK, GDN_DV), F32),
        ],
        compiler_params=pltpu.CompilerParams(
            dimension_semantics=("arbitrary", "arbitrary"), vmem_limit_bytes=VMEM_LIMIT),
        name="gdn",
    )(proj_big, proj_big, gcol, grow, norm_g, *cast)


FOX_X = 128
FOX_QC = (0, 6)
FOX_KC = (3, 9)


def _split3(c):
    hi = c.astype(BF16)
    r1 = c - hi.astype(F32)
    mid = r1.astype(BF16)
    lo = (r1 - mid.astype(F32)).astype(BF16)
    return jnp.concatenate([hi, mid, lo], axis=1)


def _select3(pair, base):
    e = np.zeros((3 * N_SMALL, FOX_X), np.float32)
    for d in range(3):
        for head in range(2):
            e[d * N_SMALL + LANE_C + 2 * pair + head, base[head] + d] = 1.0
    return e


def _lane_ones(first):
    out = np.zeros((FOX_X,), np.float32)
    for f in first:
        out[f:f + 3] = 1.0
    return out


def _fox_constants():
    npair = FOX_HEADS // 2
    sel = np.stack([np.stack([_select3(p, base) for p in range(npair)])
                    for base in (FOX_KC, FOX_QC)])
    ones = np.stack([_lane_ones(FOX_QC), _lane_ones(FOX_KC)])
    return jnp.asarray(sel, BF16), jnp.asarray(ones, F32)


def _fox_kernel(q_ref, k_ref, v_ref, gc_ref, ng_ref, sel_ref, ones_ref, o_ref, qa_ref, ka_ref, vt_ref,
                m_ref, l_ref, acc_ref, s_ref):
    nb, tq = q_ref.shape[0], q_ref.shape[1]
    seq = k_ref.shape[1]
    i = pl.program_id(1)
    npair = FOX_HEADS // 2
    pairs = range(nb * npair)
    lane = lax.broadcasted_iota(jnp.int32, (1, 2 * FOX_DH), 1)
    lo_half = lane < FOX_DH
    pcols = [slice((p % npair) * 2 * FOX_DH, (p % npair + 1) * 2 * FOX_DH) for p in pairs]

    @pl.when(i == 0)
    def _():
        ones_k = ones_ref[0:1, :]

        def fill(t, carry):
            rows = pl.ds(pl.multiple_of(t * tq, tq), tq)
            pieces = [_split3(gc_ref[r, rows, :] * LOG2E) for r in range(nb)]
            for p in pairs:
                r = p // npair
                ka_ref[p, rows, 0:2 * FOX_DH] = k_ref[r, rows, pcols[p]]
                ka_ref[p, rows, 2 * FOX_DH:] = (
                    ones_k - _dot(pieces[r], sel_ref[0, p % npair])).astype(BF16)
                vt_ref[p, t] = jnp.transpose(v_ref[r, rows, pcols[p]].astype(F32)).astype(BF16)
            return carry

        lax.fori_loop(0, seq // tq, fill, 0)

    qrows = pl.ds(pl.multiple_of(i * tq, tq), tq)
    pieces_q = [_split3(gc_ref[r, qrows, :] * LOG2E) for r in range(nb)]
    ones_q = ones_ref[1:2, :]
    xlane = lax.broadcasted_iota(jnp.int32, (1, FOX_X), 1)
    for p in pairs:
        qp = q_ref[p // npair, :, pcols[p]]
        qx = _dot(pieces_q[p // npair], sel_ref[1, p % npair]) + ones_q
        for half in range(2):
            rows = slice(half * tq, (half + 1) * tq)
            mine = lo_half if half == 0 else jnp.logical_not(lo_half)
            xmine = (xlane < FOX_QC[1]) if half == 0 else (xlane >= FOX_QC[1])
            qa_ref[p, rows, 0:2 * FOX_DH] = jnp.where(mine, qp, jnp.zeros((), BF16))
            qa_ref[p, rows, 2 * FOX_DH:] = jnp.where(xmine, qx, 0.0).astype(BF16)

    ki = lax.broadcasted_iota(jnp.int32, (tq, 2 * tq), 0)
    qi = jnp.bitwise_and(lax.broadcasted_iota(jnp.int32, (tq, 2 * tq), 1), tq - 1)
    causal = ki <= qi

    def scores(j):
        krows = pl.ds(pl.multiple_of(j * tq, tq), tq)
        return [_dot_nt(ka_ref[p, krows, :], qa_ref[p]) for p in pairs]

    def absorb(s, j, first):
        for p in pairs:
            sp = s[p]
            if first:
                sp = jnp.where(causal, sp, -jnp.inf)
            m_new = jnp.max(sp, 0, keepdims=True)
            if not first:
                m_prev = m_ref[p]
                m_new = jnp.maximum(m_prev, m_new)
                scale = jnp.exp2(m_prev - m_new)
            prob = jnp.exp2(sp - m_new)
            psum = jnp.sum(prob, 0, keepdims=True)
            pv = _dot(vt_ref[p, j], prob.astype(BF16))
            m_ref[p] = m_new
            if first:
                l_ref[p] = psum
                acc_ref[p] = pv
            else:
                l_ref[p] = scale * l_ref[p] + psum
                acc_ref[p] = scale * acc_ref[p] + pv

    last = jnp.maximum(i - 1, 0)

    def stage(slot, j_next):
        s_next = scores(jnp.minimum(j_next, last))
        for p in pairs:
            s_ref[slot, p] = s_next[p]

    def consume(slot, j):
        absorb([s_ref[slot, p] for p in pairs], j, False)

    s_diag = scores(i)
    stage(0, 0)
    absorb(s_diag, i, True)

    def body(t, carry):
        stage(1, 2 * t + 1)
        consume(0, 2 * t)
        stage(0, 2 * t + 2)
        consume(1, 2 * t + 1)
        return carry

    lax.fori_loop(0, i // 2, body, 0)

    @pl.when(i % 2 == 1)
    def _():
        consume(0, i - 1)

    ng = ng_ref[...]
    for p in pairs:
        acc = acc_ref[p]
        l = l_ref[p]
        outs = []
        for half in range(2):
            oh = (acc[half * FOX_DH:(half + 1) * FOX_DH, half * tq:(half + 1) * tq]
                  / l[:, half * tq:(half + 1) * tq])
            ms = jnp.mean(oh * oh, 0, keepdims=True)
            outs.append(oh * lax.rsqrt(ms + NORM_EPS))
        o_ref[p // npair, :, pcols[p]] = (
            jnp.transpose(jnp.concatenate(outs, axis=0)) * ng).astype(BF16)


def _fox(proj_big, gcol, norm_g2, batch, seq):
    nq = seq // FOX_TQ
    first = (GDN_QKV + GDN_WIDTH) // FOX_WIDTH
    npair = FOX_HEADS // 2
    nb = FOX_NB
    assert batch % nb == 0
    nu = nb * npair
    return pl.pallas_call(
        _fox_kernel,
        grid=(batch // nb, nq),
        in_specs=[
            pl.BlockSpec((nb, FOX_TQ, FOX_WIDTH), lambda g, i: (g, i, first)),
            pl.BlockSpec((nb, seq, FOX_WIDTH), lambda g, i: (g, 0, first + 1)),
            pl.BlockSpec((nb, seq, FOX_WIDTH), lambda g, i: (g, 0, first + 2)),
            pl.BlockSpec((nb, seq, N_SMALL), lambda g, i: (g, 0, 0)),
            _resident((1, 2 * FOX_DH)),
            _resident((2, npair, 3 * N_SMALL, FOX_X)),
            _resident((2, FOX_X)),
        ],
        out_specs=pl.BlockSpec((nb, FOX_TQ, FOX_WIDTH), lambda g, i: (g, i, 0)),
        out_shape=jax.ShapeDtypeStruct((batch, seq, FOX_WIDTH), BF16),
        scratch_shapes=[
            pltpu.VMEM((nu, 2 * FOX_TQ, 2 * FOX_DH + FOX_X), BF16),
            pltpu.VMEM((nu, seq, 2 * FOX_DH + FOX_X), BF16),
            pltpu.VMEM((nu, nq, 2 * FOX_DH, FOX_TQ), BF16),
            pltpu.VMEM((nu, 1, 2 * FOX_TQ), F32),
            pltpu.VMEM((nu, 1, 2 * FOX_TQ), F32),
            pltpu.VMEM((nu, 2 * FOX_DH, 2 * FOX_TQ), F32),
            pltpu.VMEM((2, nu, FOX_TQ, 2 * FOX_TQ), F32),
        ],
        compiler_params=pltpu.CompilerParams(
            dimension_semantics=("arbitrary", "arbitrary"), vmem_limit_bytes=VMEM_LIMIT),
        name="fox",
    )(proj_big, proj_big, proj_big, gcol, norm_g2, *_fox_constants())


def _tail_kernel(x_ref, p_ref, og_ref, of_ref, lin_g, lin_b, wo_ref, l1g, l1b, wu_ref, wd_ref,
                 wp_ref, wg_ref, bg_ref, l2g, l2b, o_ref):
    nparts = x_ref.shape[0] // TAIL_PART
    parts = [slice(r * TAIL_PART, (r + 1) * TAIL_PART) for r in range(nparts)]

    def head(rows):
        h = _layer_norm(x_ref[rows, :], lin_g[...], lin_b[...])
        mix = (_dot(og_ref[rows, :], wo_ref[0:GDN_WIDTH, :])
               + _dot(of_ref[rows, :], wo_ref[GDN_WIDTH:, :]))
        return _layer_norm(ALPHA * h + mix, l1g[...], l1b[...])

    def mlp(rows, h1):
        h1b = h1.astype(BF16)
        gate = _sigmoid(_dot(h1b, wg_ref[...]) + bg_ref[...])
        acc = ALPHA * h1 + _dot(p_ref[rows, :].astype(BF16), wp_ref[...]) * gate
        for j in range(D_FF // TAIL_TF):
            cols = slice(j * TAIL_TF, (j + 1) * TAIL_TF)
            a = jnp.maximum(_dot(h1b, wu_ref[:, cols]), 0.0)
            acc = acc + _dot((a * a).astype(BF16), wd_ref[cols, :])
        return acc

    h1 = head(parts[0])
    for r in range(nparts):
        h1_next = head(parts[r + 1]) if r + 1 < nparts else None
        o_ref[parts[r], :] = _layer_norm(mlp(parts[r], h1), l2g[...], l2b[...])
        h1 = h1_next


def _tail(x2, p2, o_gdn, o_fox, lin_g, lin_b, w_out, l1g, l1b, w_up, w_down, w_ple, w_gate,
          b_gate, l2g, l2b):
    n = x2.shape[0]
    row = lambda width: pl.BlockSpec((TAIL_TM, width), lambda i: (i, 0))
    vec = _resident((1, D_MODEL))
    return pl.pallas_call(
        _tail_kernel,
        grid=(n // TAIL_TM,),
        in_specs=[
            row(D_MODEL), row(D_PLE), row(GDN_WIDTH), row(FOX_WIDTH),
            vec, vec, _resident((D_MODEL, D_MODEL)), vec, vec,
            _resident((D_MODEL, D_FF)), _resident((D_FF, D_MODEL)),
            _resident((D_PLE, D_MODEL)), _resident((D_MODEL, D_MODEL)), vec, vec, vec,
        ],
        out_specs=row(D_MODEL),
        out_shape=jax.ShapeDtypeStruct((n, D_MODEL), F32),
        compiler_params=pltpu.CompilerParams(
            dimension_semantics=("arbitrary",), vmem_limit_bytes=VMEM_LIMIT),
        name="tail",
    )(x2, p2, o_gdn, o_fox, lin_g, lin_b, w_out, l1g, l1b, w_up, w_down, w_ple, w_gate,
      b_gate, l2g, l2b)


def kernel(x, p, ln_in_g, ln_in_b, w_in, conv_w, a_log, dt_bias, gdn_norm_g, b_f, fox_norm_g,
           w_out, ln1_g, ln1_b, w_up, w_down, w_ple, w_ple_gate, b_ple_gate, ln2_g, ln2_b):
    batch, seq, _ = x.shape
    assert x.shape[2] == D_MODEL and w_in.shape[0] == 1
    assert seq % FOX_TQ == 0 and seq % GDN_TB == 0 and (batch * seq) % PROJ_TM == 0
    n = batch * seq
    x2 = x.reshape(n, D_MODEL)
    p2 = p[0].reshape(n, D_PLE)
    row = lambda a: a.reshape(1, -1).astype(F32)

    n_gate = LANE_GTOT + GDN_HEADS
    zeros4 = jnp.zeros((GDN_HEADS,), F32)
    pad = jnp.zeros((N_SMALL - n_gate,), F32)
    pvec = jnp.zeros((8, N_SMALL), F32)
    pvec = pvec.at[0].set(jnp.concatenate([zeros4, dt_bias[0], b_f[0], dt_bias[0], pad]))
    pvec = pvec.at[1].set(
        jnp.concatenate([zeros4, a_log[0], jnp.zeros((FOX_HEADS,), F32), a_log[0], pad]))

    proj_big, small = _proj(x2, row(ln_in_g), row(ln_in_b), w_in[0], conv_w[0], seq)
    gcol, grow = _gates(small, pvec, batch, seq)

    proj3 = proj_big.reshape(batch, seq, N_BIG)
    gcol3 = gcol.reshape(batch, seq, N_SMALL)
    o_gdn, w_out_b, w_up_b, w_down_b, w_ple_b, w_gate_b = _gdn(
        proj3, gcol3, grow, row(gdn_norm_g[0]), batch, seq,
        (w_out[0], w_up[0], w_down[0], w_ple[0], w_ple_gate[0]))
    o_gdn = o_gdn.reshape(n, GDN_WIDTH)
    o_fox = _fox(proj3, gcol3, row(jnp.tile(fox_norm_g[0], 2)), batch, seq).reshape(n, FOX_WIDTH)

    out = _tail(x2, p2, o_gdn, o_fox, row(ln_in_g), row(ln_in_b), w_out_b,
                row(ln1_g[0]), row(ln1_b[0]), w_up_b, w_down_b, w_ple_b, w_gate_b,
                row(b_ple_gate[0]), row(ln2_g[0]), row(ln2_b[0]))
    return out.reshape(batch, seq, D_MODEL)
```

```python
import functools

import jax
import jax.numpy as jnp
import numpy as np
from jax import lax
from jax.experimental import pallas as pl
from jax.experimental.pallas import tpu as pltpu

F32 = jnp.float32
BF16 = jnp.bfloat16

D_MODEL = 1024
CHUNK = 64
CHUNK_SHIFT = CHUNK.bit_length() - 1
SUBLANES = 8
GDN_HEADS = 4
GDN_DK = 128
GDN_DV = 128
GDN_QK = GDN_HEADS * GDN_DK
GDN_QKV = 3 * GDN_QK
GDN_WIDTH = GDN_HEADS * GDN_DV
FOX_HEADS = 8
FOX_DH = 64
FOX_WIDTH = FOX_HEADS * FOX_DH
CONV_W = 4
D_FF = 4 * D_MODEL
D_PLE = 256
LN_EPS = 1e-5
NORM_EPS = 1e-6
ALPHA = 2.0 ** 0.25

OFF_Z = GDN_QKV
OFF_BETA = OFF_Z + GDN_WIDTH
OFF_FOX = OFF_BETA + 2 * GDN_HEADS
OFF_F = OFF_FOX + 3 * FOX_WIDTH
D_IN = OFF_F + FOX_HEADS
N_BIG = GDN_QKV + GDN_WIDTH + 3 * FOX_WIDTH
N_SMALL = 128
FOX_Q0 = GDN_QKV + GDN_WIDTH
LOG2E = 1.4426950408889634
FOX_QSCALE = FOX_DH ** -0.5 * LOG2E
LANE_BETA = 0
LANE_GAM = GDN_HEADS
LANE_C = 2 * GDN_HEADS
LANE_GTOT = LANE_C + FOX_HEADS

VMEM_LIMIT = 56 * 1024 * 1024

PROJ_TM = 1024
PROJ_TN = 256
PROJ_AHEAD = 1
GATE_TB = 256
GDN_TB = 256
GDN_NB = 4
FOX_TQ = 256
FOX_NB = 2
TAIL_TM = 512
TAIL_PART = 256
TAIL_TF = 1024


def _layer_norm(x, g, b):
    mu = jnp.mean(x, -1, keepdims=True)
    xc = x - mu
    var = jnp.mean(xc * xc, -1, keepdims=True)
    return xc * lax.rsqrt(var + LN_EPS) * g + b


def _softplus(x):
    return jnp.maximum(x, 0.0) + jnp.log(1.0 + jnp.exp(-jnp.abs(x)))


def _sigmoid(x):
    return 1.0 / (1.0 + jnp.exp2(x * -LOG2E))


def _dot(a, b):
    return jnp.dot(a, b, preferred_element_type=F32)


def _dot_nt(a, b):
    return lax.dot_general(a, b, (((1,), (1,)), ((), ())), preferred_element_type=F32)


def _resident(shape):
    return pl.BlockSpec(shape, lambda *_: (0,) * len(shape), pipeline_mode=pl.Buffered(1))


def _proj_kernel(x_ref, g_ref, b_ref, w_ref, cw_ref, ob_ref, os_ref, halo_ref, wb_ref, ws_ref, *,
                 tiles_per_seq):
    tm = x_ref.shape[0]
    halo = halo_ref.shape[0]

    @pl.when(pl.program_id(0) == 0)
    def _():
        r = lax.broadcasted_iota(jnp.int32, (N_SMALL, N_SMALL), 0)
        c = lax.broadcasted_iota(jnp.int32, (N_SMALL, N_SMALL), 1)
        same = jnp.where(r == c, 1.0, 0.0)
        again = jnp.where(r + (LANE_GTOT - LANE_GAM) == c, 1.0, 0.0)
        pick_decay = jnp.where(c < LANE_C, same,
                               jnp.where(c < LANE_GTOT, 0.0,
                                         jnp.where(c < LANE_GTOT + GDN_HEADS, again, 0.0))).astype(BF16)
        tail0 = N_SMALL - FOX_HEADS - LANE_C
        pick_forget = jnp.where(c < LANE_C, 0.0,
                                jnp.where(c < LANE_GTOT, jnp.where(r == c + tail0, 1.0, 0.0),
                                          0.0)).astype(BF16)
        step = 128
        for r0 in range(0, D_MODEL, step):
            rows = slice(r0, r0 + step)
            wb_ref[rows, 0:OFF_BETA] = w_ref[rows, 0:OFF_BETA].astype(BF16)
            wb_ref[rows, OFF_BETA:N_BIG] = w_ref[rows, OFF_FOX:OFF_F].astype(BF16)
            g1 = w_ref[rows, OFF_BETA:OFF_BETA + N_SMALL].astype(BF16)
            g2 = w_ref[rows, D_IN - N_SMALL:D_IN].astype(BF16)
            ws_ref[rows, :] = (_dot(g1, pick_decay) + _dot(g2, pick_forget)).astype(BF16)

    @pl.when(pl.program_id(0) % tiles_per_seq == 0)
    def _():
        halo_ref[...] = jnp.zeros_like(halo_ref)

    h = _layer_norm(x_ref[...], g_ref[...], b_ref[...]).astype(BF16)
    nchunk = N_BIG // PROJ_TN
    nconv = GDN_QKV // PROJ_TN
    order = [c for pair in zip(range(nconv), range(nconv, 2 * nconv)) for c in pair]
    order += list(range(2 * nconv, nchunk))
    def chunk_dot(c):
        return _dot(h, wb_ref[:, c * PROJ_TN:(c + 1) * PROJ_TN])

    inflight = [chunk_dot(c) for c in order[:PROJ_AHEAD]]
    for pos, j in enumerate(order):
        cols = slice(j * PROJ_TN, (j + 1) * PROJ_TN)
        acc = inflight.pop(0)
        if pos + PROJ_AHEAD < nchunk:
            inflight.append(chunk_dot(order[pos + PROJ_AHEAD]))
        if (j + 1) * PROJ_TN <= GDN_QKV:
            ext = jnp.concatenate([halo_ref[:, cols], acc], axis=0)
            y = acc * cw_ref[CONV_W - 1:CONV_W, cols]
            for d in range(1, CONV_W):
                y = y + pltpu.roll(ext, d, 0)[halo:] * cw_ref[CONV_W - 1 - d:CONV_W - d, cols]
            halo_ref[:, cols] = acc[tm - halo:]
            acc = y * _sigmoid(y)
        elif FOX_Q0 <= j * PROJ_TN < FOX_Q0 + FOX_WIDTH:
            acc = acc * FOX_QSCALE
        ob_ref[:, cols] = acc.astype(BF16)
    os_ref[...] = _dot(h, ws_ref[...])


def _proj(x2, ln_g, ln_b, w_in, conv_w, seq):
    n = x2.shape[0]
    assert GDN_QKV % PROJ_TN == 0 and seq % PROJ_TM == 0
    assert w_in.shape == (D_MODEL, D_IN) and OFF_BETA % N_SMALL == 0 and (OFF_F - 8) % N_SMALL == 0
    return pl.pallas_call(
        functools.partial(_proj_kernel, tiles_per_seq=seq // PROJ_TM),
        grid=(n // PROJ_TM,),
        in_specs=[
            pl.BlockSpec((PROJ_TM, D_MODEL), lambda i: (i, 0)),
            _resident((1, D_MODEL)),
            _resident((1, D_MODEL)),
            _resident((D_MODEL, D_IN)),
            _resident((CONV_W, GDN_QKV)),
        ],
        out_specs=[
            pl.BlockSpec((PROJ_TM, N_BIG), lambda i: (i, 0)),
            pl.BlockSpec((PROJ_TM, N_SMALL), lambda i: (i, 0)),
        ],
        out_shape=[
            jax.ShapeDtypeStruct((n, N_BIG), BF16),
            jax.ShapeDtypeStruct((n, N_SMALL), F32),
        ],
        scratch_shapes=[
            pltpu.VMEM((SUBLANES, GDN_QKV), F32),
            pltpu.VMEM((D_MODEL, N_BIG), BF16),
            pltpu.VMEM((D_MODEL, N_SMALL), BF16),
        ],
        compiler_params=pltpu.CompilerParams(
            dimension_semantics=("arbitrary",), vmem_limit_bytes=VMEM_LIMIT),
        name="proj",
    )(x2, ln_g, ln_b, w_in, conv_w)


def _gates_kernel(s_ref, pv_ref, o_ref, gr_ref, *, seq):
    lane = lax.broadcasted_iota(jnp.int32, (1, N_SMALL), 1)
    bias = pv_ref[0:1, :]
    neg_a = -jnp.exp(pv_ref[1:2, :])
    r = lax.broadcasted_iota(jnp.int32, (GATE_TB, GATE_TB), 0)
    c = lax.broadcasted_iota(jnp.int32, (GATE_TB, GATE_TB), 1)
    l_full = jnp.where(r >= c, 1.0, 0.0).astype(F32)
    same_chunk = jnp.right_shift(r, CHUNK_SHIFT) == jnp.right_shift(c, CHUNK_SHIFT)
    l_chunk = jnp.where(same_chunk, l_full, 0.0)
    l_total = jnp.where(same_chunk, 1.0, 0.0).astype(F32)
    l_decay = jnp.concatenate([l_chunk, l_total], axis=0).astype(BF16)
    l_forget = l_full.astype(BF16)

    def pieces_sum(a):
        return a[:, 0:N_SMALL] + a[:, N_SMALL:2 * N_SMALL] + a[:, 2 * N_SMALL:]

    carry = jnp.zeros((1, N_SMALL), F32)
    for t in range(seq // GATE_TB):
        rows = slice(t * GATE_TB, (t + 1) * GATE_TB)
        x = s_ref[rows, :] + bias
        beta = _sigmoid(x)
        log_g = (neg_a * LOG2E) * _softplus(x)
        log_f = -LOG2E * _softplus(-x)
        decay = pieces_sum(_dot(l_decay, _split3(log_g)))
        gam, gtot = decay[:GATE_TB], decay[GATE_TB:]
        cum = pieces_sum(_dot(l_forget, _split3(log_f))) + carry
        carry = cum[GATE_TB - 1:GATE_TB, :]
        o_ref[rows, :] = jnp.where(
            lane < LANE_GAM, beta,
            jnp.where(lane < LANE_C, gam, jnp.where(lane < LANE_GTOT, cum, gtot)))
        gr_ref[0, t] = jnp.transpose(gam)[LANE_GAM:LANE_GAM + GDN_HEADS, :]


def _gates(small, pvec, batch, seq):
    assert GATE_TB == GDN_TB
    nt = seq // GATE_TB
    return pl.pallas_call(
        functools.partial(_gates_kernel, seq=seq),
        grid=(batch,),
        in_specs=[
            pl.BlockSpec((seq, N_SMALL), lambda b: (b, 0)),
            _resident((8, N_SMALL)),
        ],
        out_specs=[
            pl.BlockSpec((seq, N_SMALL), lambda b: (b, 0)),
            pl.BlockSpec((1, nt, GDN_HEADS, GATE_TB), lambda b: (b, 0, 0, 0)),
        ],
        out_shape=[
            jax.ShapeDtypeStruct((batch * seq, N_SMALL), F32),
            jax.ShapeDtypeStruct((batch, nt, GDN_HEADS, GATE_TB), F32),
        ],
        compiler_params=pltpu.CompilerParams(
            dimension_semantics=("arbitrary",), vmem_limit_bytes=VMEM_LIMIT),
        name="gates",
    )(small, pvec)


def _gdn_kernel(x_ref, z_ref, gc_ref, gr_ref, ng_ref, *rest, ncast):
    o_ref, s_ref = rest[ncast], rest[2 * ncast + 1]
    for w_f32, w_bf16 in zip(rest[:ncast], rest[ncast + 1:2 * ncast + 1]):
        w_bf16[...] = w_f32[...].astype(BF16)

    nb, tb = x_ref.shape[0], x_ref.shape[1]
    nch = tb // CHUNK
    heads = range(nb * GDN_HEADS)
    nh = GDN_HEADS

    @pl.when(pl.program_id(1) == 0)
    def _():
        s_ref[...] = jnp.zeros_like(s_ref)

    def act(h, c0):
        c0 = c0 + (h % nh) * GDN_DK
        return x_ref[h // nh, :, c0:c0 + GDN_DK].astype(F32)

    def l2norm(v, scale):
        return v * (lax.rsqrt(jnp.sum(v * v, -1, keepdims=True) + NORM_EPS) * scale)

    ri = lax.broadcasted_iota(jnp.int32, (tb, tb), 0)
    ci = lax.broadcasted_iota(jnp.int32, (tb, tb), 1)
    same = jnp.right_shift(ri, CHUNK_SHIFT) == jnp.right_shift(ci, CHUNK_SHIFT)
    keep = jnp.logical_and(same, ri >= ci)
    diag = ri == ci
    pr = lax.broadcasted_iota(jnp.int32, (CHUNK, tb), 0)
    pc = lax.broadcasted_iota(jnp.int32, (CHUNK, tb), 1)
    eye_packed = jnp.where(jnp.bitwise_and(pc, CHUNK - 1) == pr, 1.0, 0.0).astype(F32)
    lane_chunk = jnp.right_shift(lax.broadcasted_iota(jnp.int32, (1, tb), 1), CHUNK_SHIFT)
    row_chunk = jnp.right_shift(lax.broadcasted_iota(jnp.int32, (tb, 1), 0), CHUNK_SHIFT)

    def to_bd(packed):
        return jnp.where(same, jnp.concatenate([packed] * nch, axis=0), jnp.zeros((), packed.dtype))

    def to_packed(bd):
        out = bd[0:CHUNK]
        for c in range(1, nch):
            out = out + bd[c * CHUNK:(c + 1) * CHUNK]
        return out

    ng = ng_ref[...]
    q = [l2norm(act(h, 0), GDN_DK ** -0.5) for h in heads]
    k = [l2norm(act(h, GDN_QK), 1.0) for h in heads]
    v = [act(h, 2 * GDN_QK) for h in heads]

    def gate(h, lane0):
        return gc_ref[h // nh, :, lane0 + h % nh:lane0 + h % nh + 1]

    beta = [gate(h, LANE_BETA) for h in heads]
    gam = [gate(h, LANE_GAM) for h in heads]
    gtot = [gate(h, LANE_GTOT) for h in heads]
    grow = [gr_ref[h // nh, 0, h % nh:h % nh + 1, :] for h in heads]
    egam = [jnp.exp2(g) for g in gam]
    kb = [a.astype(BF16) for a in k]
    decay = [jnp.exp2(jnp.where(keep, gam[h] - grow[h], -jnp.inf)) for h in heads]
    gram = [_dot_nt(kb[h], kb[h]) for h in heads]
    qk = [(_dot_nt(q[h].astype(BF16), kb[h]) * decay[h]).astype(BF16) for h in heads]

    x_bd = [jnp.where(diag, 0.0, -(gram[h] * beta[h]) * decay[h]) for h in heads]
    x_p = [to_packed(a) for a in x_bd]
    p_p = [eye_packed + a for a in x_p]
    x_p = [_dot(x_p[h].astype(BF16), x_bd[h].astype(BF16)) for h in heads]
    for _ in range(4):
        w_bd = [to_bd(a.astype(BF16)) for a in x_p]
        r = [_dot(jnp.concatenate([p_p[h], x_p[h]], axis=0).astype(BF16), w_bd[h]) for h in heads]
        p_p = [p_p[h] + r[h][:CHUNK] for h in heads]
        x_p = [r[h][CHUNK:] for h in heads]
    p_p = [p_p[h] + _dot(p_p[h].astype(BF16), to_bd(x_p[h].astype(BF16))) for h in heads]

    rhs = [jnp.concatenate([v[h] * beta[h], k[h] * (beta[h] * egam[h])], axis=1).astype(BF16)
           for h in heads]
    sol = [_dot(to_bd(p_p[h].astype(BF16)), rhs[h]) for h in heads]
    u = [a[:, :GDN_DV] for a in sol]
    w = [a[:, GDN_DV:].astype(BF16) for a in sol]
    qd = [(q[h] * egam[h]).astype(BF16) for h in heads]
    kd_t = [jnp.transpose(k[h] * jnp.exp2(gtot[h] - gam[h])).astype(BF16) for h in heads]
    z = [z_ref[h // nh, :, (h % nh) * GDN_DV:(h % nh + 1) * GDN_DV].astype(F32) for h in heads]
    zgate = [a * _sigmoid(a) for a in z]

    state = [s_ref[h] for h in heads]
    for c in range(nch):
        rows = slice(c * CHUNK, (c + 1) * CHUNK)
        sb = [a.astype(BF16) for a in state]
        r1 = [_dot(jnp.concatenate([w[h][rows], qd[h][rows]], axis=0), sb[h]) for h in heads]
        v_new = [u[h][rows] - r1[h][:CHUNK] for h in heads]
        v_full = [jnp.where(row_chunk == c, jnp.concatenate([a.astype(BF16)] * nch, axis=0),
                            jnp.zeros((), BF16)) for a in v_new]
        kd_c = [jnp.where(lane_chunk == c, a, jnp.zeros((), BF16)) for a in kd_t]
        r2 = [_dot(jnp.concatenate([qk[h][rows], kd_c[h]], axis=0), v_full[h]) for h in heads]
        for h in heads:
            o = r1[h][CHUNK:] + r2[h][:CHUNK]
            on = o * lax.rsqrt(jnp.mean(o * o, -1, keepdims=True) + NORM_EPS) * ng
            o_ref[h // nh, rows, (h % nh) * GDN_DV:(h % nh + 1) * GDN_DV] = (
                on * zgate[h][rows]).astype(BF16)
        state = [state[h] * jnp.exp2(gtot[h][c * CHUNK:c * CHUNK + 1]) + r2[h][CHUNK:] for h in heads]
    for h in heads:
        s_ref[h] = state[h]


def _gdn(proj_big, gcol, grow, norm_g, batch, seq, cast):
    nt = seq // GDN_TB
    nb = GDN_NB
    assert batch % nb == 0
    steps = (batch // nb) * nt
    assert all(w.shape[0] % (16 * steps) == 0 for w in cast)
    cast_specs = [pl.BlockSpec((w.shape[0] // steps, w.shape[1]), lambda g, t: (g * nt + t, 0))
                  for w in cast]
    return pl.pallas_call(
        functools.partial(_gdn_kernel, ncast=len(cast)),
        grid=(batch // nb, nt),
        in_specs=[
            pl.BlockSpec((nb, GDN_TB, GDN_QKV), lambda g, t: (g, t, 0)),
            pl.BlockSpec((nb, GDN_TB, GDN_WIDTH), lambda g, t: (g, t, OFF_Z // GDN_WIDTH)),
            pl.BlockSpec((nb, GDN_TB, N_SMALL), lambda g, t: (g, t, 0)),
            pl.BlockSpec((nb, 1, GDN_HEADS, GDN_TB), lambda g, t: (g, t, 0, 0)),
            _resident((1, GDN_DV)),
        ] + cast_specs,
        out_specs=[pl.BlockSpec((nb, GDN_TB, GDN_WIDTH), lambda g, t: (g, t, 0))] + cast_specs,
        out_shape=[jax.ShapeDtypeStruct((batch, seq, GDN_WIDTH), BF16)]
        + [jax.ShapeDtypeStruct(w.shape, BF16) for w in cast],
        scratch_shapes=[
            pltpu.VMEM((nb * GDN_HEADS, GDN_DK, GDN_DV), F32),
        ],
        compiler_params=pltpu.CompilerParams(
            dimension_semantics=("arbitrary", "arbitrary"), vmem_limit_bytes=VMEM_LIMIT),
        name="gdn",
    )(proj_big, proj_big, gcol, grow, norm_g, *cast)


FOX_X = 128
FOX_QC = (0, 6)
FOX_KC = (3, 9)


def _split3(c):
    hi = c.astype(BF16)
    r1 = c - hi.astype(F32)
    mid = r1.astype(BF16)
    lo = (r1 - mid.astype(F32)).astype(BF16)
    return jnp.concatenate([hi, mid, lo], axis=1)


def _select3(pair, base):
    e = np.zeros((3 * N_SMALL, FOX_X), np.float32)
    for d in range(3):
        for head in range(2):
            e[d * N_SMALL + LANE_C + 2 * pair + head, base[head] + d] = 1.0
    return e


def _lane_ones(first):
    out = np.zeros((FOX_X,), np.float32)
    for f in first:
        out[f:f + 3] = 1.0
    return out


def _fox_constants():
    npair = FOX_HEADS // 2
    sel = np.stack([np.concatenate([_select3(p, base) for p in range(npair)], axis=1)
                    for base in (FOX_KC, FOX_QC)])
    ones = np.stack([_lane_ones(FOX_QC), _lane_ones(FOX_KC)])
    return jnp.asarray(sel, BF16), jnp.asarray(ones, F32)


def _fox_kernel(q_ref, k_ref, v_ref, gc_ref, ng_ref, sel_ref, ones_ref, o_ref, qa_ref, ka_ref, vt_ref,
                m_ref, l_ref, acc_ref, s_ref):
    nb, tq = q_ref.shape[0], q_ref.shape[1]
    seq = k_ref.shape[1]
    i = pl.program_id(1)
    npair = FOX_HEADS // 2
    pairs = range(nb * npair)
    lane = lax.broadcasted_iota(jnp.int32, (1, 2 * FOX_DH), 1)
    lo_half = lane < FOX_DH
    pcols = [slice((p % npair) * 2 * FOX_DH, (p % npair + 1) * 2 * FOX_DH) for p in pairs]

    @pl.when(i == 0)
    def _():
        ones_k = ones_ref[0:1, :]

        def fill(t, carry):
            rows = pl.ds(pl.multiple_of(t * tq, tq), tq)
            kx = [_dot(_split3(gc_ref[r, rows, :]), sel_ref[0]) for r in range(nb)]
            for p in pairs:
                r = p // npair
                xcols = slice((p % npair) * FOX_X, (p % npair + 1) * FOX_X)
                ka_ref[p, rows, 0:2 * FOX_DH] = k_ref[r, rows, pcols[p]]
                ka_ref[p, rows, 2 * FOX_DH:] = (ones_k - kx[r][:, xcols]).astype(BF16)
                vt_ref[p, t] = jnp.transpose(v_ref[r, rows, pcols[p]].astype(F32)).astype(BF16)
            return carry

        lax.fori_loop(0, seq // tq, fill, 0)

    qrows = pl.ds(pl.multiple_of(i * tq, tq), tq)
    qx_all = [_dot(_split3(gc_ref[r, qrows, :]), sel_ref[1]) for r in range(nb)]
    ones_q = ones_ref[1:2, :]
    xlane = lax.broadcasted_iota(jnp.int32, (1, FOX_X), 1)
    for p in pairs:
        qp = q_ref[p // npair, :, pcols[p]]
        qx = qx_all[p // npair][:, (p % npair) * FOX_X:(p % npair + 1) * FOX_X] + ones_q
        for half in range(2):
            rows = slice(half * tq, (half + 1) * tq)
            mine = lo_half if half == 0 else jnp.logical_not(lo_half)
            xmine = (xlane < FOX_QC[1]) if half == 0 else (xlane >= FOX_QC[1])
            qa_ref[p, rows, 0:2 * FOX_DH] = jnp.where(mine, qp, jnp.zeros((), BF16))
            qa_ref[p, rows, 2 * FOX_DH:] = jnp.where(xmine, qx, 0.0).astype(BF16)

    ki = lax.broadcasted_iota(jnp.int32, (tq, 2 * tq), 0)
    qi = jnp.bitwise_and(lax.broadcasted_iota(jnp.int32, (tq, 2 * tq), 1), tq - 1)
    causal = ki <= qi

    def scores(j):
        krows = pl.ds(pl.multiple_of(j * tq, tq), tq)
        return [_dot_nt(ka_ref[p, krows, :], qa_ref[p]) for p in pairs]

    def absorb(s, j, first):
        for p in pairs:
            sp = s[p]
            if first:
                sp = jnp.where(causal, sp, -jnp.inf)
            m_new = jnp.max(sp, 0, keepdims=True)
            if not first:
                m_prev = m_ref[p]
                m_new = jnp.maximum(m_prev, m_new)
                scale = jnp.exp2(m_prev - m_new)
            prob = jnp.exp2(sp - m_new)
            psum = jnp.sum(prob, 0, keepdims=True)
            pv = _dot(vt_ref[p, j], prob.astype(BF16))
            m_ref[p] = m_new
            if first:
                l_ref[p] = psum
                acc_ref[p] = pv
            else:
                l_ref[p] = scale * l_ref[p] + psum
                acc_ref[p] = scale * acc_ref[p] + pv

    last = jnp.maximum(i - 1, 0)

    def stage(slot, j_next):
        s_next = scores(jnp.minimum(j_next, last))
        for p in pairs:
            s_ref[slot, p] = s_next[p]

    def consume(slot, j):
        absorb([s_ref[slot, p] for p in pairs], j, False)

    s_diag = scores(i)
    stage(0, 0)
    absorb(s_diag, i, True)

    def body(t, carry):
        stage(1, 2 * t + 1)
        consume(0, 2 * t)
        stage(0, 2 * t + 2)
        consume(1, 2 * t + 1)
        return carry

    lax.fori_loop(0, i // 2, body, 0)

    @pl.when(i % 2 == 1)
    def _():
        consume(0, i - 1)

    ng = ng_ref[...]
    for p in pairs:
        acc = acc_ref[p]
        l = l_ref[p]
        outs = []
        for half in range(2):
            oh = (acc[half * FOX_DH:(half + 1) * FOX_DH, half * tq:(half + 1) * tq]
                  / l[:, half * tq:(half + 1) * tq])
            ms = jnp.mean(oh * oh, 0, keepdims=True)
            outs.append(oh * lax.rsqrt(ms + NORM_EPS))
        o_ref[p // npair, :, pcols[p]] = (
            jnp.transpose(jnp.concatenate(outs, axis=0)) * ng).astype(BF16)


def _fox(proj_big, gcol, norm_g2, batch, seq):
    nq = seq // FOX_TQ
    first = (GDN_QKV + GDN_WIDTH) // FOX_WIDTH
    npair = FOX_HEADS // 2
    nb = FOX_NB
    assert batch % nb == 0
    nu = nb * npair
    return pl.pallas_call(
        _fox_kernel,
        grid=(batch // nb, nq),
        in_specs=[
            pl.BlockSpec((nb, FOX_TQ, FOX_WIDTH), lambda g, i: (g, i, first)),
            pl.BlockSpec((nb, seq, FOX_WIDTH), lambda g, i: (g, 0, first + 1)),
            pl.BlockSpec((nb, seq, FOX_WIDTH), lambda g, i: (g, 0, first + 2)),
            pl.BlockSpec((nb, seq, N_SMALL), lambda g, i: (g, 0, 0)),
            _resident((1, 2 * FOX_DH)),
            _resident((2, 3 * N_SMALL, npair * FOX_X)),
            _resident((2, FOX_X)),
        ],
        out_specs=pl.BlockSpec((nb, FOX_TQ, FOX_WIDTH), lambda g, i: (g, i, 0)),
        out_shape=jax.ShapeDtypeStruct((batch, seq, FOX_WIDTH), BF16),
        scratch_shapes=[
            pltpu.VMEM((nu, 2 * FOX_TQ, 2 * FOX_DH + FOX_X), BF16),
            pltpu.VMEM((nu, seq, 2 * FOX_DH + FOX_X), BF16),
            pltpu.VMEM((nu, nq, 2 * FOX_DH, FOX_TQ), BF16),
            pltpu.VMEM((nu, 1, 2 * FOX_TQ), F32),
            pltpu.VMEM((nu, 1, 2 * FOX_TQ), F32),
            pltpu.VMEM((nu, 2 * FOX_DH, 2 * FOX_TQ), F32),
            pltpu.VMEM((2, nu, FOX_TQ, 2 * FOX_TQ), F32),
        ],
        compiler_params=pltpu.CompilerParams(
            dimension_semantics=("arbitrary", "arbitrary"), vmem_limit_bytes=VMEM_LIMIT),
        name="fox",
    )(proj_big, proj_big, proj_big, gcol, norm_g2, *_fox_constants())


def _tail_kernel(x_ref, p_ref, og_ref, of_ref, lin_g, lin_b, wo_ref, l1g, l1b, wu_ref, wd_ref,
                 wp_ref, wg_ref, bg_ref, l2g, l2b, o_ref):
    nparts = x_ref.shape[0] // TAIL_PART
    parts = [slice(r * TAIL_PART, (r + 1) * TAIL_PART) for r in range(nparts)]

    def head(rows):
        h = _layer_norm(x_ref[rows, :], lin_g[...], lin_b[...])
        mix = (_dot(og_ref[rows, :], wo_ref[0:GDN_WIDTH, :])
               + _dot(of_ref[rows, :], wo_ref[GDN_WIDTH:, :]))
        return _layer_norm(ALPHA * h + mix, l1g[...], l1b[...])

    def mlp(rows, h1):
        h1b = h1.astype(BF16)
        gate = _sigmoid(_dot(h1b, wg_ref[...]) + bg_ref[...])
        acc = ALPHA * h1 + _dot(p_ref[rows, :].astype(BF16), wp_ref[...]) * gate
        for j in range(D_FF // TAIL_TF):
            cols = slice(j * TAIL_TF, (j + 1) * TAIL_TF)
            a = jnp.maximum(_dot(h1b, wu_ref[:, cols]), 0.0)
            acc = acc + _dot((a * a).astype(BF16), wd_ref[cols, :])
        return acc

    h1 = head(parts[0])
    for r in range(nparts):
        h1_next = head(parts[r + 1]) if r + 1 < nparts else None
        o_ref[parts[r], :] = _layer_norm(mlp(parts[r], h1), l2g[...], l2b[...])
        h1 = h1_next


def _tail(x2, p2, o_gdn, o_fox, lin_g, lin_b, w_out, l1g, l1b, w_up, w_down, w_ple, w_gate,
          b_gate, l2g, l2b):
    n = x2.shape[0]
    row = lambda width: pl.BlockSpec((TAIL_TM, width), lambda i: (i, 0))
    vec = _resident((1, D_MODEL))
    return pl.pallas_call(
        _tail_kernel,
        grid=(n // TAIL_TM,),
        in_specs=[
            row(D_MODEL), row(D_PLE), row(GDN_WIDTH), row(FOX_WIDTH),
            vec, vec, _resident((D_MODEL, D_MODEL)), vec, vec,
            _resident((D_MODEL, D_FF)), _resident((D_FF, D_MODEL)),
            _resident((D_PLE, D_MODEL)), _resident((D_MODEL, D_MODEL)), vec, vec, vec,
        ],
        out_specs=row(D_MODEL),
        out_shape=jax.ShapeDtypeStruct((n, D_MODEL), F32),
        compiler_params=pltpu.CompilerParams(
            dimension_semantics=("arbitrary",), vmem_limit_bytes=VMEM_LIMIT),
        name="tail",
    )(x2, p2, o_gdn, o_fox, lin_g, lin_b, w_out, l1g, l1b, w_up, w_down, w_ple, w_gate,
      b_gate, l2g, l2b)


def kernel(x, p, ln_in_g, ln_in_b, w_in, conv_w, a_log, dt_bias, gdn_norm_g, b_f, fox_norm_g,
           w_out, ln1_g, ln1_b, w_up, w_down, w_ple, w_ple_gate, b_ple_gate, ln2_g, ln2_b):
    batch, seq, _ = x.shape
    assert x.shape[2] == D_MODEL and w_in.shape[0] == 1
    assert seq % FOX_TQ == 0 and seq % GDN_TB == 0 and (batch * seq) % PROJ_TM == 0
    n = batch * seq
    x2 = x.reshape(n, D_MODEL)
    p2 = p[0].reshape(n, D_PLE)
    row = lambda a: a.reshape(1, -1).astype(F32)

    n_gate = LANE_GTOT + GDN_HEADS
    zeros4 = jnp.zeros((GDN_HEADS,), F32)
    pad = jnp.zeros((N_SMALL - n_gate,), F32)
    pvec = jnp.zeros((8, N_SMALL), F32)
    pvec = pvec.at[0].set(jnp.concatenate([zeros4, dt_bias[0], b_f[0], dt_bias[0], pad]))
    pvec = pvec.at[1].set(
        jnp.concatenate([zeros4, a_log[0], jnp.zeros((FOX_HEADS,), F32), a_log[0], pad]))

    proj_big, small = _proj(x2, row(ln_in_g), row(ln_in_b), w_in[0], conv_w[0], seq)
    gcol, grow = _gates(small, pvec, batch, seq)

    proj3 = proj_big.reshape(batch, seq, N_BIG)
    gcol3 = gcol.reshape(batch, seq, N_SMALL)
    o_gdn, w_out_b, w_up_b, w_down_b, w_ple_b, w_gate_b = _gdn(
        proj3, gcol3, grow, row(gdn_norm_g[0]), batch, seq,
        (w_out[0], w_up[0], w_down[0], w_ple[0], w_ple_gate[0]))
    o_gdn = o_gdn.reshape(n, GDN_WIDTH)
    o_fox = _fox(proj3, gcol3, row(jnp.tile(fox_norm_g[0], 2)), batch, seq).reshape(n, FOX_WIDTH)

    out = _tail(x2, p2, o_gdn, o_fox, row(ln_in_g), row(ln_in_b), w_out_b,
                row(ln1_g[0]), row(ln1_b[0]), w_up_b, w_down_b, w_ple_b, w_gate_b,
                row(b_ple_gate[0]), row(ln2_g[0]), row(ln2_b[0]))
    return out.reshape(batch, seq, D_MODEL)
```

```python
import functools

import jax
import jax.numpy as jnp
import numpy as np
from jax import lax
from jax.experimental import pallas as pl
from jax.experimental.pallas import tpu as pltpu

F32 = jnp.float32
BF16 = jnp.bfloat16

D_MODEL = 1024
CHUNK = 64
CHUNK_SHIFT = CHUNK.bit_length() - 1
SUBLANES = 8
GDN_HEADS = 4
GDN_DK = 128
GDN_DV = 128
GDN_QK = GDN_HEADS * GDN_DK
GDN_QKV = 3 * GDN_QK
GDN_WIDTH = GDN_HEADS * GDN_DV
FOX_HEADS = 8
FOX_DH = 64
FOX_WIDTH = FOX_HEADS * FOX_DH
CONV_W = 4
D_FF = 4 * D_MODEL
D_PLE = 256
LN_EPS = 1e-5
NORM_EPS = 1e-6
ALPHA = 2.0 ** 0.25

OFF_Z = GDN_QKV
OFF_BETA = OFF_Z + GDN_WIDTH
OFF_FOX = OFF_BETA + 2 * GDN_HEADS
OFF_F = OFF_FOX + 3 * FOX_WIDTH
D_IN = OFF_F + FOX_HEADS
N_BIG = GDN_QKV + GDN_WIDTH + 3 * FOX_WIDTH
N_SMALL = 128
FOX_Q0 = GDN_QKV + GDN_WIDTH
LOG2E = 1.4426950408889634
FOX_QSCALE = FOX_DH ** -0.5 * LOG2E
LANE_BETA = 0
LANE_GAM = GDN_HEADS
LANE_C = 2 * GDN_HEADS
LANE_GTOT = LANE_C + FOX_HEADS + GDN_HEADS

VMEM_LIMIT = 56 * 1024 * 1024

PROJ_TM = 1024
PROJ_TN = 256
PROJ_AHEAD = 1
GATE_TB = 256
GDN_TB = 256
GDN_NB = 4
FOX_TQ = 256
FOX_NB = 2
TAIL_TM = 512
TAIL_PART = 256
TAIL_TF = 1024


def _layer_norm(x, g, b):
    mu = jnp.mean(x, -1, keepdims=True)
    xc = x - mu
    var = jnp.mean(xc * xc, -1, keepdims=True)
    return xc * lax.rsqrt(var + LN_EPS) * g + b


def _softplus(x):
    return jnp.maximum(x, 0.0) + jnp.log(1.0 + jnp.exp(-jnp.abs(x)))


def _sigmoid(x):
    return 1.0 / (1.0 + jnp.exp2(x * -LOG2E))


def _dot(a, b):
    return jnp.dot(a, b, preferred_element_type=F32)


def _dot_nt(a, b):
    return lax.dot_general(a, b, (((1,), (1,)), ((), ())), preferred_element_type=F32)


def _resident(shape):
    return pl.BlockSpec(shape, lambda *_: (0,) * len(shape), pipeline_mode=pl.Buffered(1))


def _proj_kernel(x_ref, g_ref, b_ref, w_ref, cw_ref, ob_ref, os_ref, halo_ref, wb_ref, ws_ref, *,
                 tiles_per_seq):
    tm = x_ref.shape[0]
    halo = halo_ref.shape[0]

    @pl.when(pl.program_id(0) == 0)
    def _():
        step = N_SMALL
        for c0 in range(0, OFF_BETA, step):
            wb_ref[:, c0:c0 + step] = jnp.transpose(w_ref[c0:c0 + step, :]).astype(BF16)
        for c0 in range(0, N_BIG - OFF_BETA, step):
            wb_ref[:, OFF_BETA + c0:OFF_BETA + c0 + step] = jnp.transpose(
                w_ref[OFF_FOX + c0:OFF_FOX + c0 + step, :]).astype(BF16)
        decay = w_ref[OFF_BETA:OFF_FOX, :]
        gates = [decay, w_ref[OFF_F:D_IN, :], decay,
                 jnp.zeros((N_SMALL - LANE_GTOT - GDN_HEADS, D_MODEL), F32)]
        ws_ref[...] = jnp.transpose(jnp.concatenate(gates, axis=0)).astype(BF16)

    @pl.when(pl.program_id(0) % tiles_per_seq == 0)
    def _():
        halo_ref[...] = jnp.zeros_like(halo_ref)

    h = _layer_norm(x_ref[...], g_ref[...], b_ref[...]).astype(BF16)
    nchunk = N_BIG // PROJ_TN
    nconv = GDN_QKV // PROJ_TN
    order = [c for pair in zip(range(nconv), range(nconv, 2 * nconv)) for c in pair]
    order += list(range(2 * nconv, nchunk))
    def chunk_dot(c):
        return _dot(h, wb_ref[:, c * PROJ_TN:(c + 1) * PROJ_TN])

    inflight = [chunk_dot(c) for c in order[:PROJ_AHEAD]]
    for pos, j in enumerate(order):
        cols = slice(j * PROJ_TN, (j + 1) * PROJ_TN)
        acc = inflight.pop(0)
        if pos + PROJ_AHEAD < nchunk:
            inflight.append(chunk_dot(order[pos + PROJ_AHEAD]))
        if (j + 1) * PROJ_TN <= GDN_QKV:
            ext = jnp.concatenate([halo_ref[:, cols], acc], axis=0)
            y = acc * cw_ref[CONV_W - 1:CONV_W, cols]
            for d in range(1, CONV_W):
                y = y + pltpu.roll(ext, d, 0)[halo:] * cw_ref[CONV_W - 1 - d:CONV_W - d, cols]
            halo_ref[:, cols] = acc[tm - halo:]
            acc = y * _sigmoid(y)
        elif FOX_Q0 <= j * PROJ_TN < FOX_Q0 + FOX_WIDTH:
            acc = acc * FOX_QSCALE
        ob_ref[:, cols] = acc.astype(BF16)
    os_ref[...] = _dot(h, ws_ref[...])


def _proj(x2, ln_g, ln_b, w_in, conv_w, seq):
    n = x2.shape[0]
    assert GDN_QKV % PROJ_TN == 0 and seq % PROJ_TM == 0
    assert w_in.shape == (D_IN, D_MODEL) and OFF_FOX % SUBLANES == 0 and OFF_F % SUBLANES == 0
    return pl.pallas_call(
        functools.partial(_proj_kernel, tiles_per_seq=seq // PROJ_TM),
        grid=(n // PROJ_TM,),
        in_specs=[
            pl.BlockSpec((PROJ_TM, D_MODEL), lambda i: (i, 0)),
            _resident((1, D_MODEL)),
            _resident((1, D_MODEL)),
            _resident((D_IN, D_MODEL)),
            _resident((CONV_W, GDN_QKV)),
        ],
        out_specs=[
            pl.BlockSpec((PROJ_TM, N_BIG), lambda i: (i, 0)),
            pl.BlockSpec((PROJ_TM, N_SMALL), lambda i: (i, 0)),
        ],
        out_shape=[
            jax.ShapeDtypeStruct((n, N_BIG), BF16),
            jax.ShapeDtypeStruct((n, N_SMALL), F32),
        ],
        scratch_shapes=[
            pltpu.VMEM((SUBLANES, GDN_QKV), F32),
            pltpu.VMEM((D_MODEL, N_BIG), BF16),
            pltpu.VMEM((D_MODEL, N_SMALL), BF16),
        ],
        compiler_params=pltpu.CompilerParams(
            dimension_semantics=("arbitrary",), vmem_limit_bytes=VMEM_LIMIT),
        name="proj",
    )(x2, ln_g, ln_b, w_in, conv_w)


def _gates_kernel(s_ref, pv_ref, o_ref, gr_ref, *, seq):
    lane = lax.broadcasted_iota(jnp.int32, (1, N_SMALL), 1)
    bias = pv_ref[0:1, :]
    neg_a = -jnp.exp(pv_ref[1:2, :])
    r = lax.broadcasted_iota(jnp.int32, (GATE_TB, GATE_TB), 0)
    c = lax.broadcasted_iota(jnp.int32, (GATE_TB, GATE_TB), 1)
    l_full = jnp.where(r >= c, 1.0, 0.0).astype(F32)
    same_chunk = jnp.right_shift(r, CHUNK_SHIFT) == jnp.right_shift(c, CHUNK_SHIFT)
    l_chunk = jnp.where(same_chunk, l_full, 0.0)
    l_total = jnp.where(same_chunk, 1.0, 0.0).astype(F32)
    l_decay = jnp.concatenate([l_chunk, l_total], axis=0).astype(BF16)
    l_forget = l_full.astype(BF16)

    def pieces_sum(a):
        return a[:, 0:N_SMALL] + a[:, N_SMALL:2 * N_SMALL] + a[:, 2 * N_SMALL:]

    carry = jnp.zeros((1, N_SMALL), F32)
    for t in range(seq // GATE_TB):
        rows = slice(t * GATE_TB, (t + 1) * GATE_TB)
        x = s_ref[rows, :] + bias
        beta = _sigmoid(x)
        log_g = (neg_a * LOG2E) * _softplus(x)
        log_f = -LOG2E * _softplus(-x)
        decay = pieces_sum(_dot(l_decay, _split3(log_g)))
        gam, gtot = decay[:GATE_TB], decay[GATE_TB:]
        cum = pieces_sum(_dot(l_forget, _split3(log_f))) + carry
        carry = cum[GATE_TB - 1:GATE_TB, :]
        o_ref[rows, :] = jnp.where(
            lane < LANE_GAM, beta,
            jnp.where(lane < LANE_C, gam, jnp.where(lane < LANE_GTOT, cum, gtot)))
        gr_ref[0, t] = jnp.transpose(gam)[LANE_GAM:LANE_GAM + GDN_HEADS, :]


def _gates(small, pvec, batch, seq):
    assert GATE_TB == GDN_TB
    nt = seq // GATE_TB
    return pl.pallas_call(
        functools.partial(_gates_kernel, seq=seq),
        grid=(batch,),
        in_specs=[
            pl.BlockSpec((seq, N_SMALL), lambda b: (b, 0)),
            _resident((8, N_SMALL)),
        ],
        out_specs=[
            pl.BlockSpec((seq, N_SMALL), lambda b: (b, 0)),
            pl.BlockSpec((1, nt, GDN_HEADS, GATE_TB), lambda b: (b, 0, 0, 0)),
        ],
        out_shape=[
            jax.ShapeDtypeStruct((batch * seq, N_SMALL), F32),
            jax.ShapeDtypeStruct((batch, nt, GDN_HEADS, GATE_TB), F32),
        ],
        compiler_params=pltpu.CompilerParams(
            dimension_semantics=("arbitrary",), vmem_limit_bytes=VMEM_LIMIT),
        name="gates",
    )(small, pvec)


def _gdn_kernel(x_ref, z_ref, gc_ref, gr_ref, ng_ref, *rest, ncast):
    o_ref, s_ref = rest[ncast], rest[2 * ncast + 1]
    for w_f32, w_bf16 in zip(rest[:ncast], rest[ncast + 1:2 * ncast + 1]):
        w_bf16[...] = w_f32[...].astype(BF16)

    nb, tb = x_ref.shape[0], x_ref.shape[1]
    nch = tb // CHUNK
    heads = range(nb * GDN_HEADS)
    nh = GDN_HEADS

    @pl.when(pl.program_id(1) == 0)
    def _():
        s_ref[...] = jnp.zeros_like(s_ref)

    def act(h, c0):
        c0 = c0 + (h % nh) * GDN_DK
        return x_ref[h // nh, :, c0:c0 + GDN_DK].astype(F32)

    def l2norm(v, scale):
        return v * (lax.rsqrt(jnp.sum(v * v, -1, keepdims=True) + NORM_EPS) * scale)

    ri = lax.broadcasted_iota(jnp.int32, (tb, tb), 0)
    ci = lax.broadcasted_iota(jnp.int32, (tb, tb), 1)
    same = jnp.right_shift(ri, CHUNK_SHIFT) == jnp.right_shift(ci, CHUNK_SHIFT)
    keep = jnp.logical_and(same, ri >= ci)
    diag = ri == ci
    pr = lax.broadcasted_iota(jnp.int32, (CHUNK, tb), 0)
    pc = lax.broadcasted_iota(jnp.int32, (CHUNK, tb), 1)
    eye_packed = jnp.where(jnp.bitwise_and(pc, CHUNK - 1) == pr, 1.0, 0.0).astype(F32)
    lane_chunk = jnp.right_shift(lax.broadcasted_iota(jnp.int32, (1, tb), 1), CHUNK_SHIFT)
    row_chunk = jnp.right_shift(lax.broadcasted_iota(jnp.int32, (tb, 1), 0), CHUNK_SHIFT)

    def to_bd(packed):
        return jnp.where(same, jnp.concatenate([packed] * nch, axis=0), jnp.zeros((), packed.dtype))

    def to_packed(bd):
        out = bd[0:CHUNK]
        for c in range(1, nch):
            out = out + bd[c * CHUNK:(c + 1) * CHUNK]
        return out

    ng = ng_ref[...]
    q = [l2norm(act(h, 0), GDN_DK ** -0.5) for h in heads]
    k = [l2norm(act(h, GDN_QK), 1.0) for h in heads]
    v = [act(h, 2 * GDN_QK) for h in heads]

    def gate(h, lane0):
        return gc_ref[h // nh, :, lane0 + h % nh:lane0 + h % nh + 1]

    beta = [gate(h, LANE_BETA) for h in heads]
    gam = [gate(h, LANE_GAM) for h in heads]
    gtot = [gate(h, LANE_GTOT) for h in heads]
    grow = [gr_ref[h // nh, 0, h % nh:h % nh + 1, :] for h in heads]
    egam = [jnp.exp2(g) for g in gam]
    kb = [a.astype(BF16) for a in k]
    decay = [jnp.exp2(jnp.where(keep, gam[h] - grow[h], -jnp.inf)) for h in heads]
    gram = [_dot_nt(kb[h], kb[h]) for h in heads]
    qk = [(_dot_nt(q[h].astype(BF16), kb[h]) * decay[h]).astype(BF16) for h in heads]

    x_bd = [jnp.where(diag, 0.0, -(gram[h] * beta[h]) * decay[h]) for h in heads]
    x_p = [to_packed(a) for a in x_bd]
    p_p = [eye_packed + a for a in x_p]
    x_p = [_dot(x_p[h].astype(BF16), x_bd[h].astype(BF16)) for h in heads]
    for _ in range(4):
        w_bd = [to_bd(a.astype(BF16)) for a in x_p]
        r = [_dot(jnp.concatenate([p_p[h], x_p[h]], axis=0).astype(BF16), w_bd[h]) for h in heads]
        p_p = [p_p[h] + r[h][:CHUNK] for h in heads]
        x_p = [r[h][CHUNK:] for h in heads]
    p_p = [p_p[h] + _dot(p_p[h].astype(BF16), to_bd(x_p[h].astype(BF16))) for h in heads]

    rhs = [jnp.concatenate([v[h] * beta[h], k[h] * (beta[h] * egam[h])], axis=1).astype(BF16)
           for h in heads]
    sol = [_dot(to_bd(p_p[h].astype(BF16)), rhs[h]) for h in heads]
    u = [a[:, :GDN_DV] for a in sol]
    w = [a[:, GDN_DV:].astype(BF16) for a in sol]
    qd = [(q[h] * egam[h]).astype(BF16) for h in heads]
    kd_t = [jnp.transpose(k[h] * jnp.exp2(gtot[h] - gam[h])).astype(BF16) for h in heads]
    z = [z_ref[h // nh, :, (h % nh) * GDN_DV:(h % nh + 1) * GDN_DV].astype(F32) for h in heads]
    zgate = [a * _sigmoid(a) for a in z]

    state = [s_ref[h] for h in heads]
    for c in range(nch):
        rows = slice(c * CHUNK, (c + 1) * CHUNK)
        sb = [a.astype(BF16) for a in state]
        r1 = [_dot(jnp.concatenate([w[h][rows], qd[h][rows]], axis=0), sb[h]) for h in heads]
        v_new = [u[h][rows] - r1[h][:CHUNK] for h in heads]
        v_full = [jnp.where(row_chunk == c, jnp.concatenate([a.astype(BF16)] * nch, axis=0),
                            jnp.zeros((), BF16)) for a in v_new]
        kd_c = [jnp.where(lane_chunk == c, a, jnp.zeros((), BF16)) for a in kd_t]
        r2 = [_dot(jnp.concatenate([qk[h][rows], kd_c[h]], axis=0), v_full[h]) for h in heads]
        for h in heads:
            o = r1[h][CHUNK:] + r2[h][:CHUNK]
            on = o * lax.rsqrt(jnp.mean(o * o, -1, keepdims=True) + NORM_EPS) * ng
            o_ref[h // nh, rows, (h % nh) * GDN_DV:(h % nh + 1) * GDN_DV] = (
                on * zgate[h][rows]).astype(BF16)
        state = [state[h] * jnp.exp2(gtot[h][c * CHUNK:c * CHUNK + 1]) + r2[h][CHUNK:] for h in heads]
    for h in heads:
        s_ref[h] = state[h]


def _gdn(proj_big, gcol, grow, norm_g, batch, seq, cast):
    nt = seq // GDN_TB
    nb = GDN_NB
    assert batch % nb == 0
    steps = (batch // nb) * nt
    assert all(w.shape[0] % (16 * steps) == 0 for w in cast)
    cast_specs = [pl.BlockSpec((w.shape[0] // steps, w.shape[1]), lambda g, t: (g * nt + t, 0))
                  for w in cast]
    return pl.pallas_call(
        functools.partial(_gdn_kernel, ncast=len(cast)),
        grid=(batch // nb, nt),
        in_specs=[
            pl.BlockSpec((nb, GDN_TB, GDN_QKV), lambda g, t: (g, t, 0)),
            pl.BlockSpec((nb, GDN_TB, GDN_WIDTH), lambda g, t: (g, t, OFF_Z // GDN_WIDTH)),
            pl.BlockSpec((nb, GDN_TB, N_SMALL), lambda g, t: (g, t, 0)),
            pl.BlockSpec((nb, 1, GDN_HEADS, GDN_TB), lambda g, t: (g, t, 0, 0)),
            _resident((1, GDN_DV)),
        ] + cast_specs,
        out_specs=[pl.BlockSpec((nb, GDN_TB, GDN_WIDTH), lambda g, t: (g, t, 0))] + cast_specs,
        out_shape=[jax.ShapeDtypeStruct((batch, seq, GDN_WIDTH), BF16)]
        + [jax.ShapeDtypeStruct(w.shape, BF16) for w in cast],
        scratch_shapes=[
            pltpu.VMEM((nb * GDN_HEADS, GDN_DK, GDN_DV), F32),
        ],
        compiler_params=pltpu.CompilerParams(
            dimension_semantics=("arbitrary", "arbitrary"), vmem_limit_bytes=VMEM_LIMIT),
        name="gdn",
    )(proj_big, proj_big, gcol, grow, norm_g, *cast)


FOX_X = 128
FOX_QC = (0, 6)
FOX_KC = (3, 9)


def _split3(c):
    hi = c.astype(BF16)
    r1 = c - hi.astype(F32)
    mid = r1.astype(BF16)
    lo = (r1 - mid.astype(F32)).astype(BF16)
    return jnp.concatenate([hi, mid, lo], axis=1)


def _select3(pair, base):
    e = np.zeros((3 * N_SMALL, FOX_X), np.float32)
    for d in range(3):
        for head in range(2):
            e[d * N_SMALL + LANE_C + 2 * pair + head, base[head] + d] = 1.0
    return e


def _lane_ones(first):
    out = np.zeros((FOX_X,), np.float32)
    for f in first:
        out[f:f + 3] = 1.0
    return out


def _fox_constants():
    npair = FOX_HEADS // 2
    sel = np.stack([np.concatenate([_select3(p, base) for p in range(npair)], axis=1)
                    for base in (FOX_KC, FOX_QC)])
    ones = np.stack([_lane_ones(FOX_QC), _lane_ones(FOX_KC)])
    return jnp.asarray(sel, BF16), jnp.asarray(ones, F32)


def _fox_kernel(q_ref, k_ref, v_ref, gc_ref, ng_ref, sel_ref, ones_ref, o_ref, qa_ref, ka_ref, vt_ref,
                m_ref, l_ref, acc_ref, s_ref):
    nb, tq = q_ref.shape[0], q_ref.shape[1]
    seq = k_ref.shape[1]
    i = pl.program_id(1)
    npair = FOX_HEADS // 2
    pairs = range(nb * npair)
    lane = lax.broadcasted_iota(jnp.int32, (1, 2 * FOX_DH), 1)
    lo_half = lane < FOX_DH
    pcols = [slice((p % npair) * 2 * FOX_DH, (p % npair + 1) * 2 * FOX_DH) for p in pairs]

    @pl.when(i == 0)
    def _():
        ones_k = ones_ref[0:1, :]

        def fill(t, carry):
            rows = pl.ds(pl.multiple_of(t * tq, tq), tq)
            kx = [_dot(_split3(gc_ref[r, rows, :]), sel_ref[0]) for r in range(nb)]
            for p in pairs:
                r = p // npair
                xcols = slice((p % npair) * FOX_X, (p % npair + 1) * FOX_X)
                ka_ref[p, rows, 0:2 * FOX_DH] = k_ref[r, rows, pcols[p]]
                ka_ref[p, rows, 2 * FOX_DH:] = (ones_k - kx[r][:, xcols]).astype(BF16)
                vt_ref[p, t] = jnp.transpose(v_ref[r, rows, pcols[p]].astype(F32)).astype(BF16)
            return carry

        lax.fori_loop(0, seq // tq, fill, 0)

    qrows = pl.ds(pl.multiple_of(i * tq, tq), tq)
    qx_all = [_dot(_split3(gc_ref[r, qrows, :]), sel_ref[1]) for r in range(nb)]
    ones_q = ones_ref[1:2, :]
    xlane = lax.broadcasted_iota(jnp.int32, (1, FOX_X), 1)
    for p in pairs:
        qp = q_ref[p // npair, :, pcols[p]]
        qx = qx_all[p // npair][:, (p % npair) * FOX_X:(p % npair + 1) * FOX_X] + ones_q
        for half in range(2):
            rows = slice(half * tq, (half + 1) * tq)
            mine = lo_half if half == 0 else jnp.logical_not(lo_half)
            xmine = (xlane < FOX_QC[1]) if half == 0 else (xlane >= FOX_QC[1])
            qa_ref[p, rows, 0:2 * FOX_DH] = jnp.where(mine, qp, jnp.zeros((), BF16))
            qa_ref[p, rows, 2 * FOX_DH:] = jnp.where(xmine, qx, 0.0).astype(BF16)

    ki = lax.broadcasted_iota(jnp.int32, (tq, 2 * tq), 0)
    qi = jnp.bitwise_and(lax.broadcasted_iota(jnp.int32, (tq, 2 * tq), 1), tq - 1)
    causal = ki <= qi

    def scores(j):
        krows = pl.ds(pl.multiple_of(j * tq, tq), tq)
        return [_dot_nt(ka_ref[p, krows, :], qa_ref[p]) for p in pairs]

    def absorb(s, j, first):
        for p in pairs:
            sp = s[p]
            if first:
                sp = jnp.where(causal, sp, -jnp.inf)
            m_new = jnp.max(sp, 0, keepdims=True)
            if not first:
                m_prev = m_ref[p]
                m_new = jnp.maximum(m_prev, m_new)
                scale = jnp.exp2(m_prev - m_new)
            prob = jnp.exp2(sp - m_new)
            psum = jnp.sum(prob, 0, keepdims=True)
            pv = _dot(vt_ref[p, j], prob.astype(BF16))
            m_ref[p] = m_new
            if first:
                l_ref[p] = psum
                acc_ref[p] = pv
            else:
                l_ref[p] = scale * l_ref[p] + psum
                acc_ref[p] = scale * acc_ref[p] + pv

    last = jnp.maximum(i - 1, 0)

    def stage(slot, j_next):
        s_next = scores(jnp.minimum(j_next, last))
        for p in pairs:
            s_ref[slot, p] = s_next[p]

    def consume(slot, j):
        absorb([s_ref[slot, p] for p in pairs], j, False)

    s_diag = scores(i)
    stage(0, 0)
    absorb(s_diag, i, True)

    def body(t, carry):
        stage(1, 2 * t + 1)
        consume(0, 2 * t)
        stage(0, 2 * t + 2)
        consume(1, 2 * t + 1)
        return carry

    lax.fori_loop(0, i // 2, body, 0)

    @pl.when(i % 2 == 1)
    def _():
        consume(0, i - 1)

    ng = ng_ref[...]
    for p in pairs:
        acc = acc_ref[p]
        l = l_ref[p]
        outs = []
        for half in range(2):
            oh = (acc[half * FOX_DH:(half + 1) * FOX_DH, half * tq:(half + 1) * tq]
                  / l[:, half * tq:(half + 1) * tq])
            ms = jnp.mean(oh * oh, 0, keepdims=True)
            outs.append(oh * lax.rsqrt(ms + NORM_EPS))
        o_ref[p // npair, :, pcols[p]] = (
            jnp.transpose(jnp.concatenate(outs, axis=0)) * ng).astype(BF16)


def _fox(proj_big, gcol, norm_g2, batch, seq):
    nq = seq // FOX_TQ
    first = (GDN_QKV + GDN_WIDTH) // FOX_WIDTH
    npair = FOX_HEADS // 2
    nb = FOX_NB
    assert batch % nb == 0
    nu = nb * npair
    return pl.pallas_call(
        _fox_kernel,
        grid=(batch // nb, nq),
        in_specs=[
            pl.BlockSpec((nb, FOX_TQ, FOX_WIDTH), lambda g, i: (g, i, first)),
            pl.BlockSpec((nb, seq, FOX_WIDTH), lambda g, i: (g, 0, first + 1)),
            pl.BlockSpec((nb, seq, FOX_WIDTH), lambda g, i: (g, 0, first + 2)),
            pl.BlockSpec((nb, seq, N_SMALL), lambda g, i: (g, 0, 0)),
            _resident((1, 2 * FOX_DH)),
            _resident((2, 3 * N_SMALL, npair * FOX_X)),
            _resident((2, FOX_X)),
        ],
        out_specs=pl.BlockSpec((nb, FOX_TQ, FOX_WIDTH), lambda g, i: (g, i, 0)),
        out_shape=jax.ShapeDtypeStruct((batch, seq, FOX_WIDTH), BF16),
        scratch_shapes=[
            pltpu.VMEM((nu, 2 * FOX_TQ, 2 * FOX_DH + FOX_X), BF16),
            pltpu.VMEM((nu, seq, 2 * FOX_DH + FOX_X), BF16),
            pltpu.VMEM((nu, nq, 2 * FOX_DH, FOX_TQ), BF16),
            pltpu.VMEM((nu, 1, 2 * FOX_TQ), F32),
            pltpu.VMEM((nu, 1, 2 * FOX_TQ), F32),
            pltpu.VMEM((nu, 2 * FOX_DH, 2 * FOX_TQ), F32),
            pltpu.VMEM((2, nu, FOX_TQ, 2 * FOX_TQ), F32),
        ],
        compiler_params=pltpu.CompilerParams(
            dimension_semantics=("arbitrary", "arbitrary"), vmem_limit_bytes=VMEM_LIMIT),
        name="fox",
    )(proj_big, proj_big, proj_big, gcol, norm_g2, *_fox_constants())


def _tail_kernel(x_ref, p_ref, og_ref, of_ref, lin_g, lin_b, wo_ref, l1g, l1b, wu_ref, wd_ref,
                 wp_ref, wg_ref, bg_ref, l2g, l2b, o_ref):
    nparts = x_ref.shape[0] // TAIL_PART
    parts = [slice(r * TAIL_PART, (r + 1) * TAIL_PART) for r in range(nparts)]

    def head(rows):
        h = _layer_norm(x_ref[rows, :], lin_g[...], lin_b[...])
        mix = (_dot(og_ref[rows, :], wo_ref[0:GDN_WIDTH, :])
               + _dot(of_ref[rows, :], wo_ref[GDN_WIDTH:, :]))
        return _layer_norm(ALPHA * h + mix, l1g[...], l1b[...])

    def mlp(rows, h1):
        h1b = h1.astype(BF16)
        gate = _sigmoid(_dot(h1b, wg_ref[...]) + bg_ref[...])
        acc = ALPHA * h1 + _dot(p_ref[rows, :].astype(BF16), wp_ref[...]) * gate
        for j in range(D_FF // TAIL_TF):
            cols = slice(j * TAIL_TF, (j + 1) * TAIL_TF)
            a = jnp.maximum(_dot(h1b, wu_ref[:, cols]), 0.0)
            acc = acc + _dot((a * a).astype(BF16), wd_ref[cols, :])
        return acc

    h1 = head(parts[0])
    for r in range(nparts):
        h1_next = head(parts[r + 1]) if r + 1 < nparts else None
        o_ref[parts[r], :] = _layer_norm(mlp(parts[r], h1), l2g[...], l2b[...])
        h1 = h1_next


def _tail(x2, p2, o_gdn, o_fox, lin_g, lin_b, w_out, l1g, l1b, w_up, w_down, w_ple, w_gate,
          b_gate, l2g, l2b):
    n = x2.shape[0]
    row = lambda width: pl.BlockSpec((TAIL_TM, width), lambda i: (i, 0))
    vec = _resident((1, D_MODEL))
    return pl.pallas_call(
        _tail_kernel,
        grid=(n // TAIL_TM,),
        in_specs=[
            row(D_MODEL), row(D_PLE), row(GDN_WIDTH), row(FOX_WIDTH),
            vec, vec, _resident((D_MODEL, D_MODEL)), vec, vec,
            _resident((D_MODEL, D_FF)), _resident((D_FF, D_MODEL)),
            _resident((D_PLE, D_MODEL)), _resident((D_MODEL, D_MODEL)), vec, vec, vec,
        ],
        out_specs=row(D_MODEL),
        out_shape=jax.ShapeDtypeStruct((n, D_MODEL), F32),
        compiler_params=pltpu.CompilerParams(
            dimension_semantics=("arbitrary",), vmem_limit_bytes=VMEM_LIMIT),
        name="tail",
    )(x2, p2, o_gdn, o_fox, lin_g, lin_b, w_out, l1g, l1b, w_up, w_down, w_ple, w_gate,
      b_gate, l2g, l2b)


def kernel(x, p, ln_in_g, ln_in_b, w_in, conv_w, a_log, dt_bias, gdn_norm_g, b_f, fox_norm_g,
           w_out, ln1_g, ln1_b, w_up, w_down, w_ple, w_ple_gate, b_ple_gate, ln2_g, ln2_b):
    batch, seq, _ = x.shape
    assert x.shape[2] == D_MODEL and w_in.shape[0] == 1
    assert seq % FOX_TQ == 0 and seq % GDN_TB == 0 and (batch * seq) % PROJ_TM == 0
    n = batch * seq
    x2 = x.reshape(n, D_MODEL)
    p2 = p[0].reshape(n, D_PLE)
    row = lambda a: a.reshape(1, -1).astype(F32)

    n_gate = LANE_GTOT + GDN_HEADS
    zeros4 = jnp.zeros((GDN_HEADS,), F32)
    pad = jnp.zeros((N_SMALL - n_gate,), F32)
    pvec = jnp.zeros((8, N_SMALL), F32)
    pvec = pvec.at[0].set(jnp.concatenate([zeros4, dt_bias[0], b_f[0], zeros4, dt_bias[0], pad]))
    pvec = pvec.at[1].set(
        jnp.concatenate([zeros4, a_log[0], jnp.zeros((FOX_HEADS,), F32), zeros4, a_log[0], pad]))

    proj_big, small = _proj(x2, row(ln_in_g), row(ln_in_b), jnp.swapaxes(w_in[0], 0, 1), conv_w[0],
                            seq)
    gcol, grow = _gates(small, pvec, batch, seq)

    proj3 = proj_big.reshape(batch, seq, N_BIG)
    gcol3 = gcol.reshape(batch, seq, N_SMALL)
    o_gdn, w_out_b, w_up_b, w_down_b, w_ple_b, w_gate_b = _gdn(
        proj3, gcol3, grow, row(gdn_norm_g[0]), batch, seq,
        (w_out[0], w_up[0], w_down[0], w_ple[0], w_ple_gate[0]))
    o_gdn = o_gdn.reshape(n, GDN_WIDTH)
    o_fox = _fox(proj3, gcol3, row(jnp.tile(fox_norm_g[0], 2)), batch, seq).reshape(n, FOX_WIDTH)

    out = _tail(x2, p2, o_gdn, o_fox, row(ln_in_g), row(ln_in_b), w_out_b,
                row(ln1_g[0]), row(ln1_b[0]), w_up_b, w_down_b, w_ple_b, w_gate_b,
                row(b_ple_gate[0]), row(ln2_g[0]), row(ln2_b[0]))
    return out.reshape(batch, seq, D_MODEL)
```

```python
import functools

import jax
import jax.numpy as jnp
import numpy as np
from jax import lax
from jax.experimental import pallas as pl
from jax.experimental.pallas import tpu as pltpu

F32 = jnp.float32
BF16 = jnp.bfloat16

D_MODEL = 1024
CHUNK = 64
CHUNK_SHIFT = CHUNK.bit_length() - 1
SUBLANES = 8
GDN_HEADS = 4
GDN_DK = 128
GDN_DV = 128
GDN_QK = GDN_HEADS * GDN_DK
GDN_QKV = 3 * GDN_QK
GDN_WIDTH = GDN_HEADS * GDN_DV
FOX_HEADS = 8
FOX_DH = 64
FOX_WIDTH = FOX_HEADS * FOX_DH
CONV_W = 4
D_FF = 4 * D_MODEL
D_PLE = 256
LN_EPS = 1e-5
NORM_EPS = 1e-6
ALPHA = 2.0 ** 0.25

OFF_Z = GDN_QKV
OFF_BETA = OFF_Z + GDN_WIDTH
OFF_FOX = OFF_BETA + 2 * GDN_HEADS
OFF_F = OFF_FOX + 3 * FOX_WIDTH
D_IN = OFF_F + FOX_HEADS
N_BIG = GDN_QKV + GDN_WIDTH + 3 * FOX_WIDTH
N_SMALL = 128
FOX_Q0 = GDN_QKV + GDN_WIDTH
LOG2E = 1.4426950408889634
FOX_QSCALE = FOX_DH ** -0.5 * LOG2E
LANE_BETA = 0
LANE_GAM = GDN_HEADS
LANE_C = 2 * GDN_HEADS
LANE_GTOT = LANE_C + FOX_HEADS + GDN_HEADS

VMEM_LIMIT = 56 * 1024 * 1024

PROJ_TM = 1024
PROJ_TN = 256
PROJ_AHEAD = 1
GATE_TB = 256
GDN_TB = 256
GDN_BASE_SHIFT = 3
GDN_NB = 4
FOX_TQ = 256
FOX_NB = 2
TAIL_TM = 512
TAIL_PART = 256
TAIL_TF = 1024


def _layer_norm(x, g, b):
    mu = jnp.mean(x, -1, keepdims=True)
    xc = x - mu
    var = jnp.mean(xc * xc, -1, keepdims=True)
    return xc * lax.rsqrt(var + LN_EPS) * g + b


def _softplus(x):
    return jnp.maximum(x, 0.0) + jnp.log(1.0 + jnp.exp(-jnp.abs(x)))


def _sigmoid(x):
    return 1.0 / (1.0 + jnp.exp2(x * -LOG2E))


def _dot(a, b):
    return jnp.dot(a, b, preferred_element_type=F32)


def _dot_nt(a, b):
    return lax.dot_general(a, b, (((1,), (1,)), ((), ())), preferred_element_type=F32)


def _resident(shape):
    return pl.BlockSpec(shape, lambda *_: (0,) * len(shape), pipeline_mode=pl.Buffered(1))


def _proj_kernel(x_ref, g_ref, b_ref, w_ref, cw_ref, ob_ref, os_ref, halo_ref, wb_ref, ws_ref, *,
                 tiles_per_seq):
    tm = x_ref.shape[0]
    halo = halo_ref.shape[0]

    @pl.when(pl.program_id(0) == 0)
    def _():
        step = N_SMALL
        for c0 in range(0, OFF_BETA, step):
            wb_ref[:, c0:c0 + step] = jnp.transpose(w_ref[c0:c0 + step, :]).astype(BF16)
        for c0 in range(0, N_BIG - OFF_BETA, step):
            wb_ref[:, OFF_BETA + c0:OFF_BETA + c0 + step] = jnp.transpose(
                w_ref[OFF_FOX + c0:OFF_FOX + c0 + step, :]).astype(BF16)
        decay = w_ref[OFF_BETA:OFF_FOX, :]
        gates = [decay, w_ref[OFF_F:D_IN, :], decay,
                 jnp.zeros((N_SMALL - LANE_GTOT - GDN_HEADS, D_MODEL), F32)]
        ws_ref[...] = jnp.transpose(jnp.concatenate(gates, axis=0)).astype(BF16)

    @pl.when(pl.program_id(0) % tiles_per_seq == 0)
    def _():
        halo_ref[...] = jnp.zeros_like(halo_ref)

    h = _layer_norm(x_ref[...], g_ref[...], b_ref[...]).astype(BF16)
    nchunk = N_BIG // PROJ_TN
    nconv = GDN_QKV // PROJ_TN
    order = [c for pair in zip(range(nconv), range(nconv, 2 * nconv)) for c in pair]
    order += list(range(2 * nconv, nchunk))
    def chunk_dot(c):
        return _dot(h, wb_ref[:, c * PROJ_TN:(c + 1) * PROJ_TN])

    inflight = [chunk_dot(c) for c in order[:PROJ_AHEAD]]
    for pos, j in enumerate(order):
        cols = slice(j * PROJ_TN, (j + 1) * PROJ_TN)
        acc = inflight.pop(0)
        if pos + PROJ_AHEAD < nchunk:
            inflight.append(chunk_dot(order[pos + PROJ_AHEAD]))
        if (j + 1) * PROJ_TN <= GDN_QKV:
            ext = jnp.concatenate([halo_ref[:, cols], acc], axis=0)
            y = acc * cw_ref[CONV_W - 1:CONV_W, cols]
            for d in range(1, CONV_W):
                y = y + pltpu.roll(ext, d, 0)[halo:] * cw_ref[CONV_W - 1 - d:CONV_W - d, cols]
            halo_ref[:, cols] = acc[tm - halo:]
            acc = y * _sigmoid(y)
        elif FOX_Q0 <= j * PROJ_TN < FOX_Q0 + FOX_WIDTH:
            acc = acc * FOX_QSCALE
        ob_ref[:, cols] = acc.astype(BF16)
    os_ref[...] = _dot(h, ws_ref[...])


def _proj(x2, ln_g, ln_b, w_in, conv_w, seq):
    n = x2.shape[0]
    assert GDN_QKV % PROJ_TN == 0 and seq % PROJ_TM == 0
    assert w_in.shape == (D_IN, D_MODEL) and OFF_FOX % SUBLANES == 0 and OFF_F % SUBLANES == 0
    return pl.pallas_call(
        functools.partial(_proj_kernel, tiles_per_seq=seq // PROJ_TM),
        grid=(n // PROJ_TM,),
        in_specs=[
            pl.BlockSpec((PROJ_TM, D_MODEL), lambda i: (i, 0)),
            _resident((1, D_MODEL)),
            _resident((1, D_MODEL)),
            _resident((D_IN, D_MODEL)),
            _resident((CONV_W, GDN_QKV)),
        ],
        out_specs=[
            pl.BlockSpec((PROJ_TM, N_BIG), lambda i: (i, 0)),
            pl.BlockSpec((PROJ_TM, N_SMALL), lambda i: (i, 0)),
        ],
        out_shape=[
            jax.ShapeDtypeStruct((n, N_BIG), BF16),
            jax.ShapeDtypeStruct((n, N_SMALL), F32),
        ],
        scratch_shapes=[
            pltpu.VMEM((SUBLANES, GDN_QKV), F32),
            pltpu.VMEM((D_MODEL, N_BIG), BF16),
            pltpu.VMEM((D_MODEL, N_SMALL), BF16),
        ],
        compiler_params=pltpu.CompilerParams(
            dimension_semantics=("arbitrary",), vmem_limit_bytes=VMEM_LIMIT),
        name="proj",
    )(x2, ln_g, ln_b, w_in, conv_w)


def _gates_kernel(s_ref, pv_ref, o_ref, gr_ref, *, seq):
    lane = lax.broadcasted_iota(jnp.int32, (1, N_SMALL), 1)
    bias = pv_ref[0:1, :]
    neg_a = -jnp.exp(pv_ref[1:2, :])
    r = lax.broadcasted_iota(jnp.int32, (GATE_TB, GATE_TB), 0)
    c = lax.broadcasted_iota(jnp.int32, (GATE_TB, GATE_TB), 1)
    l_full = jnp.where(r >= c, 1.0, 0.0).astype(F32)
    same_chunk = jnp.right_shift(r, CHUNK_SHIFT) == jnp.right_shift(c, CHUNK_SHIFT)
    l_chunk = jnp.where(same_chunk, l_full, 0.0)
    l_total = jnp.where(same_chunk, 1.0, 0.0).astype(F32)
    l_decay = jnp.concatenate([l_chunk, l_total], axis=0).astype(BF16)
    l_forget = l_full.astype(BF16)

    def pieces_sum(a):
        return a[:, 0:N_SMALL] + a[:, N_SMALL:2 * N_SMALL] + a[:, 2 * N_SMALL:]

    carry = jnp.zeros((1, N_SMALL), F32)
    for t in range(seq // GATE_TB):
        rows = slice(t * GATE_TB, (t + 1) * GATE_TB)
        x = s_ref[rows, :] + bias
        beta = _sigmoid(x)
        log_g = (neg_a * LOG2E) * _softplus(x)
        log_f = -LOG2E * _softplus(-x)
        decay = pieces_sum(_dot(l_decay, _split3(log_g)))
        gam, gtot = decay[:GATE_TB], decay[GATE_TB:]
        cum = pieces_sum(_dot(l_forget, _split3(log_f))) + carry
        carry = cum[GATE_TB - 1:GATE_TB, :]
        o_ref[rows, :] = jnp.where(
            lane < LANE_GAM, beta,
            jnp.where(lane < LANE_C, gam, jnp.where(lane < LANE_GTOT, cum, gtot)))
        gr_ref[0, t] = jnp.transpose(gam)[LANE_GAM:LANE_GAM + GDN_HEADS, :]


def _gates(small, pvec, batch, seq):
    assert GATE_TB == GDN_TB
    nt = seq // GATE_TB
    return pl.pallas_call(
        functools.partial(_gates_kernel, seq=seq),
        grid=(batch,),
        in_specs=[
            pl.BlockSpec((seq, N_SMALL), lambda b: (b, 0)),
            _resident((8, N_SMALL)),
        ],
        out_specs=[
            pl.BlockSpec((seq, N_SMALL), lambda b: (b, 0)),
            pl.BlockSpec((1, nt, GDN_HEADS, GATE_TB), lambda b: (b, 0, 0, 0)),
        ],
        out_shape=[
            jax.ShapeDtypeStruct((batch * seq, N_SMALL), F32),
            jax.ShapeDtypeStruct((batch, nt, GDN_HEADS, GATE_TB), F32),
        ],
        compiler_params=pltpu.CompilerParams(
            dimension_semantics=("arbitrary",), vmem_limit_bytes=VMEM_LIMIT),
        name="gates",
    )(small, pvec)


def _gdn_kernel(x_ref, z_ref, gc_ref, gr_ref, ng_ref, *rest, ncast):
    o_ref, s_ref = rest[ncast], rest[2 * ncast + 1]
    for w_f32, w_bf16 in zip(rest[:ncast], rest[ncast + 1:2 * ncast + 1]):
        w_bf16[...] = w_f32[...].astype(BF16)

    nb, tb = x_ref.shape[0], x_ref.shape[1]
    nch = tb // CHUNK
    heads = range(nb * GDN_HEADS)
    nh = GDN_HEADS

    @pl.when(pl.program_id(1) == 0)
    def _():
        s_ref[...] = jnp.zeros_like(s_ref)

    def act(h, c0):
        c0 = c0 + (h % nh) * GDN_DK
        return x_ref[h // nh, :, c0:c0 + GDN_DK].astype(F32)

    def l2norm(v, scale):
        return v * (lax.rsqrt(jnp.sum(v * v, -1, keepdims=True) + NORM_EPS) * scale)

    ri = lax.broadcasted_iota(jnp.int32, (tb, tb), 0)
    ci = lax.broadcasted_iota(jnp.int32, (tb, tb), 1)
    same = jnp.right_shift(ri, CHUNK_SHIFT) == jnp.right_shift(ci, CHUNK_SHIFT)
    keep = jnp.logical_and(same, ri >= ci)
    diag = ri == ci
    pr = lax.broadcasted_iota(jnp.int32, (CHUNK, tb), 0)
    pc = lax.broadcasted_iota(jnp.int32, (CHUNK, tb), 1)
    eye_packed = jnp.where(jnp.bitwise_and(pc, CHUNK - 1) == pr, 1.0, 0.0).astype(F32)
    lane_chunk = jnp.right_shift(lax.broadcasted_iota(jnp.int32, (1, tb), 1), CHUNK_SHIFT)
    row_chunk = jnp.right_shift(lax.broadcasted_iota(jnp.int32, (tb, 1), 0), CHUNK_SHIFT)

    def to_bd(packed):
        return jnp.where(same, jnp.concatenate([packed] * nch, axis=0), jnp.zeros((), packed.dtype))

    def to_packed(bd):
        out = bd[0:CHUNK]
        for c in range(1, nch):
            out = out + bd[c * CHUNK:(c + 1) * CHUNK]
        return out

    ng = ng_ref[...]
    q = [l2norm(act(h, 0), GDN_DK ** -0.5) for h in heads]
    k = [l2norm(act(h, GDN_QK), 1.0) for h in heads]
    v = [act(h, 2 * GDN_QK) for h in heads]

    def gate(h, lane0):
        return gc_ref[h // nh, :, lane0 + h % nh:lane0 + h % nh + 1]

    beta = [gate(h, LANE_BETA) for h in heads]
    gam = [gate(h, LANE_GAM) for h in heads]
    gtot = [gate(h, LANE_GTOT) for h in heads]
    grow = [gr_ref[h // nh, 0, h % nh:h % nh + 1, :] for h in heads]
    egam = [jnp.exp2(g) for g in gam]
    kb = [a.astype(BF16) for a in k]
    decay = [jnp.exp2(jnp.where(keep, gam[h] - grow[h], -jnp.inf)) for h in heads]
    gram = [_dot_nt(kb[h], kb[h]) for h in heads]
    qk = [(_dot_nt(q[h].astype(BF16), kb[h]) * decay[h]).astype(BF16) for h in heads]

    pj = jnp.bitwise_and(pc, CHUNK - 1)

    def in_block(shift):
        return jnp.right_shift(pr, shift) == jnp.right_shift(pj, shift)

    levels = [in_block(s) for s in range(GDN_BASE_SHIFT, CHUNK_SHIFT)]
    a_p = [to_packed(jnp.where(diag, 0.0, (gram[h] * beta[h]) * decay[h])) for h in heads]
    x_p = [jnp.where(levels[0], -a, 0.0) for a in a_p]
    p_p = [eye_packed + a for a in x_p]
    x_p = [_dot(x_p[h].astype(BF16), to_bd(x_p[h].astype(BF16))) for h in heads]
    r = [_dot(jnp.concatenate([p_p[h], x_p[h]], axis=0).astype(BF16), to_bd(x_p[h].astype(BF16)))
         for h in heads]
    p_p = [p_p[h] + r[h][:CHUNK] for h in heads]
    p_p = [p_p[h] + _dot(p_p[h].astype(BF16), to_bd(r[h][CHUNK:].astype(BF16))) for h in heads]
    for lv, inner in enumerate(levels):
        if lv + 1 < len(levels):
            between = [jnp.where(levels[lv + 1], jnp.where(inner, 0.0, a), 0.0) for a in a_p]
        else:
            between = [jnp.where(inner, 0.0, a) for a in a_p]
        y = [_dot(between[h].astype(BF16), to_bd(p_p[h].astype(BF16))) for h in heads]
        p_p = [p_p[h] - _dot(p_p[h].astype(BF16), to_bd(y[h].astype(BF16))) for h in heads]

    rhs = [jnp.concatenate([v[h] * beta[h], k[h] * (beta[h] * egam[h])], axis=1).astype(BF16)
           for h in heads]
    sol = [_dot(to_bd(p_p[h].astype(BF16)), rhs[h]) for h in heads]
    u = [a[:, :GDN_DV] for a in sol]
    w = [a[:, GDN_DV:].astype(BF16) for a in sol]
    qd = [(q[h] * egam[h]).astype(BF16) for h in heads]
    kd_t = [jnp.transpose(k[h] * jnp.exp2(gtot[h] - gam[h])).astype(BF16) for h in heads]
    z = [z_ref[h // nh, :, (h % nh) * GDN_DV:(h % nh + 1) * GDN_DV].astype(F32) for h in heads]
    zgate = [a * _sigmoid(a) for a in z]

    state = [s_ref[h] for h in heads]
    for c in range(nch):
        rows = slice(c * CHUNK, (c + 1) * CHUNK)
        sb = [a.astype(BF16) for a in state]
        r1 = [_dot(jnp.concatenate([w[h][rows], qd[h][rows]], axis=0), sb[h]) for h in heads]
        v_new = [u[h][rows] - r1[h][:CHUNK] for h in heads]
        v_full = [jnp.where(row_chunk == c, jnp.concatenate([a.astype(BF16)] * nch, axis=0),
                            jnp.zeros((), BF16)) for a in v_new]
        kd_c = [jnp.where(lane_chunk == c, a, jnp.zeros((), BF16)) for a in kd_t]
        r2 = [_dot(jnp.concatenate([qk[h][rows], kd_c[h]], axis=0), v_full[h]) for h in heads]
        for h in heads:
            o = r1[h][CHUNK:] + r2[h][:CHUNK]
            on = o * lax.rsqrt(jnp.mean(o * o, -1, keepdims=True) + NORM_EPS) * ng
            o_ref[h // nh, rows, (h % nh) * GDN_DV:(h % nh + 1) * GDN_DV] = (
                on * zgate[h][rows]).astype(BF16)
        state = [state[h] * jnp.exp2(gtot[h][c * CHUNK:c * CHUNK + 1]) + r2[h][CHUNK:] for h in heads]
    for h in heads:
        s_ref[h] = state[h]


def _gdn(proj_big, gcol, grow, norm_g, batch, seq, cast):
    nt = seq // GDN_TB
    nb = GDN_NB
    assert batch % nb == 0
    steps = (batch // nb) * nt
    assert all(w.shape[0] % (16 * steps) == 0 for w in cast)
    cast_specs = [pl.BlockSpec((w.shape[0] // steps, w.shape[1]), lambda g, t: (g * nt + t, 0))
                  for w in cast]
    return pl.pallas_call(
        functools.partial(_gdn_kernel, ncast=len(cast)),
        grid=(batch // nb, nt),
        in_specs=[
            pl.BlockSpec((nb, GDN_TB, GDN_QKV), lambda g, t: (g, t, 0)),
            pl.BlockSpec((nb, GDN_TB, GDN_WIDTH), lambda g, t: (g, t, OFF_Z // GDN_WIDTH)),
            pl.BlockSpec((nb, GDN_TB, N_SMALL), lambda g, t: (g, t, 0)),
            pl.BlockSpec((nb, 1, GDN_HEADS, GDN_TB), lambda g, t: (g, t, 0, 0)),
            _resident((1, GDN_DV)),
        ] + cast_specs,
        out_specs=[pl.BlockSpec((nb, GDN_TB, GDN_WIDTH), lambda g, t: (g, t, 0))] + cast_specs,
        out_shape=[jax.ShapeDtypeStruct((batch, seq, GDN_WIDTH), BF16)]
        + [jax.ShapeDtypeStruct(w.shape, BF16) for w in cast],
        scratch_shapes=[
            pltpu.VMEM((nb * GDN_HEADS, GDN_DK, GDN_DV), F32),
        ],
        compiler_params=pltpu.CompilerParams(
            dimension_semantics=("arbitrary", "arbitrary"), vmem_limit_bytes=VMEM_LIMIT),
        name="gdn",
    )(proj_big, proj_big, gcol, grow, norm_g, *cast)


FOX_X = 128
FOX_QC = (0, 6)
FOX_KC = (3, 9)


def _split3(c):
    hi = c.astype(BF16)
    r1 = c - hi.astype(F32)
    mid = r1.astype(BF16)
    lo = (r1 - mid.astype(F32)).astype(BF16)
    return jnp.concatenate([hi, mid, lo], axis=1)


def _select3(pair, base):
    e = np.zeros((3 * N_SMALL, FOX_X), np.float32)
    for d in range(3):
        for head in range(2):
            e[d * N_SMALL + LANE_C + 2 * pair + head, base[head] + d] = 1.0
    return e


def _lane_ones(first):
    out = np.zeros((FOX_X,), np.float32)
    for f in first:
        out[f:f + 3] = 1.0
    return out


def _fox_constants():
    npair = FOX_HEADS // 2
    sel = np.stack([np.concatenate([_select3(p, base) for p in range(npair)], axis=1)
                    for base in (FOX_KC, FOX_QC)])
    ones = np.stack([_lane_ones(FOX_QC), _lane_ones(FOX_KC)])
    return jnp.asarray(sel, BF16), jnp.asarray(ones, F32)


def _fox_kernel(q_ref, k_ref, v_ref, gc_ref, ng_ref, sel_ref, ones_ref, o_ref, qa_ref, ka_ref, vt_ref,
                m_ref, l_ref, acc_ref, s_ref):
    nb, tq = q_ref.shape[0], q_ref.shape[1]
    seq = k_ref.shape[1]
    i = pl.program_id(1)
    npair = FOX_HEADS // 2
    pairs = range(nb * npair)
    lane = lax.broadcasted_iota(jnp.int32, (1, 2 * FOX_DH), 1)
    lo_half = lane < FOX_DH
    pcols = [slice((p % npair) * 2 * FOX_DH, (p % npair + 1) * 2 * FOX_DH) for p in pairs]

    @pl.when(i == 0)
    def _():
        ones_k = ones_ref[0:1, :]

        def fill(t, carry):
            rows = pl.ds(pl.multiple_of(t * tq, tq), tq)
            kx = [_dot(_split3(gc_ref[r, rows, :]), sel_ref[0]) for r in range(nb)]
            for p in pairs:
                r = p // npair
                xcols = slice((p % npair) * FOX_X, (p % npair + 1) * FOX_X)
                ka_ref[p, rows, 0:2 * FOX_DH] = k_ref[r, rows, pcols[p]]
                ka_ref[p, rows, 2 * FOX_DH:] = (ones_k - kx[r][:, xcols]).astype(BF16)
                vt_ref[p, t] = jnp.transpose(v_ref[r, rows, pcols[p]].astype(F32)).astype(BF16)
            return carry

        lax.fori_loop(0, seq // tq, fill, 0)

    qrows = pl.ds(pl.multiple_of(i * tq, tq), tq)
    qx_all = [_dot(_split3(gc_ref[r, qrows, :]), sel_ref[1]) for r in range(nb)]
    ones_q = ones_ref[1:2, :]
    xlane = lax.broadcasted_iota(jnp.int32, (1, FOX_X), 1)
    for p in pairs:
        qp = q_ref[p // npair, :, pcols[p]]
        qx = qx_all[p // npair][:, (p % npair) * FOX_X:(p % npair + 1) * FOX_X] + ones_q
        for half in range(2):
            rows = slice(half * tq, (half + 1) * tq)
            mine = lo_half if half == 0 else jnp.logical_not(lo_half)
            xmine = (xlane < FOX_QC[1]) if half == 0 else (xlane >= FOX_QC[1])
            qa_ref[p, rows, 0:2 * FOX_DH] = jnp.where(mine, qp, jnp.zeros((), BF16))
            qa_ref[p, rows, 2 * FOX_DH:] = jnp.where(xmine, qx, 0.0).astype(BF16)

    ki = lax.broadcasted_iota(jnp.int32, (tq, 2 * tq), 0)
    qi = jnp.bitwise_and(lax.broadcasted_iota(jnp.int32, (tq, 2 * tq), 1), tq - 1)
    causal = ki <= qi

    def scores(j):
        krows = pl.ds(pl.multiple_of(j * tq, tq), tq)
        return [_dot_nt(ka_ref[p, krows, :], qa_ref[p]) for p in pairs]

    def absorb(s, j, first):
        for p in pairs:
            sp = s[p]
            if first:
                sp = jnp.where(causal, sp, -jnp.inf)
            m_new = jnp.max(sp, 0, keepdims=True)
            if not first:
                m_prev = m_ref[p]
                m_new = jnp.maximum(m_prev, m_new)
                scale = jnp.exp2(m_prev - m_new)
            prob = jnp.exp2(sp - m_new)
            psum = jnp.sum(prob, 0, keepdims=True)
            pv = _dot(vt_ref[p, j], prob.astype(BF16))
            m_ref[p] = m_new
            if first:
                l_ref[p] = psum
                acc_ref[p] = pv
            else:
                l_ref[p] = scale * l_ref[p] + psum
                acc_ref[p] = scale * acc_ref[p] + pv

    last = jnp.maximum(i - 1, 0)

    def stage(slot, j_next):
        s_next = scores(jnp.minimum(j_next, last))
        for p in pairs:
            s_ref[slot, p] = s_next[p]

    def consume(slot, j):
        absorb([s_ref[slot, p] for p in pairs], j, False)

    s_diag = scores(i)
    stage(0, 0)
    absorb(s_diag, i, True)

    def body(t, carry):
        stage(1, 2 * t + 1)
        consume(0, 2 * t)
        stage(0, 2 * t + 2)
        consume(1, 2 * t + 1)
        return carry

    lax.fori_loop(0, i // 2, body, 0)

    @pl.when(i % 2 == 1)
    def _():
        consume(0, i - 1)

    ng = ng_ref[...]
    for p in pairs:
        acc = acc_ref[p]
        l = l_ref[p]
        outs = []
        for half in range(2):
            oh = (acc[half * FOX_DH:(half + 1) * FOX_DH, half * tq:(half + 1) * tq]
                  / l[:, half * tq:(half + 1) * tq])
            ms = jnp.mean(oh * oh, 0, keepdims=True)
            outs.append(oh * lax.rsqrt(ms + NORM_EPS))
        o_ref[p // npair, :, pcols[p]] = (
            jnp.transpose(jnp.concatenate(outs, axis=0)) * ng).astype(BF16)


def _fox(proj_big, gcol, norm_g2, batch, seq):
    nq = seq // FOX_TQ
    first = (GDN_QKV + GDN_WIDTH) // FOX_WIDTH
    npair = FOX_HEADS // 2
    nb = FOX_NB
    assert batch % nb == 0
    nu = nb * npair
    return pl.pallas_call(
        _fox_kernel,
        grid=(batch // nb, nq),
        in_specs=[
            pl.BlockSpec((nb, FOX_TQ, FOX_WIDTH), lambda g, i: (g, i, first)),
            pl.BlockSpec((nb, seq, FOX_WIDTH), lambda g, i: (g, 0, first + 1)),
            pl.BlockSpec((nb, seq, FOX_WIDTH), lambda g, i: (g, 0, first + 2)),
            pl.BlockSpec((nb, seq, N_SMALL), lambda g, i: (g, 0, 0)),
            _resident((1, 2 * FOX_DH)),
            _resident((2, 3 * N_SMALL, npair * FOX_X)),
            _resident((2, FOX_X)),
        ],
        out_specs=pl.BlockSpec((nb, FOX_TQ, FOX_WIDTH), lambda g, i: (g, i, 0)),
        out_shape=jax.ShapeDtypeStruct((batch, seq, FOX_WIDTH), BF16),
        scratch_shapes=[
            pltpu.VMEM((nu, 2 * FOX_TQ, 2 * FOX_DH + FOX_X), BF16),
            pltpu.VMEM((nu, seq, 2 * FOX_DH + FOX_X), BF16),
            pltpu.VMEM((nu, nq, 2 * FOX_DH, FOX_TQ), BF16),
            pltpu.VMEM((nu, 1, 2 * FOX_TQ), F32),
            pltpu.VMEM((nu, 1, 2 * FOX_TQ), F32),
            pltpu.VMEM((nu, 2 * FOX_DH, 2 * FOX_TQ), F32),
            pltpu.VMEM((2, nu, FOX_TQ, 2 * FOX_TQ), F32),
        ],
        compiler_params=pltpu.CompilerParams(
            dimension_semantics=("arbitrary", "arbitrary"), vmem_limit_bytes=VMEM_LIMIT),
        name="fox",
    )(proj_big, proj_big, proj_big, gcol, norm_g2, *_fox_constants())


def _tail_kernel(x_ref, p_ref, og_ref, of_ref, lin_g, lin_b, wo_ref, l1g, l1b, wu_ref, wd_ref,
                 wp_ref, wg_ref, bg_ref, l2g, l2b, o_ref):
    nparts = x_ref.shape[0] // TAIL_PART
    parts = [slice(r * TAIL_PART, (r + 1) * TAIL_PART) for r in range(nparts)]

    def head(rows):
        h = _layer_norm(x_ref[rows, :], lin_g[...], lin_b[...])
        mix = (_dot(og_ref[rows, :], wo_ref[0:GDN_WIDTH, :])
               + _dot(of_ref[rows, :], wo_ref[GDN_WIDTH:, :]))
        return _layer_norm(ALPHA * h + mix, l1g[...], l1b[...])

    def mlp(rows, h1):
        h1b = h1.astype(BF16)
        gate = _sigmoid(_dot(h1b, wg_ref[...]) + bg_ref[...])
        acc = ALPHA * h1 + _dot(p_ref[rows, :].astype(BF16), wp_ref[...]) * gate
        for j in range(D_FF // TAIL_TF):
            cols = slice(j * TAIL_TF, (j + 1) * TAIL_TF)
            a = jnp.maximum(_dot(h1b, wu_ref[:, cols]), 0.0)
            acc = acc + _dot((a * a).astype(BF16), wd_ref[cols, :])
        return acc

    h1 = head(parts[0])
    for r in range(nparts):
        h1_next = head(parts[r + 1]) if r + 1 < nparts else None
        o_ref[parts[r], :] = _layer_norm(mlp(parts[r], h1), l2g[...], l2b[...])
        h1 = h1_next


def _tail(x2, p2, o_gdn, o_fox, lin_g, lin_b, w_out, l1g, l1b, w_up, w_down, w_ple, w_gate,
          b_gate, l2g, l2b):
    n = x2.shape[0]
    row = lambda width: pl.BlockSpec((TAIL_TM, width), lambda i: (i, 0))
    vec = _resident((1, D_MODEL))
    return pl.pallas_call(
        _tail_kernel,
        grid=(n // TAIL_TM,),
        in_specs=[
            row(D_MODEL), row(D_PLE), row(GDN_WIDTH), row(FOX_WIDTH),
            vec, vec, _resident((D_MODEL, D_MODEL)), vec, vec,
            _resident((D_MODEL, D_FF)), _resident((D_FF, D_MODEL)),
            _resident((D_PLE, D_MODEL)), _resident((D_MODEL, D_MODEL)), vec, vec, vec,
        ],
        out_specs=row(D_MODEL),
        out_shape=jax.ShapeDtypeStruct((n, D_MODEL), F32),
        compiler_params=pltpu.CompilerParams(
            dimension_semantics=("arbitrary",), vmem_limit_bytes=VMEM_LIMIT),
        name="tail",
    )(x2, p2, o_gdn, o_fox, lin_g, lin_b, w_out, l1g, l1b, w_up, w_down, w_ple, w_gate,
      b_gate, l2g, l2b)


def kernel(x, p, ln_in_g, ln_in_b, w_in, conv_w, a_log, dt_bias, gdn_norm_g, b_f, fox_norm_g,
           w_out, ln1_g, ln1_b, w_up, w_down, w_ple, w_ple_gate, b_ple_gate, ln2_g, ln2_b):
    batch, seq, _ = x.shape
    assert x.shape[2] == D_MODEL and w_in.shape[0] == 1
    assert seq % FOX_TQ == 0 and seq % GDN_TB == 0 and (batch * seq) % PROJ_TM == 0
    n = batch * seq
    x2 = x.reshape(n, D_MODEL)
    p2 = p[0].reshape(n, D_PLE)
    row = lambda a: a.reshape(1, -1).astype(F32)

    n_gate = LANE_GTOT + GDN_HEADS
    zeros4 = jnp.zeros((GDN_HEADS,), F32)
    pad = jnp.zeros((N_SMALL - n_gate,), F32)
    pvec = jnp.zeros((8, N_SMALL), F32)
    pvec = pvec.at[0].set(jnp.concatenate([zeros4, dt_bias[0], b_f[0], zeros4, dt_bias[0], pad]))
    pvec = pvec.at[1].set(
        jnp.concatenate([zeros4, a_log[0], jnp.zeros((FOX_HEADS,), F32), zeros4, a_log[0], pad]))

    proj_big, small = _proj(x2, row(ln_in_g), row(ln_in_b), jnp.swapaxes(w_in[0], 0, 1), conv_w[0],
                            seq)
    gcol, grow = _gates(small, pvec, batch, seq)

    proj3 = proj_big.reshape(batch, seq, N_BIG)
    gcol3 = gcol.reshape(batch, seq, N_SMALL)
    o_gdn, w_out_b, w_up_b, w_down_b, w_ple_b, w_gate_b = _gdn(
        proj3, gcol3, grow, row(gdn_norm_g[0]), batch, seq,
        (w_out[0], w_up[0], w_down[0], w_ple[0], w_ple_gate[0]))
    o_gdn = o_gdn.reshape(n, GDN_WIDTH)
    o_fox = _fox(proj3, gcol3, row(jnp.tile(fox_norm_g[0], 2)), batch, seq).reshape(n, FOX_WIDTH)

    out = _tail(x2, p2, o_gdn, o_fox, row(ln_in_g), row(ln_in_b), w_out_b,
                row(ln1_g[0]), row(ln1_b[0]), w_up_b, w_down_b, w_ple_b, w_gate_b,
                row(b_ple_gate[0]), row(ln2_g[0]), row(ln2_b[0]))
    return out.reshape(batch, seq, D_MODEL)
```
